```python
import math
import jax, jax.numpy as jnp
from jax import lax
import numpy as np

D_MODEL = 1024
BATCH = 4
SEQ = 8192
DEPTH = 1
DEC_BATCH = 32
DEC_SEQ = 8
PAST_LEN = 16384
PAGE_SIZE = 128

HEAD_DIM = 64
D_MIX = D_MODEL
D_A = D_MIX // 2
D_B = D_MIX - D_A
A_GROUPS = D_A // HEAD_DIM
CHUNK = 128
N_HEADS = D_B // HEAD_DIM
N_KV = 2
GQA = N_HEADS // N_KV
KV_W = N_KV * HEAD_DIM
CMP_STRIDE = 16
CMP_BLOCK = 32
CMP_R = CMP_BLOCK // CMP_STRIDE
CMP_HIDDEN = 256
SEL_BLOCK = 64
N_SEL = 16
WINDOW = 512
N_BUCKETS = 32
MAX_DISTANCE = 128
Q_BLOCK = 128
IN_SIZES = (D_A, D_A, D_A, D_B, 2 * KV_W, 2 * KV_W, 2 * KV_W, 3 * N_HEADS, D_B)
D_IN = sum(IN_SIZES)
RMS_EPS = 1e-6
LN_EPS = 1e-5
NEG = -1e30
FORCE_BONUS = 1e6

kernel_name = 'hymba_gmlp_nsa_decode_step'


def rmsnorm(x, g):
    xf = x.astype(jnp.float32)
    y = xf * lax.rsqrt(jnp.mean(xf * xf, axis=-1, keepdims=True) + RMS_EPS)
    return (y * g.astype(jnp.float32)).astype(x.dtype)


def t5_bucket(dist):
    n = jnp.maximum(dist, 0)
    max_exact = N_BUCKETS // 2
    nf = jnp.maximum(n, max_exact).astype(jnp.float32)
    large = max_exact + (jnp.log(nf / max_exact) / math.log(MAX_DISTANCE / max_exact)
                         * (N_BUCKETS - max_exact)).astype(jnp.int32)
    return jnp.where(n < max_exact, n, jnp.minimum(large, N_BUCKETS - 1))


def bias_qk(rel_table, dist):
    b = rel_table.astype(jnp.float32)[t5_bucket(dist)]
    return b.reshape(dist.shape + (N_KV, GQA)).transpose(0, 2, 3, 1)


def split_proj(x, norm_g, w_in):
    h = rmsnorm(x, norm_g)
    p = jnp.einsum('bsd,de->bse', h, w_in)
    offs = np.cumsum(IN_SIZES)[:-1].tolist()
    u, v, z_a, q, kv_c, kv_s, kv_w, g, z_b = jnp.split(p, offs, axis=-1)
    B, S = x.shape[:2]
    kv = (B, S, 2, N_KV, HEAD_DIM)
    q = q.astype(jnp.float32).reshape(B, S, N_KV, GQA, HEAD_DIM)
    gates = jax.nn.sigmoid(g.astype(jnp.float32)).reshape(B, S, N_KV, GQA, 3)
    return u, v, z_a, q, kv_c.reshape(kv), kv_s.reshape(kv), kv_w.reshape(kv), gates, z_b


def chunk_mlp_prep(u, v, ln_g):
    B, S = u.shape[:2]
    u = jax.nn.gelu(u.astype(jnp.float32)).reshape(B, S, A_GROUPS, HEAD_DIM)
    v = jax.nn.gelu(v.astype(jnp.float32)).reshape(B, S, A_GROUPS, HEAD_DIM)
    mu = jnp.mean(v, axis=-1, keepdims=True)
    var = jnp.mean(jnp.square(v - mu), axis=-1, keepdims=True)
    v = (v - mu) * lax.rsqrt(var + LN_EPS) * ln_g.astype(jnp.float32).reshape(A_GROUPS, HEAD_DIM)
    return u, v


def spatial_gate(u, v, w_s, b_s):
    n = v.shape[2]
    w = jnp.tril(w_s[:, :n, :n].astype(jnp.float32))
    s = jnp.einsum('gts,bcsgd->bctgd', w, v) + b_s[:, :n].astype(jnp.float32).T[None, None, :, :, None]
    return u * s


def compress_blocks(k, pe, w1, b1, w2, b2):
    B, T, G, dh = k.shape
    n_chunks = T // CMP_STRIDE
    nc = n_chunks - CMP_R + 1
    ch = k[:, :n_chunks * CMP_STRIDE].astype(jnp.float32).reshape(B, n_chunks, CMP_STRIDE, G, dh)
    pe = pe.astype(jnp.float32).reshape(CMP_R, CMP_STRIDE, dh)
    w1 = w1.astype(jnp.float32)
    h = b1.astype(jnp.float32)
    for s in range(CMP_R):
        proj = jnp.einsum('bnlgd,ldh->bngh', ch, w1[s])
        h = h + proj[:, s:s + nc] + jnp.einsum('ld,ldh->h', pe[s], w1[s])
    return jnp.einsum('bngh,hd->bngd', jax.nn.gelu(h), w2.astype(jnp.float32)) + b2.astype(jnp.float32)


def compress_kv(kv, pe, w1, b1, w2, b2):
    kc = compress_blocks(kv[:, :, 0], pe[0], w1[0], b1[0], w2[0], b2[0])
    vc = compress_blocks(kv[:, :, 1], pe[1], w1[1], b1[1], w2[1], b2[1])
    return kc, vc


def sel_blocks(kv):
    B, T = kv.shape[:2]
    ns = -(-T // SEL_BLOCK)
    kv = jnp.pad(kv.astype(jnp.float32), ((0, 0), (0, ns * SEL_BLOCK - T), (0, 0), (0, 0), (0, 0)))
    blk = kv.reshape(B, ns, SEL_BLOCK, 2, N_KV, HEAD_DIM).transpose(3, 0, 4, 1, 2, 5)
    return blk[0], blk[1]


def _cover_matrix(nc, ns):
    c0 = np.arange(nc) * CMP_STRIDE
    s0 = np.arange(ns) * SEL_BLOCK
    m = (c0[:, None] < s0[None, :] + SEL_BLOCK) & (c0[:, None] + CMP_BLOCK > s0[None, :])
    return m.astype(np.float32)


def nsa_attend(q, q_pos, gates, kc, vc, ks_blk, vs_blk, kw, vw, w_pos, rel_table):
    scale = HEAD_DIM ** -0.5
    B = q.shape[0]
    qp = q_pos[:, None]
    nc = kc.shape[1]
    c_end = jnp.arange(nc, dtype=jnp.int32) * CMP_STRIDE + (CMP_BLOCK - 1)
    c_mask = (c_end[None, :] <= qp)[:, None, None, :]
    s_c = jnp.einsum('bqgrd,bcgd->bqgrc', q, kc) * scale + bias_qk(rel_table, qp - c_end[None, :])
    p_c = jax.nn.softmax(jnp.where(c_mask, s_c, NEG), axis=-1) * c_mask
    o_c = jnp.einsum('bqgrc,bcgd->bqgrd', p_c, vc)
    ns = ks_blk.shape[2]
    imp = jnp.einsum('bqgc,cs->bqgs', p_c.sum(axis=3), jnp.asarray(_cover_matrix(nc, ns)))
    blk = jnp.arange(ns, dtype=jnp.int32)[None, :]
    cur = (q_pos // SEL_BLOCK)[:, None]
    forced = (blk == 0) | (blk == cur) | (blk == cur - 1)
    imp = imp + jnp.where(forced, FORCE_BONUS, 0.0)[:, None, :]
    imp = jnp.where((blk * SEL_BLOCK <= qp)[:, None, :], imp, NEG)
    top_v, top_i = lax.top_k(imp, min(N_SEL, ns))
    b_idx = jnp.arange(B)[:, None, None, None]
    g_idx = jnp.arange(N_KV)[None, None, :, None]
    ks = ks_blk[b_idx, g_idx, top_i]
    vs = vs_blk[b_idx, g_idx, top_i]
    s_pos = top_i[..., None] * SEL_BLOCK + jnp.arange(SEL_BLOCK, dtype=jnp.int32)
    s_mask = (s_pos <= q_pos[None, :, None, None, None]) & (top_v > NEG / 2)[..., None]
    tbl = rel_table.astype(jnp.float32).reshape(N_BUCKETS, N_KV, GQA).transpose(1, 0, 2)
    s_bias = tbl[g_idx[..., None], t5_bucket(q_pos[None, :, None, None, None] - s_pos)]
    s_bias = s_bias.transpose(0, 1, 2, 5, 3, 4)
    s_s = jnp.einsum('bqgrd,bqgnkd->bqgrnk', q, ks) * scale + s_bias
    s_s = jnp.where(s_mask[:, :, :, None], s_s, NEG)
    sh = s_s.shape
    p_s = jax.nn.softmax(s_s.reshape(sh[:4] + (-1,)), axis=-1).reshape(sh)
    o_s = jnp.einsum('bqgrnk,bqgnkd->bqgrd', p_s, vs)
    dist_w = qp - w_pos[None, :]
    w_mask = ((dist_w >= 0) & (dist_w < WINDOW) & (w_pos[None, :] >= 0))[:, None, None, :]
    s_w = jnp.einsum('bqgrd,bkgd->bqgrk', q, kw) * scale + bias_qk(rel_table, dist_w)
    p_w = jax.nn.softmax(jnp.where(w_mask, s_w, NEG), axis=-1)
    o_w = jnp.einsum('bqgrk,bkgd->bqgrd', p_w, vw)
    o = gates[..., 0:1] * o_c + gates[..., 1:2] * o_s + gates[..., 2:3] * o_w
    return o.reshape(B, q.shape[1], D_B)


def nsa_prompt(q, gates, kc, vc, ks_blk, vs_blk, kv_w, rel_table):
    B, S = q.shape[:2]
    kw_pad = jnp.pad(kv_w.astype(jnp.float32), ((0, 0), (WINDOW, 0), (0, 0), (0, 0), (0, 0)))

    def block(i):
        q0 = i * Q_BLOCK
        qb = lax.dynamic_slice_in_dim(q, q0, Q_BLOCK, axis=1)
        gb = lax.dynamic_slice_in_dim(gates, q0, Q_BLOCK, axis=1)
        wb = lax.dynamic_slice_in_dim(kw_pad, q0, WINDOW + Q_BLOCK, axis=1)
        q_pos = q0 + jnp.arange(Q_BLOCK, dtype=jnp.int32)
        w_pos = q0 - WINDOW + jnp.arange(WINDOW + Q_BLOCK, dtype=jnp.int32)
        return nsa_attend(qb, q_pos, gb, kc, vc, ks_blk, vs_blk, wb[:, :, 0], wb[:, :, 1], w_pos, rel_table)

    out = lax.map(block, jnp.arange(S // Q_BLOCK, dtype=jnp.int32))
    return out.transpose(1, 0, 2, 3).reshape(B, S, D_B)


def merge_out(x, a, z_a, o_b, z_b, w_out):
    B, S = x.shape[:2]
    mix = jnp.concatenate([a.reshape(B, S, D_A) * jax.nn.silu(z_a.astype(jnp.float32)),
                           o_b * jax.nn.silu(z_b.astype(jnp.float32))], axis=-1)
    y = x.astype(jnp.float32) + jnp.einsum('bse,ed->bsd', mix, w_out.astype(jnp.float32))
    return y.astype(x.dtype)


def last_rows(x, n):
    S = x.shape[1]
    if S >= n:
        return x[:, S - n:]
    return jnp.pad(x, ((0, 0), (n - S, 0)) + ((0, 0),) * (x.ndim - 2))


def setup_inputs(seed: int = 0) -> dict:
    key = jax.random.key(seed)
    ks = jax.random.split(key, 20)
    f32 = jnp.float32
    nrm = jax.random.normal
    n_pages = PAST_LEN // PAGE_SIZE
    n_used = DEC_BATCH * n_pages
    n_phys = n_used + n_used // 4
    win_buf = min(WINDOW, PAST_LEN)
    page_table = jax.random.permutation(ks[5], n_phys)[:n_used].reshape(DEC_BATCH, n_pages).astype(jnp.int32)
    return {
        'x_prompt': nrm(ks[0], (BATCH, SEQ, D_MODEL), f32),
        'x_sample': nrm(ks[1], (DEC_BATCH, DEC_SEQ, D_MODEL), f32),
        'cache_cmp_kv': nrm(ks[2], (DEPTH, n_phys, PAGE_SIZE, 2, N_KV, HEAD_DIM), f32),
        'cache_sel_kv': nrm(ks[3], (DEPTH, n_phys, PAGE_SIZE, 2, N_KV, HEAD_DIM), f32),
        'state_win_kv': nrm(ks[4], (DEPTH, DEC_BATCH, win_buf, 2, N_KV, HEAD_DIM), f32),
        'page_table': page_table,
        'norm_g': 1.0 + 0.02 * nrm(ks[6], (DEPTH, D_MODEL), f32),
        'w_in': nrm(ks[7], (DEPTH, D_MODEL, D_IN), f32) * D_MODEL ** -0.5,
        'ln_v_g': 1.0 + 0.02 * nrm(ks[8], (DEPTH, D_A), f32),
        'spatial_w': nrm(ks[9], (DEPTH, A_GROUPS, CHUNK, CHUNK), f32) * CHUNK ** -0.5,
        'spatial_b': 1.0 + 0.02 * nrm(ks[10], (DEPTH, A_GROUPS, CHUNK), f32),
        'cmp_pe': 0.1 * nrm(ks[11], (DEPTH, 2, CMP_BLOCK, HEAD_DIM), f32),
        'cmp_w1': nrm(ks[12], (DEPTH, 2, CMP_R, CMP_STRIDE, HEAD_DIM, CMP_HIDDEN), f32) * (CMP_BLOCK * HEAD_DIM) ** -0.5,
        'cmp_b1': 0.01 * nrm(ks[13], (DEPTH, 2, CMP_HIDDEN), f32),
        'cmp_w2': nrm(ks[14], (DEPTH, 2, CMP_HIDDEN, HEAD_DIM), f32) * CMP_HIDDEN ** -0.5,
        'cmp_b2': 0.01 * nrm(ks[15], (DEPTH, 2, HEAD_DIM), f32),
        'rel_table': 0.5 * nrm(ks[16], (N_BUCKETS, N_HEADS), f32),
        'w_out': nrm(ks[17], (DEPTH, D_MIX, D_MODEL), f32) * D_MIX ** -0.5,
        'final_g': 1.0 + 0.02 * nrm(ks[18], (D_MODEL,), f32),
    }


def reference(x_prompt, x_sample, cache_cmp_kv, cache_sel_kv, state_win_kv, page_table,
              norm_g, w_in, ln_v_g, spatial_w, spatial_b, cmp_pe, cmp_w1, cmp_b1,
              cmp_w2, cmp_b2, rel_table, w_out, final_g):
    n_pages = PAST_LEN // PAGE_SIZE
    win_buf = min(WINDOW, PAST_LEN)
    B, S = x_prompt.shape[:2]
    DB, DS = x_sample.shape[:2]
    xp, xs = x_prompt, x_sample
    cmp_p, sel_p, win_p = [], [], []
    cmp_s, sel_s, win_s, chv_s = [], [], [], []
    for l in range(DEPTH):
        u, v, za, q, kvc, kvs, kvw, gates, zb = split_proj(xp, norm_g[l], w_in[l])
        ua, va = chunk_mlp_prep(u, v, ln_v_g[l])
        a = spatial_gate(ua.reshape(B, S // CHUNK, CHUNK, A_GROUPS, HEAD_DIM),
                         va.reshape(B, S // CHUNK, CHUNK, A_GROUPS, HEAD_DIM),
                         spatial_w[l], spatial_b[l])
        kc, vc = compress_kv(kvc, cmp_pe[l], cmp_w1[l], cmp_b1[l], cmp_w2[l], cmp_b2[l])
        ks_blk, vs_blk = sel_blocks(kvs)
        o = nsa_prompt(q, gates, kc, vc, ks_blk, vs_blk, kvw, rel_table)
        xp = merge_out(xp, a, za, o, zb, w_out[l])
        cmp_p.append(kvc)
        sel_p.append(kvs)
        win_p.append(last_rows(kvw, win_buf))
        u, v, za, q, kvc, kvs, kvw, gates, zb = split_proj(xs, norm_g[l], w_in[l])
        ua, va = chunk_mlp_prep(u, v, ln_v_g[l])
        a = spatial_gate(ua[:, None], va[:, None], spatial_w[l], spatial_b[l])[:, 0]
        past_c = cache_cmp_kv[l][page_table].reshape(DB, n_pages * PAGE_SIZE, 2, N_KV, HEAD_DIM)
        past_s = cache_sel_kv[l][page_table].reshape(DB, n_pages * PAGE_SIZE, 2, N_KV, HEAD_DIM)
        full_c = jnp.concatenate([past_c, kvc.astype(past_c.dtype)], axis=1)
        full_s = jnp.concatenate([past_s, kvs.astype(past_s.dtype)], axis=1)
        kc, vc = compress_kv(full_c, cmp_pe[l], cmp_w1[l], cmp_b1[l], cmp_w2[l], cmp_b2[l])
        ks_blk, vs_blk = sel_blocks(full_s)
        win = jnp.concatenate([state_win_kv[l], kvw.astype(state_win_kv.dtype)], axis=1)
        w_pos = PAST_LEN - win_buf + jnp.arange(win_buf + DS, dtype=jnp.int32)
        q_pos = PAST_LEN + jnp.arange(DS, dtype=jnp.int32)
        winf = win.astype(jnp.float32)
        o = nsa_attend(q, q_pos, gates, kc, vc, ks_blk, vs_blk, winf[:, :, 0], winf[:, :, 1], w_pos, rel_table)
        xs = merge_out(xs, a, za, o, zb, w_out[l])
        cmp_s.append(kvc)
        sel_s.append(kvs)
        win_s.append(win[:, DS:])
        chv_s.append(va.reshape(DB, DS, D_A).astype(x_sample.dtype))
    y_prompt = rmsnorm(xp, final_g)
    y_sample = rmsnorm(xs, final_g)
    new_cmp_kv_prompt = jnp.stack(cmp_p)
    new_sel_kv_prompt = jnp.stack(sel_p)
    new_win_kv_prompt = jnp.stack(win_p)
    new_cmp_kv_sample = jnp.stack(cmp_s)
    new_sel_kv_sample = jnp.stack(sel_s)
    new_win_kv_sample = jnp.stack(win_s)
    new_chunk_v_sample = jnp.stack(chv_s)
    return (y_prompt, y_sample, new_cmp_kv_prompt, new_sel_kv_prompt, new_win_kv_prompt,
            new_cmp_kv_sample, new_sel_kv_sample, new_win_kv_sample, new_chunk_v_sample)
```

```python
import functools
import math

import numpy as np
import jax
import jax.numpy as jnp
from jax import lax
from jax.experimental import pallas as pl
from jax.experimental.pallas import tpu as pltpu

F32 = jnp.float32
BF16 = jnp.bfloat16

D_MODEL = 1024
HEAD_DIM = 64
D_A = 512
D_B = 512
A_GROUPS = D_A // HEAD_DIM
CHUNK = 128
N_HEADS = D_B // HEAD_DIM
N_KV = 2
GQA = N_HEADS // N_KV
KV_W = N_KV * HEAD_DIM
KV_ROW = 2 * KV_W
CMP_STRIDE = 16
CMP_BLOCK = 32
CMP_R = CMP_BLOCK // CMP_STRIDE
CMP_HIDDEN = 256
CMP_FLAT = CMP_STRIDE * HEAD_DIM
SEL_BLOCK = 64
N_SEL = 16
WINDOW = 512
N_BUCKETS = 32
MAX_DISTANCE = 128
Q_BLOCK = 128
PAGE_SIZE = 128
RMS_EPS = 1e-6
LN_EPS = 1e-5
NEG = -1e30
FORCE_BONUS = 1e6
Q_SCALE = HEAD_DIM ** -0.5
FAR_DIST = 1 << 20

_OFF_U, _OFF_V, _OFF_ZA, _OFF_Q = 0, 512, 1024, 1536
_OFF_KVC, _OFF_KVS, _OFF_KVW, _OFF_ZB, _OFF_G = 2048, 2304, 2560, 2816, 3328
D_IN_PAD = 3456
GATE_PAD = 128

VMEM_LIMIT = 52 * 1024 * 1024


def _gelu(x):
    return x * (0.5 * (1.0 + jnp.tanh(0.7978845608028654 * (x + 0.044715 * (x * x * x)))))


def _sigmoid(x):
    return 1.0 / (1.0 + jnp.exp(-x))


def _dot(a, b):
    return jnp.dot(a, b, preferred_element_type=F32)


def _dot_t(a, b):
    return lax.dot_general(a, b, (((0,), (0,)), ((), ())), preferred_element_type=F32)


def _split_dot(a, b):
    hi = a.astype(BF16)
    lo = (a - hi.astype(F32)).astype(BF16)
    return _dot(hi, b) + _dot(lo, b)


def _split_dot_l(a, b):
    hi = b.astype(BF16)
    lo = (b - hi.astype(F32)).astype(BF16)
    return _dot(a, hi) + _dot(a, lo)


def _inproj_kernel(x_ref, ng_ref, w_ref, lng_ref, pavg_ref,
                   ua_ref, vn_ref, zas_ref, q_ref, kvc_ref, kvs_ref, kvw_ref, gt_ref, zbs_ref):
    x = x_ref[...]
    ms = jnp.mean(x * x, axis=-1, keepdims=True)
    h = (x * lax.rsqrt(ms + RMS_EPS) * ng_ref[...]).astype(BF16)

    def proj(a, b):
        return _dot(h, w_ref[:, a:b])

    ua_ref[...] = _gelu(proj(_OFF_U, _OFF_V))
    v = _gelu(proj(_OFF_V, _OFF_ZA))
    mu = _split_dot(v, pavg_ref[...])
    d = v - mu
    var = _split_dot(d * d, pavg_ref[...])
    vn_ref[...] = d * lax.rsqrt(var + LN_EPS) * lng_ref[...]
    za = proj(_OFF_ZA, _OFF_Q)
    zas_ref[...] = za * _sigmoid(za)
    q_ref[...] = (proj(_OFF_Q, _OFF_KVC) * Q_SCALE).astype(BF16)
    kvc_ref[...] = proj(_OFF_KVC, _OFF_KVS)
    kvs_ref[...] = proj(_OFF_KVS, _OFF_KVW)
    kvw_ref[...] = proj(_OFF_KVW, _OFF_ZB)
    zb = proj(_OFF_ZB, _OFF_G)
    zbs_ref[...] = zb * _sigmoid(zb)
    gt_ref[...] = _sigmoid(proj(_OFF_G, D_IN_PAD))


def _inproj(x, norm_g, w_perm, ln_g, pavg):
    n = x.shape[0]
    tm = min(256, n)
    row = lambda w: pl.BlockSpec((tm, w), lambda i: (i, 0))
    full = lambda a: pl.BlockSpec(a.shape, lambda i: (0,) * a.ndim)
    out_w = [(D_A, F32), (D_A, F32), (D_A, F32), (D_B, BF16), (KV_ROW, F32), (KV_ROW, F32),
             (KV_ROW, F32), (GATE_PAD, F32), (D_B, F32)]
    return pl.pallas_call(
        _inproj_kernel,
        grid=(n // tm,),
        in_specs=[row(D_MODEL), full(norm_g), full(w_perm), full(ln_g), full(pavg)],
        out_specs=[row(w) for w, _ in out_w],
        out_shape=[jax.ShapeDtypeStruct((n, w), dt) for w, dt in out_w],
        compiler_params=pltpu.CompilerParams(dimension_semantics=("parallel",),
                                             vmem_limit_bytes=VMEM_LIMIT),
        name="inproj",
    )(x, norm_g, w_perm, ln_g, pavg)


def _mixout_kernel(x_ref, ua_ref, vn_ref, zas_ref, ob_ref, zbs_ref, wsp_ref, bsp_ref, wo_ref, fg_ref,
                   y_ref, s_ref):
    tm = x_ref.shape[0]
    for c in range(tm // CHUNK):
        rows = slice(c * CHUNK, (c + 1) * CHUNK)
        vc = vn_ref[rows, :].astype(BF16)
        for g in range(A_GROUPS):
            cols = slice(g * HEAD_DIM, (g + 1) * HEAD_DIM)
            s_ref[rows, cols] = _dot(wsp_ref[g], vc[:, cols])
        s_ref[rows, :] = s_ref[rows, :] + bsp_ref[...]
    mix_a = (ua_ref[...] * s_ref[...] * zas_ref[...]).astype(BF16)
    mix_b = (ob_ref[...] * zbs_ref[...]).astype(BF16)
    y = x_ref[...] + _dot(mix_a, wo_ref[0:D_A, :]) + _dot(mix_b, wo_ref[D_A:D_A + D_B, :])
    ms = jnp.mean(y * y, axis=-1, keepdims=True)
    y_ref[...] = y * lax.rsqrt(ms + RMS_EPS) * fg_ref[...]


def _mixout(x, ua, vn, zas, ob, zbs, wsp, bsp, wo, fg):
    n = x.shape[0]
    tm = min(256, n)
    row = lambda w: pl.BlockSpec((tm, w), lambda i: (i, 0))
    full = lambda a: pl.BlockSpec(a.shape, lambda i: (0,) * a.ndim)
    return pl.pallas_call(
        _mixout_kernel,
        grid=(n // tm,),
        in_specs=[row(D_MODEL), row(D_A), row(D_A), row(D_A), row(D_B), row(D_B),
                  full(wsp), full(bsp), full(wo), full(fg)],
        out_specs=row(D_MODEL),
        out_shape=jax.ShapeDtypeStruct((n, D_MODEL), F32),
        scratch_shapes=[pltpu.VMEM((tm, D_A), F32)],
        compiler_params=pltpu.CompilerParams(dimension_semantics=("parallel",),
                                             vmem_limit_bytes=VMEM_LIMIT),
        name="mixout",
    )(x, ua, vn, zas, ob, zbs, wsp, bsp, wo, fg)


def _t5_bucket_np(dist):
    dist = np.asarray(dist, np.int64)
    n = np.maximum(dist, 0)
    max_exact = N_BUCKETS // 2
    nf = np.maximum(n, max_exact).astype(np.float64)
    large = max_exact + (np.log(nf / max_exact) / math.log(MAX_DISTANCE / max_exact)
                         * (N_BUCKETS - max_exact)).astype(np.int64)
    b = np.where(n < max_exact, n, np.minimum(large, N_BUCKETS - 1))
    return np.where(dist < 0, -1, b).astype(np.int32)


def _bias_kernel(tbl_ref, bkt_ref, out_ref, *, qb):
    b = bkt_ref[...]
    grp = lax.broadcasted_iota(jnp.int32, (1, b.shape[1]), 1) // qb
    for g in range(N_KV):
        acc = jnp.full(b.shape, NEG, F32)
        for k in range(N_BUCKETS):
            row = jnp.zeros(grp.shape, F32)
            for r in range(GQA):
                row = jnp.where(grp == r, tbl_ref[k * N_HEADS + g * GQA + r], row)
            acc = jnp.where(b == k, row, acc)
        out_ref[g] = acc


def _bias_tiles(rel_table, bkt, qb):
    r, c = bkt.shape
    rb = r
    for cand in (512, 256, 128, 64, 32, 16, 8):
        if r % cand == 0:
            rb = cand
            break
    return pl.pallas_call(
        functools.partial(_bias_kernel, qb=qb),
        grid=(r // rb,),
        in_specs=[pl.BlockSpec(memory_space=pltpu.SMEM), pl.BlockSpec((rb, c), lambda i: (i, 0))],
        out_specs=pl.BlockSpec((N_KV, rb, c), lambda i: (0, i, 0)),
        out_shape=jax.ShapeDtypeStruct((N_KV, r, c), F32),
        compiler_params=pltpu.CompilerParams(dimension_semantics=("parallel",)),
        name="bias_tiles",
    )(rel_table.reshape(-1), jnp.asarray(bkt))


def _compress_kernel(tbl_ref, *refs, pg):
    del tbl_ref
    page_refs = refs[:pg + 1]
    w1_ref, pe_ref, b1_ref, w2_ref, b2_ref, kc_ref, vc_ref, x_ref = refs[pg + 1:]
    mp = (pg + 1) * 8
    nrow = pg * 8
    left = lax.broadcasted_iota(jnp.int32, (8, 128), 1) < HEAD_DIM
    for k, pr in enumerate(page_refs):
        for t in range(CMP_STRIDE // 2):
            for kv in range(2):
                e = pr[0, pl.ds(4 * t + kv, 8, stride=2 * CMP_STRIDE), :]
                o = pr[0, pl.ds(4 * t + 2 + kv, 8, stride=2 * CMP_STRIDE), :]
                x_ref[kv, 8 * k:8 * k + 8, 128 * t:128 * t + 128] = jnp.where(
                    left, e, pltpu.roll(o, HEAD_DIM, 1))
                x_ref[kv, mp + 8 * k:mp + 8 * k + 8, 128 * t:128 * t + 128] = jnp.where(
                    left, pltpu.roll(e, HEAD_DIM, 1), o)
    out_refs = (kc_ref, vc_ref)
    for kv in range(2):
        x_ref[kv, 2 * mp:2 * mp + 8, :] = pe_ref[kv]
        p = _dot(x_ref[kv].astype(BF16), w1_ref[kv])
        hc = (b1_ref[kv] + p[2 * mp:2 * mp + 1, 0:CMP_HIDDEN]
              + p[2 * mp + 1:2 * mp + 2, CMP_HIDDEN:2 * CMP_HIDDEN])
        for g in range(N_KV):
            base = g * mp
            h = (p[base:base + nrow, 0:CMP_HIDDEN]
                 + p[base + 1:base + nrow + 1, CMP_HIDDEN:2 * CMP_HIDDEN] + hc)
            out_refs[kv][0, g] = _dot(_gelu(h).astype(BF16), w2_ref[kv]) + b2_ref[kv]


def _compress(table, pages, w1cat, pe8, b1, w2, b2):
    nb, npg = table.shape
    pg = min(32, npg)
    assert npg % pg == 0
    mp = (pg + 1) * 8

    def page_spec(k):
        return pl.BlockSpec(
            (1, 2 * PAGE_SIZE, KV_W),
            lambda b, j, tbl: (tbl[b, jnp.minimum(j * pg + k, npg - 1)], 0, 0))

    full = lambda a: pl.BlockSpec(a.shape, lambda b, j, tbl: (0,) * a.ndim)
    out_spec = pl.BlockSpec((1, N_KV, pg * 8, HEAD_DIM), lambda b, j, tbl: (b, 0, j, 0))
    out_sds = jax.ShapeDtypeStruct((nb, N_KV, npg * 8, HEAD_DIM), F32)
    grid_spec = pltpu.PrefetchScalarGridSpec(
        num_scalar_prefetch=1,
        grid=(nb, npg // pg),
        in_specs=[page_spec(k) for k in range(pg + 1)] + [full(a) for a in (w1cat, pe8, b1, w2, b2)],
        out_specs=[out_spec, out_spec],
        scratch_shapes=[pltpu.VMEM((2, 2 * mp + 8, CMP_FLAT), F32)],
    )
    return pl.pallas_call(
        functools.partial(_compress_kernel, pg=pg),
        grid_spec=grid_spec,
        out_shape=[out_sds, out_sds],
        compiler_params=pltpu.CompilerParams(dimension_semantics=("parallel", "parallel"),
                                             vmem_limit_bytes=VMEM_LIMIT),
        name="compress",
    )(table, *([pages] * (pg + 1)), w1cat, pe8, b1, w2, b2)


def _topk_rows(imp, n_sel):
    ns = imp.shape[0]
    blk = lax.broadcasted_iota(jnp.int32, imp.shape, 0).astype(F32)
    sel = jnp.zeros(imp.shape, F32)
    for _ in range(n_sel):
        mx = jnp.max(imp, axis=0, keepdims=True)
        idx = jnp.min(jnp.where(imp == mx, blk, float(ns)), axis=0, keepdims=True)
        hit = blk == idx
        sel = jnp.where(hit, 1.0, sel)
        imp = jnp.where(hit, -jnp.inf, imp)
    return sel


def _select_mask(imp, qpos):
    blk = lax.broadcasted_iota(jnp.int32, imp.shape, 0)
    cur = qpos // SEL_BLOCK
    forced = (blk == 0) | (blk == cur) | (blk == cur - 1)
    valid = blk * SEL_BLOCK <= qpos
    imp = imp + jnp.where(forced, FORCE_BONUS, 0.0)
    imp = jnp.where(valid, imp, NEG)
    sel = _topk_rows(imp, N_SEL)
    return jnp.where((sel > 0.5) & valid, 0.0, NEG)


def _softmax_rows(s):
    m = jnp.max(s, axis=0, keepdims=True)
    e = jnp.exp(s - m)
    inv = jnp.where(m > NEG / 2, 1.0 / jnp.sum(e, axis=0, keepdims=True), 0.0)
    return e * inv


def _online_update(s, v_dot, m, l, acc):
    m_new = jnp.maximum(m, jnp.max(s, axis=0, keepdims=True))
    alpha = jnp.exp(m - m_new)
    p = jnp.exp(s - m_new)
    l = alpha * l + jnp.sum(p, axis=0, keepdims=True)
    acc = alpha * acc + v_dot(p.astype(BF16))
    return m_new, l, acc


def _nsa_prompt_kernel(q_ref, gt_ref, kc_ref, vct_ref, ks_ref, vst_ref, kw_ref, vwt_ref,
                       cband_ref, far_ref, seld_ref, wbias_ref, cover_ref,
                       o_ref, bias_ref, msk_ref):
    i = pl.program_id(2)
    q = q_ref[0, 0, 0]
    ncp = kc_ref.shape[2]
    far = far_ref[0]
    qpos = i * Q_BLOCK + lax.broadcasted_iota(jnp.int32, (1, Q_BLOCK), 1)

    c0 = pl.multiple_of(jnp.maximum(i * 8 - 8, 0), 8)
    crow = lax.broadcasted_iota(jnp.int32, (ncp, 1), 0)
    bias_ref[...] = jnp.where(crow < c0, far, NEG)
    bias_ref[pl.ds(c0, 16), :] = cband_ref[0, jnp.minimum(i, 1)]
    p_c = _softmax_rows(_dot(kc_ref[0, 0], q) + bias_ref[...])
    o_c = _dot(vct_ref[0, 0], p_c.astype(BF16))
    psum = (p_c[:, 0:Q_BLOCK] + p_c[:, Q_BLOCK:2 * Q_BLOCK]
            + p_c[:, 2 * Q_BLOCK:3 * Q_BLOCK] + p_c[:, 3 * Q_BLOCK:4 * Q_BLOCK])
    imp = _split_dot_l(cover_ref[...], psum)
    mask = _select_mask(imp, qpos)
    msk_ref[...] = jnp.concatenate([mask] * GQA, axis=1)

    def sel_step(kb, carry):
        m, l, acc = carry
        off = pl.multiple_of(kb * CHUNK, CHUNK)
        s = _dot(ks_ref[0, 0, pl.ds(off, CHUNK), :], q)
        s = s + seld_ref[0, jnp.clip(kb - i + 2, 0, 2)]
        m0 = msk_ref[pl.ds(2 * kb, 1), :]
        m1 = msk_ref[pl.ds(2 * kb + 1, 1), :]
        half = lax.broadcasted_iota(jnp.int32, (CHUNK, 1), 0) < SEL_BLOCK
        s = s + jnp.where(half, m0, m1)
        vt = vst_ref[0, 0, :, pl.ds(off, CHUNK)]
        return _online_update(s, lambda p: _dot(vt, p), m, l, acc)

    init = (jnp.full((1, GQA * Q_BLOCK), -jnp.inf, F32), jnp.zeros((1, GQA * Q_BLOCK), F32),
            jnp.zeros((HEAD_DIM, GQA * Q_BLOCK), F32))
    _, l_s, acc_s = lax.fori_loop(0, i + 1, sel_step, init)
    o_s = acc_s * (1.0 / l_s)

    woff = pl.multiple_of(i * Q_BLOCK, Q_BLOCK)
    nwk = WINDOW + Q_BLOCK
    s_w = _dot(kw_ref[0, 0, pl.ds(woff, nwk), :], q) + wbias_ref[0]
    wrow = lax.broadcasted_iota(jnp.int32, (nwk, 1), 0)
    s_w = jnp.where(wrow + (i * Q_BLOCK - WINDOW) >= 0, s_w, NEG)
    p_w = _softmax_rows(s_w)
    o_w = _dot(vwt_ref[0, 0, :, pl.ds(woff, nwk)], p_w.astype(BF16))

    g = gt_ref[0, 0, 0]
    o_ref[0, 0, 0] = g[0:1, :] * o_c + g[1:2, :] * o_s + g[2:3, :] * o_w


def _nsa_prompt(qt, gt, kc, vct, ks, vst, kw, vwt, cband, far, seld, wbias, cover_t):
    bsz, nblk = qt.shape[:2]
    ncp = kc.shape[2]
    ns = cover_t.shape[0]
    cols = GQA * Q_BLOCK
    per_bg = lambda a: pl.BlockSpec((1, 1) + a.shape[2:], lambda b, g, i: (b, g, 0, 0))
    per_g = lambda a: pl.BlockSpec((1,) + a.shape[1:], lambda b, g, i: (g,) + (0,) * (a.ndim - 1))
    blk5 = lambda a: pl.BlockSpec((1, 1, 1) + a.shape[3:], lambda b, g, i: (b, i, g, 0, 0))
    return pl.pallas_call(
        _nsa_prompt_kernel,
        grid=(bsz, N_KV, nblk),
        in_specs=[blk5(qt), blk5(gt), per_bg(kc), per_bg(vct), per_bg(ks), per_bg(vst),
                  per_bg(kw), per_bg(vwt), per_g(cband), per_g(far), per_g(seld), per_g(wbias),
                  pl.BlockSpec(cover_t.shape, lambda b, g, i: (0, 0))],
        out_specs=pl.BlockSpec((1, 1, 1, HEAD_DIM, cols), lambda b, g, i: (b, i, g, 0, 0)),
        out_shape=jax.ShapeDtypeStruct((bsz, nblk, N_KV, HEAD_DIM, cols), F32),
        scratch_shapes=[pltpu.VMEM((ncp, cols), F32), pltpu.VMEM((ns, cols), F32)],
        compiler_params=pltpu.CompilerParams(
            dimension_semantics=("parallel", "parallel", "arbitrary"),
            vmem_limit_bytes=VMEM_LIMIT),
        name="nsa_prompt",
    )(qt, gt, kc, vct, ks, vst, kw, vwt, cband, far, seld, wbias, cover_t)


def _nsa_sample_kernel(tbl_ref, *refs, pgs, npg, past, ds):
    del tbl_ref
    page_refs = refs[:pgs]
    (qbd_ref, gt_ref, kc_ref, vc_ref, win_ref, new_ref, cbias_ref, wbias_ref, sfar_ref, slast_ref,
     snew_ref, cover_ref, rsum_ref, o_ref, msk_ref, m_ref, l_ref, acc_ref, ocw_ref) = refs[pgs:]
    j = pl.program_id(1)
    qbd = qbd_ref[0]
    ncol = qbd.shape[1]
    qcols = ncol // (N_KV * GQA)
    g = gt_ref[0]

    @pl.when(j == 0)
    def _():
        p_c = _softmax_rows(_dot(kc_ref[0], qbd) + cbias_ref[...])
        o_c = _dot_t(vc_ref[0], p_c.astype(BF16))
        imp = _split_dot(_split_dot_l(cover_ref[...], p_c), rsum_ref[...])
        lane = lax.broadcasted_iota(jnp.int32, (1, ncol), 1)
        qpos = past + (lane % qcols) % ds
        msk_ref[...] = _select_mask(imp, qpos)
        kw = win_ref[0, :, 0:KV_W].astype(BF16)
        vw = win_ref[0, :, KV_W:KV_ROW].astype(BF16)
        p_w = _softmax_rows(_dot(kw, qbd) + wbias_ref[...])
        o_w = _dot_t(vw, p_w.astype(BF16))
        ocw_ref[...] = g[0:1, :] * o_c + g[2:3, :] * o_w
        m_ref[...] = jnp.full(m_ref.shape, -jnp.inf, F32)
        l_ref[...] = jnp.zeros(l_ref.shape, F32)
        acc_ref[...] = jnp.zeros(acc_ref.shape, F32)

    half = lax.broadcasted_iota(jnp.int32, (PAGE_SIZE, 1), 0) < SEL_BLOCK
    carry = (m_ref[...], l_ref[...], acc_ref[...])
    for k, pr in enumerate(page_refs):
        pidx = j * pgs + k
        kk = pr[0, :, 0:KV_W].astype(BF16)
        vv = pr[0, :, KV_W:KV_ROW].astype(BF16)
        s = _dot(kk, qbd)
        s = s + jnp.where(pidx == npg - 1, slast_ref[...], sfar_ref[...])
        s = s + jnp.where(half, msk_ref[pl.ds(2 * pidx, 1), :], msk_ref[pl.ds(2 * pidx + 1, 1), :])
        carry = _online_update(s, lambda p, vv=vv: _dot_t(vv, p), *carry)
    m_ref[...], l_ref[...], acc_ref[...] = carry

    @pl.when(j == pl.num_programs(1) - 1)
    def _():
        kn = new_ref[0, :, 0:KV_W].astype(BF16)
        vn = new_ref[0, :, KV_W:KV_ROW].astype(BF16)
        s = _dot(kn, qbd) + snew_ref[...] + msk_ref[pl.ds(2 * npg, 1), :]
        _, l, acc = _online_update(s, lambda p: _dot_t(vn, p), m_ref[...], l_ref[...], acc_ref[...])
        o = ocw_ref[...] + g[1:2, :] * (acc * (1.0 / l))
        for gi in range(N_KV):
            o_ref[0, gi] = o[gi * HEAD_DIM:(gi + 1) * HEAD_DIM, gi * GQA * qcols:(gi + 1) * GQA * qcols]


def _nsa_sample(table, pages, qbd, gt, kc, vc, win, new, cbias, wbias, sfar, slast, snew,
                cover_t, rsum, past, ds):
    nb, npg = table.shape
    pgs = min(16, npg)
    assert npg % pgs == 0
    ncol = qbd.shape[2]
    ns = cover_t.shape[0]

    def page_spec(k):
        return pl.BlockSpec((1, PAGE_SIZE, KV_ROW), lambda b, j, tbl: (tbl[b, j * pgs + k], 0, 0))

    per_b = lambda a: pl.BlockSpec((1,) + a.shape[1:], lambda b, j, tbl: (b,) + (0,) * (a.ndim - 1))
    full = lambda a: pl.BlockSpec(a.shape, lambda b, j, tbl: (0,) * a.ndim)
    grid_spec = pltpu.PrefetchScalarGridSpec(
        num_scalar_prefetch=1,
        grid=(nb, npg // pgs),
        in_specs=[page_spec(k) for k in range(pgs)]
        + [per_b(a) for a in (qbd, gt, kc, vc, win, new)]
        + [full(a) for a in (cbias, wbias, sfar, slast, snew, cover_t, rsum)],
        out_specs=pl.BlockSpec((1, N_KV, HEAD_DIM, ncol // N_KV), lambda b, j, tbl: (b, 0, 0, 0)),
        scratch_shapes=[pltpu.VMEM((ns, ncol), F32), pltpu.VMEM((1, ncol), F32),
                        pltpu.VMEM((1, ncol), F32), pltpu.VMEM((KV_W, ncol), F32),
                        pltpu.VMEM((KV_W, ncol), F32)],
    )
    return pl.pallas_call(
        functools.partial(_nsa_sample_kernel, pgs=pgs, npg=npg, past=past, ds=ds),
        grid_spec=grid_spec,
        out_shape=jax.ShapeDtypeStruct((nb, N_KV, HEAD_DIM, ncol // N_KV), F32),
        compiler_params=pltpu.CompilerParams(dimension_semantics=("parallel", "arbitrary"),
                                             vmem_limit_bytes=VMEM_LIMIT),
        name="nsa_sample",
    )(table, *([pages] * pgs), qbd, gt, kc, vc, win, new, cbias, wbias, sfar, slast, snew,
      cover_t, rsum)


def _cover_t(nc, ns, nc_pad, ns_pad):
    c0 = np.arange(nc) * CMP_STRIDE
    s0 = np.arange(ns) * SEL_BLOCK
    m = (c0[None, :] < s0[:, None] + SEL_BLOCK) & (c0[None, :] + CMP_BLOCK > s0[:, None])
    out = np.zeros((ns_pad, nc_pad), np.float32)
    out[:ns, :nc] = m
    return jnp.asarray(out, BF16)


def _prep_weights(norm_g, w_in, ln_v_g, spatial_w, spatial_b, cmp_pe, cmp_w1, cmp_b1, cmp_w2, cmp_b2,
                  w_out, final_g):
    offs = np.cumsum((D_A, D_A, D_A, D_B, 2 * KV_W, 2 * KV_W, 2 * KV_W, 3 * N_HEADS, D_B))
    g0, g1 = int(offs[6]), int(offs[7])
    w_perm = jnp.concatenate(
        [w_in[:, :g0], w_in[:, g1:], w_in[:, g0:g1],
         jnp.zeros((D_MODEL, GATE_PAD - 3 * N_HEADS), w_in.dtype)], axis=1).astype(BF16)
    pavg = jnp.asarray(np.kron(np.eye(A_GROUPS), np.full((HEAD_DIM, HEAD_DIM), 1.0 / HEAD_DIM)), BF16)
    w1cat = jnp.concatenate([cmp_w1[:, s].reshape(2, CMP_FLAT, CMP_HIDDEN) for s in range(CMP_R)],
                            axis=2).astype(BF16)
    pe8 = jnp.concatenate([cmp_pe.reshape(2, CMP_R, CMP_FLAT),
                           jnp.zeros((2, 8 - CMP_R, CMP_FLAT), F32)], axis=1)
    return dict(
        norm_g=norm_g.reshape(1, D_MODEL), w_perm=w_perm, ln_g=ln_v_g.reshape(1, D_A), pavg=pavg,
        w1cat=w1cat, pe8=pe8, b1=cmp_b1.reshape(2, 1, CMP_HIDDEN), w2=cmp_w2.astype(BF16),
        b2=cmp_b2.reshape(2, 1, HEAD_DIM), wo=w_out.astype(BF16), fg=final_g.reshape(1, D_MODEL),
        spatial_w=spatial_w, spatial_b=spatial_b)


def _spatial_operands(spatial_w, spatial_b, n):
    reps = CHUNK // n
    w = jnp.tril(spatial_w[:, :n, :n])
    eye = jnp.eye(reps, dtype=w.dtype)
    wsp = jnp.einsum('ab,gts->gatbs', eye, w).reshape(A_GROUPS, CHUNK, CHUNK).astype(BF16)
    b = jnp.tile(spatial_b[:, :n].T, (reps, 1))
    bsp = jnp.repeat(b, HEAD_DIM, axis=1)
    return wsp, bsp


def _prompt_bias_buckets():
    ql = np.arange(Q_BLOCK)[None, :]
    cl = np.arange(16)[:, None]
    band = np.stack([ql - CMP_STRIDE * cl - (CMP_BLOCK - 1),
                     ql + 97 - CMP_STRIDE * cl])
    kl = np.arange(CHUNK)[:, None]
    seld = np.stack([np.full((CHUNK, Q_BLOCK), FAR_DIST), CHUNK + ql - kl, ql - kl])
    wl = np.arange(WINDOW + Q_BLOCK)[:, None]
    dw = ql + WINDOW - wl
    dw = np.where(dw < WINDOW, dw, -1)
    far = np.full((8, Q_BLOCK), FAR_DIST)
    tile4 = lambda d: np.tile(_t5_bucket_np(d.reshape(-1, Q_BLOCK)), (1, GQA))
    return tile4(band), tile4(seld), tile4(dw), tile4(far)


def _sample_bias_buckets(past, ds, qpad, ncp, nwin_pad):
    ql = (np.arange(qpad) % ds)[None, :]
    c = np.arange(ncp)[:, None]
    dc = past + ql - (CMP_STRIDE * c + CMP_BLOCK - 1)
    wl = np.arange(nwin_pad)[:, None]
    dw = WINDOW + ql - wl
    dw = np.where((dw < WINDOW) & (wl < WINDOW + ds), dw, -1)
    kl = np.arange(PAGE_SIZE)[:, None]
    dlast = PAGE_SIZE + ql - kl
    nl = np.arange(16)[:, None]
    dnew = np.where(nl < ds, ql - nl, -1)
    far = np.full((8, qpad), FAR_DIST)
    tile4 = lambda d: np.tile(_t5_bucket_np(d), (1, GQA))
    return tile4(dc), tile4(dw), tile4(dlast), tile4(dnew), tile4(far)


def _group_layer(x, wts, n_seq):
    return _inproj(x, wts['norm_g'], wts['w_perm'], wts['ln_g'], wts['pavg'])


def kernel(x_prompt, x_sample, cache_cmp_kv, cache_sel_kv, state_win_kv, page_table, norm_g, w_in,
           ln_v_g, spatial_w, spatial_b, cmp_pe, cmp_w1, cmp_b1, cmp_w2, cmp_b2, rel_table, w_out,
           final_g):
    depth = norm_g.shape[0]
    assert depth == 1
    bsz, seq = x_prompt.shape[:2]
    db, ds = x_sample.shape[:2]
    npg = page_table.shape[1]
    past = npg * PAGE_SIZE
    win_buf = state_win_kv.shape[2]
    assert win_buf == WINDOW and past >= WINDOW and seq % Q_BLOCK == 0 and seq >= WINDOW
    assert CHUNK % ds == 0 and ds <= 8 and (db * ds) % CHUNK == 0
    nblk = seq // Q_BLOCK
    l = 0
    wts = _prep_weights(norm_g[l], w_in[l], ln_v_g[l], spatial_w[l], spatial_b[l], cmp_pe[l],
                        cmp_w1[l], cmp_b1[l], cmp_w2[l], cmp_b2[l], w_out[l], final_g)

    n_p = bsz * seq
    xp = x_prompt.reshape(n_p, D_MODEL)
    ua, vn, zas, q, kvc, kvs, kvw, gt, zbs = _group_layer(xp, wts, bsz)

    ident = jnp.arange(bsz * nblk, dtype=jnp.int32).reshape(bsz, nblk)
    kc, vc = _compress(ident, kvc.reshape(bsz * nblk, 2 * PAGE_SIZE, KV_W), wts['w1cat'], wts['pe8'],
                       wts['b1'], wts['w2'], wts['b2'])
    ncp = seq // CMP_STRIDE
    cover_p = _cover_t(ncp - CMP_R + 1, seq // SEL_BLOCK, ncp, seq // SEL_BLOCK)

    band_b, seld_b, win_b, far_b = _prompt_bias_buckets()
    cols = GQA * Q_BLOCK
    cband = _bias_tiles(rel_table, band_b, Q_BLOCK).reshape(N_KV, 2, 16, cols)
    seld = _bias_tiles(rel_table, seld_b, Q_BLOCK).reshape(N_KV, 3, CHUNK, cols)
    wbias = _bias_tiles(rel_table, win_b, Q_BLOCK)
    far = _bias_tiles(rel_table, far_b, Q_BLOCK)[:, 0:1]

    def heads_t(a):
        a = a.reshape(bsz, seq, 2, N_KV, HEAD_DIM).astype(BF16)
        return a[:, :, 0].transpose(0, 2, 1, 3), a[:, :, 1].transpose(0, 2, 3, 1)

    ks, vst = heads_t(kvs)
    kw, vwt = heads_t(kvw)
    kw = jnp.pad(kw, ((0, 0), (0, 0), (WINDOW, 0), (0, 0)))
    vwt = jnp.pad(vwt, ((0, 0), (0, 0), (0, 0), (WINDOW, 0)))
    qt = (q.reshape(bsz, nblk, Q_BLOCK, N_KV, GQA, HEAD_DIM).transpose(0, 1, 3, 5, 4, 2)
          .reshape(bsz, nblk, N_KV, HEAD_DIM, cols))
    gtt = (gt[:, :3 * N_HEADS].reshape(bsz, nblk, Q_BLOCK, N_KV, GQA, 3).transpose(0, 1, 3, 5, 4, 2)
           .reshape(bsz, nblk, N_KV, 3, cols))
    o_t = _nsa_prompt(qt, gtt, kc.astype(BF16), vc.transpose(0, 1, 3, 2).astype(BF16), ks, vst, kw, vwt,
                      cband, far, seld, wbias, cover_p)
    ob = (o_t.reshape(bsz, nblk, N_KV, HEAD_DIM, GQA, Q_BLOCK).transpose(0, 1, 5, 2, 4, 3)
          .reshape(n_p, D_B))
    wsp, bsp = _spatial_operands(wts['spatial_w'], wts['spatial_b'], CHUNK)
    y_prompt = _mixout(xp, ua, vn, zas, ob, zbs, wsp, bsp, wts['wo'], wts['fg']).reshape(bsz, seq, D_MODEL)
    kv5 = lambda a, b_, s_: a.reshape(1, b_, s_, 2, N_KV, HEAD_DIM)
    new_cmp_p = kv5(kvc, bsz, seq)
    new_sel_p = kv5(kvs, bsz, seq)
    new_win_p = kv5(kvw, bsz, seq)[:, :, seq - win_buf:]

    n_s = db * ds
    xs = x_sample.reshape(n_s, D_MODEL)
    ua, vn, zas, q, kvc, kvs, kvw, gt, zbs = _group_layer(xs, wts, db)
    n_phys = cache_cmp_kv.shape[1]
    kc, vc = _compress(page_table, cache_cmp_kv[l].reshape(n_phys, 2 * PAGE_SIZE, KV_W), wts['w1cat'],
                       wts['pe8'], wts['b1'], wts['w2'], wts['b2'])
    ncs = past // CMP_STRIDE
    t_all = past + ds
    nss = -(-t_all // SEL_BLOCK)
    nss_pad = -(-nss // 8) * 8
    cover_s = _cover_t(t_all // CMP_STRIDE - CMP_R + 1, nss, ncs, nss_pad)
    qpad = 32
    ncol = N_KV * GQA * qpad
    nwin_pad = -(-(win_buf + ds) // 16) * 16
    dc_b, dw_b, dl_b, dn_b, far_b = _sample_bias_buckets(past, ds, qpad, ncs, nwin_pad)
    both = lambda t: jnp.concatenate([t[0], t[1]], axis=1)
    cbias_s = both(_bias_tiles(rel_table, dc_b, qpad))
    wbias_s = both(_bias_tiles(rel_table, dw_b, qpad))
    slast = both(_bias_tiles(rel_table, dl_b, qpad))
    snew = both(_bias_tiles(rel_table, dn_b, qpad))
    sfar = both(_bias_tiles(rel_table, far_b, qpad))[0:1]

    qg = q.reshape(db, ds, N_KV, GQA, HEAD_DIM).transpose(0, 2, 4, 3, 1)
    qg = jnp.tile(qg, (1, 1, 1, 1, qpad // ds)).reshape(db, N_KV, HEAD_DIM, GQA * qpad)
    zq = jnp.zeros_like(qg[:, 0])
    qbd = jnp.concatenate([jnp.concatenate([qg[:, 0], zq], axis=2),
                           jnp.concatenate([zq, qg[:, 1]], axis=2)], axis=1)
    gts = gt[:, :3 * N_HEADS].reshape(db, ds, N_KV, GQA, 3).transpose(0, 4, 2, 3, 1)
    gts = jnp.tile(gts, (1, 1, 1, 1, qpad // ds)).reshape(db, 3, ncol)
    rows_gd = lambda a: a.transpose(0, 2, 1, 3).reshape(db, ncs, KV_W).astype(BF16)
    win = jnp.concatenate([state_win_kv[l].reshape(db, win_buf, KV_ROW),
                           kvw.reshape(db, ds, KV_ROW)], axis=1)
    win_pad = jnp.pad(win, ((0, 0), (0, nwin_pad - win_buf - ds), (0, 0)))
    new_pad = jnp.pad(kvs.reshape(db, ds, KV_ROW), ((0, 0), (0, 16 - ds), (0, 0)))
    rsum = jnp.asarray(np.kron(np.eye(N_KV), np.kron(np.ones((GQA, GQA)), np.eye(qpad))), BF16)
    o_s = _nsa_sample(page_table, cache_sel_kv[l].reshape(n_phys, PAGE_SIZE, KV_ROW), qbd, gts,
                      rows_gd(kc), rows_gd(vc), win_pad, new_pad, cbias_s, wbias_s, sfar, slast, snew,
                      cover_s, rsum, past, ds)
    ob = (o_s.reshape(db, N_KV, HEAD_DIM, GQA, qpad)[..., :ds].transpose(0, 4, 1, 3, 2)
          .reshape(n_s, D_B))
    wsp, bsp = _spatial_operands(wts['spatial_w'], wts['spatial_b'], ds)
    y_sample = _mixout(xs, ua, vn, zas, ob, zbs, wsp, bsp, wts['wo'], wts['fg']).reshape(db, ds, D_MODEL)
    new_cmp_s = kv5(kvc, db, ds)
    new_sel_s = kv5(kvs, db, ds)
    new_win_s = win[:, ds:].reshape(1, db, win_buf, 2, N_KV, HEAD_DIM)
    new_chunk_v = vn.reshape(1, db, ds, D_A)
    return (y_prompt, y_sample, new_cmp_p, new_sel_p, new_win_p, new_cmp_s, new_sel_s, new_win_s,
            new_chunk_v)
```

```python
import functools
import math

import numpy as np
import jax
import jax.numpy as jnp
from jax import lax
from jax.experimental import pallas as pl
from jax.experimental.pallas import tpu as pltpu

F32 = jnp.float32
BF16 = jnp.bfloat16

D_MODEL = 1024
HEAD_DIM = 64
D_A = 512
D_B = 512
A_GROUPS = D_A // HEAD_DIM
CHUNK = 128
N_HEADS = D_B // HEAD_DIM
N_KV = 2
GQA = N_HEADS // N_KV
KV_W = N_KV * HEAD_DIM
KV_ROW = 2 * KV_W
CMP_STRIDE = 16
CMP_BLOCK = 32
CMP_R = CMP_BLOCK // CMP_STRIDE
CMP_HIDDEN = 256
CMP_FLAT = CMP_STRIDE * HEAD_DIM
SEL_BLOCK = 64
N_SEL = 16
WINDOW = 512
N_BUCKETS = 32
MAX_DISTANCE = 128
Q_BLOCK = 128
PAGE_SIZE = 128
RMS_EPS = 1e-6
LN_EPS = 1e-5
NEG = -1e30
FORCE_BONUS = 1e6
LOG2E = 1.4426950408889634
Q_SCALE = HEAD_DIM ** -0.5 * LOG2E
FAR_DIST = 1 << 20
WIN_VARIANTS = WINDOW // Q_BLOCK + 1

_OFF_U, _OFF_V, _OFF_ZA, _OFF_Q = 0, 512, 1024, 1536
_OFF_KVC, _OFF_KVS, _OFF_KVW, _OFF_ZB, _OFF_G = 2048, 2304, 2560, 2816, 3328
D_IN_PAD = 3456
GATE_PAD = 128
GATE_ROWS = 32

VMEM_LIMIT = 52 * 1024 * 1024


def _gelu(x):
    return x * (0.5 * (1.0 + jnp.tanh(0.7978845608028654 * (x + 0.044715 * (x * x * x)))))


def _sigmoid(x):
    return 1.0 / (1.0 + jnp.exp(-x))


def _dot(a, b):
    return jnp.dot(a, b, preferred_element_type=F32)


def _dot_t(a, b):
    return lax.dot_general(a, b, (((0,), (0,)), ((), ())), preferred_element_type=F32)


def _split_dot(a, b):
    hi = a.astype(BF16)
    lo = (a - hi.astype(F32)).astype(BF16)
    return _dot(hi, b) + _dot(lo, b)


def _split_dot_l(a, b):
    hi = b.astype(BF16)
    lo = (b - hi.astype(F32)).astype(BF16)
    return _dot(a, hi) + _dot(a, lo)


def _inproj_kernel(x_ref, ng_ref, w_ref, lng_ref, pavg_ref,
                   ua_ref, vn_ref, zas_ref, zbs_ref, kvc_ref, kvs_ref, kvw_ref,
                   qt_ref, gt_ref, ksk_ref, vst_ref, kwk_ref, vwt_ref):
    x = x_ref[...]
    ms = jnp.mean(x * x, axis=-1, keepdims=True)
    h = (x * lax.rsqrt(ms + RMS_EPS) * ng_ref[...]).astype(BF16)

    def proj(a, b):
        return _dot(h, w_ref[:, a:b])

    ua_ref[...] = _gelu(proj(_OFF_U, _OFF_V))
    v = _gelu(proj(_OFF_V, _OFF_ZA))
    mu = _split_dot(v, pavg_ref[...])
    d = v - mu
    var = _split_dot(d * d, pavg_ref[...])
    vn_ref[...] = d * lax.rsqrt(var + LN_EPS) * lng_ref[...]
    za = proj(_OFF_ZA, _OFF_Q)
    zas_ref[...] = za * _sigmoid(za)
    zb = proj(_OFF_ZB, _OFF_G)
    zbs_ref[...] = zb * _sigmoid(zb)
    qt_ref[...] = (proj(_OFF_Q, _OFF_KVC) * Q_SCALE).T.astype(BF16)
    gt_ref[...] = _sigmoid(proj(_OFF_G, D_IN_PAD)).T[0:GATE_ROWS, :]
    kvc_ref[...] = proj(_OFF_KVC, _OFF_KVS)
    kvs = proj(_OFF_KVS, _OFF_KVW)
    kvs_ref[...] = kvs
    ksk_ref[...] = kvs[:, 0:KV_W].astype(BF16)
    vst_ref[...] = kvs[:, KV_W:KV_ROW].T.astype(BF16)
    kvw = proj(_OFF_KVW, _OFF_ZB)
    kvw_ref[...] = kvw
    kwk_ref[...] = kvw[:, 0:KV_W].astype(BF16)
    vwt_ref[...] = kvw[:, KV_W:KV_ROW].T.astype(BF16)


def _inproj(x, norm_g, w_perm, ln_g, pavg):
    n = x.shape[0]
    tm = min(256, n)
    row = lambda w: pl.BlockSpec((tm, w), lambda i: (i, 0))
    col = lambda h: pl.BlockSpec((h, tm), lambda i: (0, i))
    full = lambda a: pl.BlockSpec(a.shape, lambda i: (0,) * a.ndim)
    rows = [(D_A, F32), (D_A, F32), (D_A, F32), (D_B, F32), (KV_ROW, F32), (KV_ROW, F32), (KV_ROW, F32)]
    sds = lambda shape, dt: jax.ShapeDtypeStruct(shape, dt)
    out_specs = ([row(w) for w, _ in rows]
                 + [col(D_B), col(GATE_ROWS), row(KV_W), col(KV_W), row(KV_W), col(KV_W)])
    out_shape = ([sds((n, w), dt) for w, dt in rows]
                 + [sds((D_B, n), BF16), sds((GATE_ROWS, n), F32), sds((n, KV_W), BF16),
                    sds((KV_W, n), BF16), sds((n, KV_W), BF16), sds((KV_W, n), BF16)])
    return pl.pallas_call(
        _inproj_kernel,
        grid=(n // tm,),
        in_specs=[row(D_MODEL), full(norm_g), full(w_perm), full(ln_g), full(pavg)],
        out_specs=out_specs,
        out_shape=out_shape,
        compiler_params=pltpu.CompilerParams(dimension_semantics=("parallel",),
                                             vmem_limit_bytes=VMEM_LIMIT),
        name="inproj",
    )(x, norm_g, w_perm, ln_g, pavg)


def _mixout_kernel(x_ref, ua_ref, vn_ref, zas_ref, ob_ref, zbs_ref, wsp_ref, bsp_ref, wo_ref, fg_ref,
                   y_ref, s_ref):
    tm = x_ref.shape[0]
    for c in range(tm // CHUNK):
        rows = slice(c * CHUNK, (c + 1) * CHUNK)
        vc = vn_ref[rows, :].astype(BF16)
        for g in range(A_GROUPS):
            cols = slice(g * HEAD_DIM, (g + 1) * HEAD_DIM)
            s_ref[rows, cols] = _dot(wsp_ref[g], vc[:, cols])
        s_ref[rows, :] = s_ref[rows, :] + bsp_ref[...]
    mix_a = (ua_ref[...] * s_ref[...] * zas_ref[...]).astype(BF16)
    mix_b = (ob_ref[...] * zbs_ref[...]).astype(BF16)
    y = x_ref[...] + _dot(mix_a, wo_ref[0:D_A, :]) + _dot(mix_b, wo_ref[D_A:D_A + D_B, :])
    ms = jnp.mean(y * y, axis=-1, keepdims=True)
    y_ref[...] = y * lax.rsqrt(ms + RMS_EPS) * fg_ref[...]


def _mixout(x, ua, vn, zas, ob, zbs, wsp, bsp, wo, fg):
    n = x.shape[0]
    tm = min(256, n)
    row = lambda w: pl.BlockSpec((tm, w), lambda i: (i, 0))
    full = lambda a: pl.BlockSpec(a.shape, lambda i: (0,) * a.ndim)
    return pl.pallas_call(
        _mixout_kernel,
        grid=(n // tm,),
        in_specs=[row(D_MODEL), row(D_A), row(D_A), row(D_A), row(D_B), row(D_B),
                  full(wsp), full(bsp), full(wo), full(fg)],
        out_specs=row(D_MODEL),
        out_shape=jax.ShapeDtypeStruct((n, D_MODEL), F32),
        scratch_shapes=[pltpu.VMEM((tm, D_A), F32)],
        compiler_params=pltpu.CompilerParams(dimension_semantics=("parallel",),
                                             vmem_limit_bytes=VMEM_LIMIT),
        name="mixout",
    )(x, ua, vn, zas, ob, zbs, wsp, bsp, wo, fg)


def _t5_bucket_np(dist):
    dist = np.asarray(dist, np.int64)
    n = np.maximum(dist, 0)
    max_exact = N_BUCKETS // 2
    nf = np.maximum(n, max_exact).astype(np.float64)
    large = max_exact + (np.log(nf / max_exact) / math.log(MAX_DISTANCE / max_exact)
                         * (N_BUCKETS - max_exact)).astype(np.int64)
    b = np.where(n < max_exact, n, np.minimum(large, N_BUCKETS - 1))
    return np.where(dist < 0, -1, b).astype(np.int32)


def _bias_kernel(tbl_ref, bkt_ref, out_ref, *, qb, rel_far):
    b = bkt_ref[...]
    grp = lax.broadcasted_iota(jnp.int32, (1, b.shape[1]), 1) // qb

    def head_row(k, g):
        row = jnp.zeros(grp.shape, F32)
        for r in range(GQA):
            row = jnp.where(grp == r, tbl_ref[k * N_HEADS + g * GQA + r], row)
        return row * LOG2E

    for g in range(N_KV):
        acc = jnp.full(b.shape, NEG, F32)
        base = head_row(N_BUCKETS - 1, g) if rel_far else None
        for k in range(N_BUCKETS):
            row = head_row(k, g)
            if rel_far:
                row = row - base
            acc = jnp.where(b == k, row, acc)
        out_ref[g] = acc


def _bias_tiles(rel_table, bkt, qb, rel_far=False):
    r, c = bkt.shape
    rb = r
    for cand in (512, 256, 128, 64, 32, 16, 8):
        if r % cand == 0:
            rb = cand
            break
    return pl.pallas_call(
        functools.partial(_bias_kernel, qb=qb, rel_far=rel_far),
        grid=(r // rb,),
        in_specs=[pl.BlockSpec(memory_space=pltpu.SMEM), pl.BlockSpec((rb, c), lambda i: (i, 0))],
        out_specs=pl.BlockSpec((N_KV, rb, c), lambda i: (0, i, 0)),
        out_shape=jax.ShapeDtypeStruct((N_KV, r, c), F32),
        compiler_params=pltpu.CompilerParams(dimension_semantics=("parallel",)),
        name="bias_tiles",
    )(rel_table.reshape(-1), jnp.asarray(bkt))


def _compress_kernel(tbl_ref, *refs, pg):
    del tbl_ref
    page_refs = refs[:pg + 1]
    w1_ref, pe_ref, b1_ref, w2_ref, b2_ref, kc_ref, vct_ref, x_ref = refs[pg + 1:]
    mp = (pg + 1) * 8
    nrow = pg * 8
    left = lax.broadcasted_iota(jnp.int32, (8, 128), 1) < HEAD_DIM
    for k, pr in enumerate(page_refs):
        for t in range(CMP_STRIDE // 2):
            for kv in range(2):
                e = pr[0, pl.ds(4 * t + kv, 8, stride=2 * CMP_STRIDE), :]
                o = pr[0, pl.ds(4 * t + 2 + kv, 8, stride=2 * CMP_STRIDE), :]
                x_ref[kv, 8 * k:8 * k + 8, 128 * t:128 * t + 128] = jnp.where(
                    left, e, pltpu.roll(o, HEAD_DIM, 1))
                x_ref[kv, mp + 8 * k:mp + 8 * k + 8, 128 * t:128 * t + 128] = jnp.where(
                    left, pltpu.roll(e, HEAD_DIM, 1), o)
    outs = []
    for kv in range(2):
        x_ref[kv, 2 * mp:2 * mp + 8, :] = pe_ref[kv]
        p = _dot(x_ref[kv].astype(BF16), w1_ref[kv])
        hc = (b1_ref[kv] + p[2 * mp:2 * mp + 1, 0:CMP_HIDDEN]
              + p[2 * mp + 1:2 * mp + 2, CMP_HIDDEN:2 * CMP_HIDDEN])
        per_g = []
        for g in range(N_KV):
            base = g * mp
            h = (p[base:base + nrow, 0:CMP_HIDDEN]
                 + p[base + 1:base + nrow + 1, CMP_HIDDEN:2 * CMP_HIDDEN] + hc)
            per_g.append(_dot(_gelu(h).astype(BF16), w2_ref[kv]) + b2_ref[kv])
        outs.append(jnp.concatenate(per_g, axis=1))
    kc_ref[0] = outs[0].astype(BF16)
    vct_ref[0] = outs[1].T.astype(BF16)


def _compress(table, pages, w1cat, pe8, b1, w2, b2):
    nb, npg = table.shape
    pg = min(32, npg)
    assert npg % pg == 0
    mp = (pg + 1) * 8

    def page_spec(k):
        return pl.BlockSpec(
            (1, 2 * PAGE_SIZE, KV_W),
            lambda b, j, tbl: (tbl[b, jnp.minimum(j * pg + k, npg - 1)], 0, 0))

    full = lambda a: pl.BlockSpec(a.shape, lambda b, j, tbl: (0,) * a.ndim)
    grid_spec = pltpu.PrefetchScalarGridSpec(
        num_scalar_prefetch=1,
        grid=(nb, npg // pg),
        in_specs=[page_spec(k) for k in range(pg + 1)] + [full(a) for a in (w1cat, pe8, b1, w2, b2)],
        out_specs=[pl.BlockSpec((1, pg * 8, KV_W), lambda b, j, tbl: (b, j, 0)),
                   pl.BlockSpec((1, KV_W, pg * 8), lambda b, j, tbl: (b, 0, j))],
        scratch_shapes=[pltpu.VMEM((2, 2 * mp + 8, CMP_FLAT), F32)],
    )
    return pl.pallas_call(
        functools.partial(_compress_kernel, pg=pg),
        grid_spec=grid_spec,
        out_shape=[jax.ShapeDtypeStruct((nb, npg * 8, KV_W), BF16),
                   jax.ShapeDtypeStruct((nb, KV_W, npg * 8), BF16)],
        compiler_params=pltpu.CompilerParams(dimension_semantics=("parallel", "parallel"),
                                             vmem_limit_bytes=VMEM_LIMIT),
        name="compress",
    )(table, *([pages] * (pg + 1)), w1cat, pe8, b1, w2, b2)


def _topk_rows(imp, n_sel):
    ns = imp.shape[0]
    blk = lax.broadcasted_iota(jnp.int32, imp.shape, 0).astype(F32)
    sel = jnp.zeros(imp.shape, F32)
    for _ in range(n_sel):
        mx = jnp.max(imp, axis=0, keepdims=True)
        idx = jnp.min(jnp.where(imp == mx, blk, float(ns)), axis=0, keepdims=True)
        hit = blk == idx
        sel = jnp.where(hit, 1.0, sel)
        imp = jnp.where(hit, -jnp.inf, imp)
    return sel


def _select_mask(imp, qpos):
    blk = lax.broadcasted_iota(jnp.int32, imp.shape, 0)
    cur = qpos // SEL_BLOCK
    forced = (blk == 0) | (blk == cur) | (blk == cur - 1)
    valid = blk * SEL_BLOCK <= qpos
    imp = imp + jnp.where(forced, FORCE_BONUS, 0.0)
    imp = jnp.where(valid, imp, NEG)
    sel = _topk_rows(imp, N_SEL)
    return jnp.where((sel > 0.5) & valid, 0.0, NEG)


def _softmax_rows(s):
    m = jnp.max(s, axis=0, keepdims=True)
    e = jnp.exp2(s - m)
    inv = jnp.where(m > NEG / 2, 1.0 / jnp.sum(e, axis=0, keepdims=True), 0.0)
    return e * inv


def _online_update(s, v_dot, m, l, acc):
    m_new = jnp.maximum(m, jnp.max(s, axis=0, keepdims=True))
    alpha = jnp.exp2(m - m_new)
    p = jnp.exp2(s - m_new)
    l = alpha * l + jnp.sum(p, axis=0, keepdims=True)
    acc = alpha * acc + v_dot(p.astype(BF16))
    return m_new, l, acc


def _nsa_prompt_kernel(qt_ref, gt_ref, kc_ref, vct_ref, ks_ref, vst_ref, kw_ref, vwt_ref,
                       cband_ref, far_ref, seld_ref, wbias_ref, cover_ref,
                       o_ref, bias_ref, msk_ref, s_ref, *, nsub):
    g = pl.program_id(1)
    i = pl.program_id(2)
    cols = GQA * Q_BLOCK
    qblk = qt_ref[...]
    q64 = jnp.concatenate([qblk[r * HEAD_DIM:(r + 1) * HEAD_DIM, :] for r in range(GQA)], axis=1)
    zero = jnp.zeros_like(q64)
    q = jnp.concatenate([jnp.where(g == 0, q64, zero), jnp.where(g == 1, q64, zero)], axis=0)
    ncp = kc_ref.shape[1]
    far = far_ref[0]
    qpos = i * Q_BLOCK + lax.broadcasted_iota(jnp.int32, (1, Q_BLOCK), 1)

    c0 = pl.multiple_of(jnp.maximum(i * 8 - 8, 0), 8)
    crow = lax.broadcasted_iota(jnp.int32, (ncp, 1), 0)
    bias_ref[...] = jnp.where(crow < c0, far, NEG)
    bias_ref[pl.ds(c0, 16), :] = cband_ref[0, jnp.minimum(i, 1)]
    p_c = _softmax_rows(_dot(kc_ref[0], q) + bias_ref[...])
    o_c = _dot(vct_ref[0], p_c.astype(BF16))
    psum = (p_c[:, 0:Q_BLOCK] + p_c[:, Q_BLOCK:2 * Q_BLOCK]
            + p_c[:, 2 * Q_BLOCK:3 * Q_BLOCK] + p_c[:, 3 * Q_BLOCK:4 * Q_BLOCK])
    imp = _split_dot_l(cover_ref[...], psum)

    woff = pl.multiple_of(jnp.maximum(i * Q_BLOCK - WINDOW, 0), Q_BLOCK)
    nwk = WINDOW + Q_BLOCK
    p_w = _softmax_rows(_dot(kw_ref[pl.ds(woff, nwk), :], q) + wbias_ref[0, 0])
    o_w = _dot(vwt_ref[:, pl.ds(woff, nwk)], p_w.astype(BF16))

    mask = _select_mask(imp, qpos)
    msk_ref[...] = jnp.concatenate([mask] * GQA, axis=1) + far

    n_trips = i // nsub + 1
    tk = nsub * CHUNK

    def scores(t, slot):
        pen = jnp.where(t < n_trips, 0.0, NEG)
        t = jnp.minimum(t, n_trips - 1)
        for u in range(nsub):
            kb = t * nsub + u
            off = pl.multiple_of(kb * CHUNK, CHUNK)
            for hb in range(2):
                rows = pl.ds(off + hb * SEL_BLOCK, SEL_BLOCK)
                s = _dot(ks_ref[rows, :], q) + (msk_ref[pl.ds(2 * kb + hb, 1), :] + pen)
                s_ref[slot, (2 * u + hb) * SEL_BLOCK:(2 * u + hb + 1) * SEL_BLOCK, :] = s

    def attend(t, slot, near, carry):
        s = s_ref[slot]
        if near:
            s = s + jnp.concatenate(
                [seld_ref[0, jnp.clip(t * nsub + u - i + 2, 0, 2)] for u in range(nsub)], axis=0)
        koff = pl.multiple_of(jnp.minimum(t, n_trips - 1) * tk, tk)
        vt = jnp.concatenate([vst_ref[:, pl.ds(koff, tk)], jnp.ones((16, tk), BF16)], axis=0)
        m, acc = carry
        m_new = jnp.maximum(m, jnp.max(s, axis=0, keepdims=True))
        p = jnp.exp2(s - m_new).astype(BF16)
        return m_new, jnp.exp2(m - m_new) * acc + _dot(vt, p)

    def make_pair(near):
        def pair(tt, carry):
            scores(2 * tt + 1, 1)
            carry = attend(2 * tt, 0, near, carry)
            scores(2 * tt + 2, 0)
            return attend(2 * tt + 1, 1, near, carry)
        return pair

    n_far_pairs = (jnp.maximum(i - 1, 0) // nsub) // 2
    init = (jnp.full((1, cols), -jnp.inf, F32), jnp.zeros((HEAD_DIM + 16, cols), F32))
    scores(0, 0)
    carry = lax.fori_loop(0, n_far_pairs, make_pair(False), init)
    _, acc_s = lax.fori_loop(n_far_pairs, (n_trips + 1) // 2, make_pair(True), carry)
    o_s = acc_s[0:HEAD_DIM] * (1.0 / acc_s[HEAD_DIM:HEAD_DIM + 1])

    gt = gt_ref[...]

    def gate(j):
        return jnp.concatenate(
            [jnp.where(g == 0, gt[r * 3 + j:r * 3 + j + 1, :],
                       gt[GQA * 3 + r * 3 + j:GQA * 3 + r * 3 + j + 1, :]) for r in range(GQA)], axis=1)

    o = gate(0) * o_c + gate(1) * o_s + gate(2) * o_w
    o_ref[...] = jnp.concatenate([o[:, r * Q_BLOCK:(r + 1) * Q_BLOCK].T for r in range(GQA)], axis=1)


def _nsa_prompt(qt, gt, kc, vct, ks, vst, kw, vwt, cband, far, seld, wbias, cover_t, bsz, seq):
    nblk = seq // Q_BLOCK
    ncp = kc.shape[1]
    ns = cover_t.shape[0]
    cols = GQA * Q_BLOCK
    nsub = next(c for c in (4, 2, 1) if nblk % c == 0)
    per_g = lambda a: pl.BlockSpec((1,) + a.shape[1:], lambda b, g, i: (g,) + (0,) * (a.ndim - 1))
    in_specs = [
        pl.BlockSpec((GQA * HEAD_DIM, Q_BLOCK), lambda b, g, i: (g, b * nblk + i)),
        pl.BlockSpec((GATE_ROWS, Q_BLOCK), lambda b, g, i: (0, b * nblk + i)),
        pl.BlockSpec((1, ncp, KV_W), lambda b, g, i: (b, 0, 0)),
        pl.BlockSpec((1, HEAD_DIM, ncp), lambda b, g, i: (b, g, 0)),
        pl.BlockSpec((seq, KV_W), lambda b, g, i: (b, 0)),
        pl.BlockSpec((HEAD_DIM, seq), lambda b, g, i: (g, b)),
        pl.BlockSpec((seq, KV_W), lambda b, g, i: (b, 0)),
        pl.BlockSpec((HEAD_DIM, seq), lambda b, g, i: (g, b)),
        per_g(cband), per_g(far), per_g(seld),
        pl.BlockSpec((1, 1) + wbias.shape[2:],
                     lambda b, g, i: (g, jnp.minimum(i, WIN_VARIANTS - 1), 0, 0)),
        pl.BlockSpec(cover_t.shape, lambda b, g, i: (0, 0)),
    ]
    return pl.pallas_call(
        functools.partial(_nsa_prompt_kernel, nsub=nsub),
        grid=(bsz, N_KV, nblk),
        in_specs=in_specs,
        out_specs=pl.BlockSpec((Q_BLOCK, GQA * HEAD_DIM), lambda b, g, i: (b * nblk + i, g)),
        out_shape=jax.ShapeDtypeStruct((bsz * seq, D_B), F32),
        scratch_shapes=[pltpu.VMEM((ncp, cols), F32), pltpu.VMEM((ns, cols), F32),
                        pltpu.VMEM((2, nsub * CHUNK, cols), F32)],
        compiler_params=pltpu.CompilerParams(
            dimension_semantics=("parallel", "parallel", "arbitrary"),
            vmem_limit_bytes=VMEM_LIMIT),
        name="nsa_prompt",
    )(qt, gt, kc, vct, ks, vst, kw, vwt, cband, far, seld, wbias, cover_t)


def _nsa_sample_kernel(tbl_ref, *refs, pgs, npg, past, ds):
    del tbl_ref
    page_refs = refs[:pgs]
    (qbd_ref, gt_ref, kc_ref, vct_ref, win_ref, new_ref, cbias_ref, wbias_ref, sfar_ref, slast_ref,
     snew_ref, cover_ref, rsum_ref, o_ref, msk_ref, m_ref, l_ref, acc_ref, ocw_ref) = refs[pgs:]
    j = pl.program_id(1)
    qbd = qbd_ref[0]
    ncol = qbd.shape[1]
    qcols = ncol // (N_KV * GQA)
    g = gt_ref[0]

    @pl.when(j == 0)
    def _():
        p_c = _softmax_rows(_dot(kc_ref[0], qbd) + cbias_ref[...])
        o_c = _dot(vct_ref[0], p_c.astype(BF16))
        imp = _split_dot(_split_dot_l(cover_ref[...], p_c), rsum_ref[...])
        lane = lax.broadcasted_iota(jnp.int32, (1, ncol), 1)
        qpos = past + (lane % qcols) % ds
        msk_ref[...] = _select_mask(imp, qpos)
        kw = win_ref[0, :, 0:KV_W].astype(BF16)
        vw = win_ref[0, :, KV_W:KV_ROW].astype(BF16)
        p_w = _softmax_rows(_dot(kw, qbd) + wbias_ref[...])
        o_w = _dot_t(vw, p_w.astype(BF16))
        ocw_ref[...] = g[0:1, :] * o_c + g[2:3, :] * o_w
        m_ref[...] = jnp.full(m_ref.shape, -jnp.inf, F32)
        l_ref[...] = jnp.zeros(l_ref.shape, F32)
        acc_ref[...] = jnp.zeros(acc_ref.shape, F32)

    half = lax.broadcasted_iota(jnp.int32, (PAGE_SIZE, 1), 0) < SEL_BLOCK
    carry = (m_ref[...], l_ref[...], acc_ref[...])
    for k, pr in enumerate(page_refs):
        pidx = j * pgs + k
        kk = pr[0, :, 0:KV_W].astype(BF16)
        vv = pr[0, :, KV_W:KV_ROW].astype(BF16)
        s = _dot(kk, qbd)
        s = s + jnp.where(pidx == npg - 1, slast_ref[...], sfar_ref[...])
        s = s + jnp.where(half, msk_ref[pl.ds(2 * pidx, 1), :], msk_ref[pl.ds(2 * pidx + 1, 1), :])
        carry = _online_update(s, lambda p, vv=vv: _dot_t(vv, p), *carry)
    m_ref[...], l_ref[...], acc_ref[...] = carry

    @pl.when(j == pl.num_programs(1) - 1)
    def _():
        kn = new_ref[0, :, 0:KV_W].astype(BF16)
        vn = new_ref[0, :, KV_W:KV_ROW].astype(BF16)
        s = _dot(kn, qbd) + snew_ref[...] + msk_ref[pl.ds(2 * npg, 1), :]
        _, l, acc = _online_update(s, lambda p: _dot_t(vn, p), m_ref[...], l_ref[...], acc_ref[...])
        o = ocw_ref[...] + g[1:2, :] * (acc * (1.0 / l))
        for gi in range(N_KV):
            o_ref[0, gi] = o[gi * HEAD_DIM:(gi + 1) * HEAD_DIM, gi * GQA * qcols:(gi + 1) * GQA * qcols]


def _nsa_sample(table, pages, qbd, gt, kc, vct, win, new, cbias, wbias, sfar, slast, snew,
                cover_t, rsum, past, ds):
    nb, npg = table.shape
    pgs = min(16, npg)
    assert npg % pgs == 0
    ncol = qbd.shape[2]
    ns = cover_t.shape[0]

    def page_spec(k):
        return pl.BlockSpec((1, PAGE_SIZE, KV_ROW), lambda b, j, tbl: (tbl[b, j * pgs + k], 0, 0))

    per_b = lambda a: pl.BlockSpec((1,) + a.shape[1:], lambda b, j, tbl: (b,) + (0,) * (a.ndim - 1))
    full = lambda a: pl.BlockSpec(a.shape, lambda b, j, tbl: (0,) * a.ndim)
    grid_spec = pltpu.PrefetchScalarGridSpec(
        num_scalar_prefetch=1,
        grid=(nb, npg // pgs),
        in_specs=[page_spec(k) for k in range(pgs)]
        + [per_b(a) for a in (qbd, gt, kc, vct, win, new)]
        + [full(a) for a in (cbias, wbias, sfar, slast, snew, cover_t, rsum)],
        out_specs=pl.BlockSpec((1, N_KV, HEAD_DIM, ncol // N_KV), lambda b, j, tbl: (b, 0, 0, 0)),
        scratch_shapes=[pltpu.VMEM((ns, ncol), F32), pltpu.VMEM((1, ncol), F32),
                        pltpu.VMEM((1, ncol), F32), pltpu.VMEM((KV_W, ncol), F32),
                        pltpu.VMEM((KV_W, ncol), F32)],
    )
    return pl.pallas_call(
        functools.partial(_nsa_sample_kernel, pgs=pgs, npg=npg, past=past, ds=ds),
        grid_spec=grid_spec,
        out_shape=jax.ShapeDtypeStruct((nb, N_KV, HEAD_DIM, ncol // N_KV), F32),
        compiler_params=pltpu.CompilerParams(dimension_semantics=("parallel", "arbitrary"),
                                             vmem_limit_bytes=VMEM_LIMIT),
        name="nsa_sample",
    )(table, *([pages] * pgs), qbd, gt, kc, vct, win, new, cbias, wbias, sfar, slast, snew,
      cover_t, rsum)


def _cover_t(nc, ns, nc_pad, ns_pad):
    c0 = np.arange(nc) * CMP_STRIDE
    s0 = np.arange(ns) * SEL_BLOCK
    m = (c0[None, :] < s0[:, None] + SEL_BLOCK) & (c0[None, :] + CMP_BLOCK > s0[:, None])
    out = np.zeros((ns_pad, nc_pad), np.float32)
    out[:ns, :nc] = m
    return jnp.asarray(out, BF16)


def _prep_weights(norm_g, w_in, ln_v_g, spatial_w, spatial_b, cmp_pe, cmp_w1, cmp_b1, cmp_w2, cmp_b2,
                  w_out, final_g):
    offs = np.cumsum((D_A, D_A, D_A, D_B, 2 * KV_W, 2 * KV_W, 2 * KV_W, 3 * N_HEADS, D_B))
    g0, g1 = int(offs[6]), int(offs[7])
    w_perm = jnp.concatenate(
        [w_in[:, :g0], w_in[:, g1:], w_in[:, g0:g1],
         jnp.zeros((D_MODEL, GATE_PAD - 3 * N_HEADS), w_in.dtype)], axis=1).astype(BF16)
    pavg = jnp.asarray(np.kron(np.eye(A_GROUPS), np.full((HEAD_DIM, HEAD_DIM), 1.0 / HEAD_DIM)), BF16)
    w1cat = jnp.concatenate([cmp_w1[:, s].reshape(2, CMP_FLAT, CMP_HIDDEN) for s in range(CMP_R)],
                            axis=2).astype(BF16)
    pe8 = jnp.concatenate([cmp_pe.reshape(2, CMP_R, CMP_FLAT),
                           jnp.zeros((2, 8 - CMP_R, CMP_FLAT), F32)], axis=1)
    return dict(
        norm_g=norm_g.reshape(1, D_MODEL), w_perm=w_perm, ln_g=ln_v_g.reshape(1, D_A), pavg=pavg,
        w1cat=w1cat, pe8=pe8, b1=cmp_b1.reshape(2, 1, CMP_HIDDEN), w2=cmp_w2.astype(BF16),
        b2=cmp_b2.reshape(2, 1, HEAD_DIM), wo=w_out.astype(BF16), fg=final_g.reshape(1, D_MODEL),
        spatial_w=spatial_w, spatial_b=spatial_b)


def _spatial_operands(spatial_w, spatial_b, n):
    reps = CHUNK // n
    w = jnp.tril(spatial_w[:, :n, :n])
    eye = jnp.eye(reps, dtype=w.dtype)
    wsp = jnp.einsum('ab,gts->gatbs', eye, w).reshape(A_GROUPS, CHUNK, CHUNK).astype(BF16)
    b = jnp.tile(spatial_b[:, :n].T, (reps, 1))
    bsp = jnp.repeat(b, HEAD_DIM, axis=1)
    return wsp, bsp


def _prompt_bias_buckets():
    ql = np.arange(Q_BLOCK)[None, :]
    cl = np.arange(16)[:, None]
    band = np.stack([ql - CMP_STRIDE * cl - (CMP_BLOCK - 1),
                     ql + 97 - CMP_STRIDE * cl])
    kl = np.arange(CHUNK)[:, None]
    seld = np.stack([np.full((CHUNK, Q_BLOCK), FAR_DIST), CHUNK + ql - kl, ql - kl])
    wl = np.arange(WINDOW + Q_BLOCK)[:, None]
    dw = np.stack([Q_BLOCK * v + ql - wl for v in range(WIN_VARIANTS)])
    dw = np.where(dw < WINDOW, dw, -1)
    far = np.full((8, Q_BLOCK), FAR_DIST)
    tile4 = lambda d: np.tile(_t5_bucket_np(d.reshape(-1, Q_BLOCK)), (1, GQA))
    return tile4(band), tile4(seld), tile4(dw), tile4(far)


def _sample_bias_buckets(past, ds, qpad, ncp, nwin_pad):
    ql = (np.arange(qpad) % ds)[None, :]
    c = np.arange(ncp)[:, None]
    dc = past + ql - (CMP_STRIDE * c + CMP_BLOCK - 1)
    wl = np.arange(nwin_pad)[:, None]
    dw = WINDOW + ql - wl
    dw = np.where((dw < WINDOW) & (wl < WINDOW + ds), dw, -1)
    kl = np.arange(PAGE_SIZE)[:, None]
    dlast = PAGE_SIZE + ql - kl
    nl = np.arange(16)[:, None]
    dnew = np.where(nl < ds, ql - nl, -1)
    far = np.full((8, qpad), FAR_DIST)
    tile4 = lambda d: np.tile(_t5_bucket_np(d), (1, GQA))
    return tile4(dc), tile4(dw), tile4(dlast), tile4(dnew), tile4(far)


def kernel(x_prompt, x_sample, cache_cmp_kv, cache_sel_kv, state_win_kv, page_table, norm_g, w_in,
           ln_v_g, spatial_w, spatial_b, cmp_pe, cmp_w1, cmp_b1, cmp_w2, cmp_b2, rel_table, w_out,
           final_g):
    depth = norm_g.shape[0]
    assert depth == 1
    bsz, seq = x_prompt.shape[:2]
    db, ds = x_sample.shape[:2]
    npg = page_table.shape[1]
    past = npg * PAGE_SIZE
    win_buf = state_win_kv.shape[2]
    assert win_buf == WINDOW and past >= WINDOW and seq % Q_BLOCK == 0 and seq >= WINDOW + Q_BLOCK
    assert CHUNK % ds == 0 and ds <= 8 and (db * ds) % CHUNK == 0
    nblk = seq // Q_BLOCK
    l = 0
    wts = _prep_weights(norm_g[l], w_in[l], ln_v_g[l], spatial_w[l], spatial_b[l], cmp_pe[l],
                        cmp_w1[l], cmp_b1[l], cmp_w2[l], cmp_b2[l], w_out[l], final_g)
    inproj = lambda x: _inproj(x, wts['norm_g'], wts['w_perm'], wts['ln_g'], wts['pavg'])
    compress = lambda tbl, pages: _compress(tbl, pages, wts['w1cat'], wts['pe8'], wts['b1'],
                                            wts['w2'], wts['b2'])

    n_p = bsz * seq
    xp = x_prompt.reshape(n_p, D_MODEL)
    ua, vn, zas, zbs, kvc, kvs, kvw, qt, gt, ksk, vst, kwk, vwt = inproj(xp)
    ident = jnp.arange(bsz * nblk, dtype=jnp.int32).reshape(bsz, nblk)
    kc, vct = compress(ident, kvc.reshape(bsz * nblk, 2 * PAGE_SIZE, KV_W))
    ncp = seq // CMP_STRIDE
    cover_p = _cover_t(ncp - CMP_R + 1, seq // SEL_BLOCK, ncp, seq // SEL_BLOCK)
    band_b, seld_b, win_b, far_b = _prompt_bias_buckets()
    cols = GQA * Q_BLOCK
    cband = _bias_tiles(rel_table, band_b, Q_BLOCK).reshape(N_KV, 2, 16, cols)
    seld = _bias_tiles(rel_table, seld_b, Q_BLOCK, rel_far=True).reshape(N_KV, 3, CHUNK, cols)
    wbias = _bias_tiles(rel_table, win_b, Q_BLOCK).reshape(N_KV, WIN_VARIANTS, WINDOW + Q_BLOCK, cols)
    far = _bias_tiles(rel_table, far_b, Q_BLOCK)[:, 0:1]
    ob = _nsa_prompt(qt, gt, kc, vct, ksk, vst, kwk, vwt, cband, far, seld, wbias, cover_p, bsz, seq)
    wsp, bsp = _spatial_operands(wts['spatial_w'], wts['spatial_b'], CHUNK)
    y_prompt = _mixout(xp, ua, vn, zas, ob, zbs, wsp, bsp, wts['wo'], wts['fg']).reshape(bsz, seq, D_MODEL)
    kv5 = lambda a, b_, s_: a.reshape(1, b_, s_, 2, N_KV, HEAD_DIM)
    new_cmp_p = kv5(kvc, bsz, seq)
    new_sel_p = kv5(kvs, bsz, seq)
    new_win_p = kv5(kvw, bsz, seq)[:, :, seq - win_buf:]

    n_s = db * ds
    xs = x_sample.reshape(n_s, D_MODEL)
    ua, vn, zas, zbs, kvc, kvs, kvw, qt, gt, _, _, _, _ = inproj(xs)
    n_phys = cache_cmp_kv.shape[1]
    kc, vct = compress(page_table, cache_cmp_kv[l].reshape(n_phys, 2 * PAGE_SIZE, KV_W))
    ncs = past // CMP_STRIDE
    t_all = past + ds
    nss = -(-t_all // SEL_BLOCK)
    nss_pad = -(-nss // 8) * 8
    cover_s = _cover_t(t_all // CMP_STRIDE - CMP_R + 1, nss, ncs, nss_pad)
    qpad = 32
    ncol = N_KV * GQA * qpad
    nwin_pad = -(-(win_buf + ds) // 16) * 16
    dc_b, dw_b, dl_b, dn_b, far_b = _sample_bias_buckets(past, ds, qpad, ncs, nwin_pad)
    both = lambda t: jnp.concatenate([t[0], t[1]], axis=1)
    cbias_s = both(_bias_tiles(rel_table, dc_b, qpad))
    wbias_s = both(_bias_tiles(rel_table, dw_b, qpad))
    slast = both(_bias_tiles(rel_table, dl_b, qpad))
    snew = both(_bias_tiles(rel_table, dn_b, qpad))
    sfar = both(_bias_tiles(rel_table, far_b, qpad))[0:1]

    qg = qt.reshape(N_KV, GQA, HEAD_DIM, db, ds).transpose(3, 0, 2, 1, 4)
    qg = jnp.tile(qg, (1, 1, 1, 1, qpad // ds)).reshape(db, N_KV, HEAD_DIM, GQA * qpad)
    zq = jnp.zeros_like(qg[:, 0])
    qbd = jnp.concatenate([jnp.concatenate([qg[:, 0], zq], axis=2),
                           jnp.concatenate([zq, qg[:, 1]], axis=2)], axis=1)
    gts = gt[:3 * N_HEADS].reshape(N_KV, GQA, 3, db, ds).transpose(3, 2, 0, 1, 4)
    gts = jnp.tile(gts, (1, 1, 1, 1, qpad // ds)).reshape(db, 3, ncol)
    win = jnp.concatenate([state_win_kv[l].reshape(db, win_buf, KV_ROW),
                           kvw.reshape(db, ds, KV_ROW)], axis=1)
    win_pad = jnp.pad(win, ((0, 0), (0, nwin_pad - win_buf - ds), (0, 0)))
    new_pad = jnp.pad(kvs.reshape(db, ds, KV_ROW), ((0, 0), (0, 16 - ds), (0, 0)))
    rsum = jnp.asarray(np.kron(np.eye(N_KV), np.kron(np.ones((GQA, GQA)), np.eye(qpad))), BF16)
    o_s = _nsa_sample(page_table, cache_sel_kv[l].reshape(n_phys, PAGE_SIZE, KV_ROW), qbd, gts,
                      kc, vct, win_pad, new_pad, cbias_s, wbias_s, sfar, slast, snew,
                      cover_s, rsum, past, ds)
    ob = (o_s.reshape(db, N_KV, HEAD_DIM, GQA, qpad)[..., :ds].transpose(0, 4, 1, 3, 2)
          .reshape(n_s, D_B))
    wsp, bsp = _spatial_operands(wts['spatial_w'], wts['spatial_b'], ds)
    y_sample = _mixout(xs, ua, vn, zas, ob, zbs, wsp, bsp, wts['wo'], wts['fg']).reshape(db, ds, D_MODEL)
    new_cmp_s = kv5(kvc, db, ds)
    new_sel_s = kv5(kvs, db, ds)
    new_win_s = win[:, ds:].reshape(1, db, win_buf, 2, N_KV, HEAD_DIM)
    new_chunk_v = vn.reshape(1, db, ds, D_A)
    return (y_prompt, y_sample, new_cmp_p, new_sel_p, new_win_p, new_cmp_s, new_sel_s, new_win_s,
            new_chunk_v)
```

```python
import functools
import math

import numpy as np
import jax
import jax.numpy as jnp
from jax import lax
from jax.experimental import pallas as pl
from jax.experimental.pallas import tpu as pltpu

F32 = jnp.float32
BF16 = jnp.bfloat16

D_MODEL = 1024
HEAD_DIM = 64
D_A = 512
D_B = 512
A_GROUPS = D_A // HEAD_DIM
CHUNK = 128
N_HEADS = D_B // HEAD_DIM
N_KV = 2
GQA = N_HEADS // N_KV
KV_W = N_KV * HEAD_DIM
KV_ROW = 2 * KV_W
CMP_STRIDE = 16
CMP_BLOCK = 32
CMP_R = CMP_BLOCK // CMP_STRIDE
CMP_HIDDEN = 256
CMP_FLAT = CMP_STRIDE * HEAD_DIM
SEL_BLOCK = 64
N_SEL = 16
WINDOW = 512
N_BUCKETS = 32
MAX_DISTANCE = 128
Q_BLOCK = 128
PAGE_SIZE = 128
RMS_EPS = 1e-6
LN_EPS = 1e-5
NEG = -1e30
FORCE_BONUS = 1e6
LOG2E = 1.4426950408889634
Q_SCALE = HEAD_DIM ** -0.5 * LOG2E
FAR_DIST = 1 << 20
WIN_VARIANTS = WINDOW // Q_BLOCK + 1

_OFF_U, _OFF_V, _OFF_ZA, _OFF_Q = 0, 512, 1024, 1536
_OFF_KVC, _OFF_KVS, _OFF_KVW, _OFF_ZB, _OFF_G = 2048, 2304, 2560, 2816, 3328
D_IN_PAD = 3456
GATE_PAD = 128
GATE_ROWS = 32

VMEM_LIMIT = 52 * 1024 * 1024


def _gelu(x):
    return x * (0.5 * (1.0 + jnp.tanh(0.7978845608028654 * (x + 0.044715 * (x * x * x)))))


def _sigmoid(x):
    return 1.0 / (1.0 + jnp.exp(-x))


def _dot(a, b):
    return jnp.dot(a, b, preferred_element_type=F32)


def _dot_t(a, b):
    return lax.dot_general(a, b, (((0,), (0,)), ((), ())), preferred_element_type=F32)


def _split_dot(a, b):
    hi = a.astype(BF16)
    lo = (a - hi.astype(F32)).astype(BF16)
    return _dot(hi, b) + _dot(lo, b)


def _split_dot_l(a, b):
    hi = b.astype(BF16)
    lo = (b - hi.astype(F32)).astype(BF16)
    return _dot(a, hi) + _dot(a, lo)


def _inproj_kernel(x_ref, ng_ref, w_ref, lng_ref, pavg_ref,
                   ua_ref, vn_ref, zas_ref, zbs_ref, kvc_ref, kvs_ref, kvw_ref,
                   qt_ref, gt_ref, ksk_ref, vst_ref, kwk_ref, vwt_ref):
    x = x_ref[...]
    ms = jnp.mean(x * x, axis=-1, keepdims=True)
    h = (x * lax.rsqrt(ms + RMS_EPS) * ng_ref[...]).astype(BF16)

    def proj(a, b):
        return _dot(h, w_ref[:, a:b])

    ua_ref[...] = _gelu(proj(_OFF_U, _OFF_V))
    v = _gelu(proj(_OFF_V, _OFF_ZA))
    mu = _split_dot(v, pavg_ref[...])
    d = v - mu
    var = _split_dot(d * d, pavg_ref[...])
    vn_ref[...] = d * lax.rsqrt(var + LN_EPS) * lng_ref[...]
    za = proj(_OFF_ZA, _OFF_Q)
    zas_ref[...] = za * _sigmoid(za)
    zb = proj(_OFF_ZB, _OFF_G)
    zbs_ref[...] = zb * _sigmoid(zb)
    qt_ref[...] = (proj(_OFF_Q, _OFF_KVC) * Q_SCALE).T.astype(BF16)
    gt_ref[...] = _sigmoid(proj(_OFF_G, D_IN_PAD)).T[0:GATE_ROWS, :]
    kvc_ref[0] = proj(_OFF_KVC, _OFF_KVS).T
    kvs = proj(_OFF_KVS, _OFF_KVW)
    kvs_t = kvs.T
    kvs_ref[0] = kvs_t
    ksk_ref[...] = kvs[:, 0:KV_W].astype(BF16)
    vst_ref[...] = kvs_t[KV_W:KV_ROW, :].astype(BF16)
    kvw = proj(_OFF_KVW, _OFF_ZB)
    kvw_t = kvw.T
    kvw_ref[0] = kvw_t
    kwk_ref[...] = kvw[:, 0:KV_W].astype(BF16)
    vwt_ref[...] = kvw_t[KV_W:KV_ROW, :].astype(BF16)


def _inproj(x, nb, norm_g, w_perm, ln_g, pavg):
    n = x.shape[0]
    s = n // nb
    tm = min(256, s)
    assert s % tm == 0
    per = s // tm
    row = lambda w: pl.BlockSpec((tm, w), lambda i: (i, 0))
    col = lambda h: pl.BlockSpec((h, tm), lambda i: (0, i))
    kvt = pl.BlockSpec((1, KV_ROW, tm), lambda i: (i // per, 0, i % per))
    full = lambda a: pl.BlockSpec(a.shape, lambda i: (0,) * a.ndim)
    rows = [(D_A, F32), (D_A, F32), (D_A, F32), (D_B, F32)]
    sds = lambda shape, dt: jax.ShapeDtypeStruct(shape, dt)
    out_specs = ([row(w) for w, _ in rows] + [kvt, kvt, kvt]
                 + [col(D_B), col(GATE_ROWS), row(KV_W), col(KV_W), row(KV_W), col(KV_W)])
    out_shape = ([sds((n, w), dt) for w, dt in rows] + [sds((nb, KV_ROW, s), F32)] * 3
                 + [sds((D_B, n), BF16), sds((GATE_ROWS, n), F32), sds((n, KV_W), BF16),
                    sds((KV_W, n), BF16), sds((n, KV_W), BF16), sds((KV_W, n), BF16)])
    return pl.pallas_call(
        _inproj_kernel,
        grid=(n // tm,),
        in_specs=[row(D_MODEL), full(norm_g), full(w_perm), full(ln_g), full(pavg)],
        out_specs=out_specs,
        out_shape=out_shape,
        compiler_params=pltpu.CompilerParams(dimension_semantics=("parallel",),
                                             vmem_limit_bytes=VMEM_LIMIT),
        name="inproj",
    )(x, norm_g, w_perm, ln_g, pavg)


def _mixout_kernel(x_ref, ua_ref, vn_ref, zas_ref, ob_ref, zbs_ref, wsp_ref, bsp_ref, wo_ref, fg_ref,
                   y_ref, s_ref):
    tm = x_ref.shape[0]
    for c in range(tm // CHUNK):
        rows = slice(c * CHUNK, (c + 1) * CHUNK)
        vc = vn_ref[rows, :].astype(BF16)
        for g in range(A_GROUPS):
            cols = slice(g * HEAD_DIM, (g + 1) * HEAD_DIM)
            s_ref[rows, cols] = _dot(wsp_ref[g], vc[:, cols])
        s_ref[rows, :] = s_ref[rows, :] + bsp_ref[...]
    mix_a = (ua_ref[...] * s_ref[...] * zas_ref[...]).astype(BF16)
    mix_b = (ob_ref[...] * zbs_ref[...]).astype(BF16)
    y = x_ref[...] + _dot(mix_a, wo_ref[0:D_A, :]) + _dot(mix_b, wo_ref[D_A:D_A + D_B, :])
    ms = jnp.mean(y * y, axis=-1, keepdims=True)
    y_ref[...] = y * lax.rsqrt(ms + RMS_EPS) * fg_ref[...]


def _mixout(x, ua, vn, zas, ob, zbs, wsp, bsp, wo, fg):
    n = x.shape[0]
    tm = min(256, n)
    row = lambda w: pl.BlockSpec((tm, w), lambda i: (i, 0))
    full = lambda a: pl.BlockSpec(a.shape, lambda i: (0,) * a.ndim)
    return pl.pallas_call(
        _mixout_kernel,
        grid=(n // tm,),
        in_specs=[row(D_MODEL), row(D_A), row(D_A), row(D_A), row(D_B), row(D_B),
                  full(wsp), full(bsp), full(wo), full(fg)],
        out_specs=row(D_MODEL),
        out_shape=jax.ShapeDtypeStruct((n, D_MODEL), F32),
        scratch_shapes=[pltpu.VMEM((tm, D_A), F32)],
        compiler_params=pltpu.CompilerParams(dimension_semantics=("parallel",),
                                             vmem_limit_bytes=VMEM_LIMIT),
        name="mixout",
    )(x, ua, vn, zas, ob, zbs, wsp, bsp, wo, fg)


def _t5_bucket_np(dist):
    dist = np.asarray(dist, np.int64)
    n = np.maximum(dist, 0)
    max_exact = N_BUCKETS // 2
    nf = np.maximum(n, max_exact).astype(np.float64)
    large = max_exact + (np.log(nf / max_exact) / math.log(MAX_DISTANCE / max_exact)
                         * (N_BUCKETS - max_exact)).astype(np.int64)
    b = np.where(n < max_exact, n, np.minimum(large, N_BUCKETS - 1))
    return np.where(dist < 0, -1, b).astype(np.int32)


def _bias_kernel(tbl_ref, bkt_ref, out_ref, *, qb, rel_far):
    b = bkt_ref[...]
    grp = lax.broadcasted_iota(jnp.int32, (1, b.shape[1]), 1) // qb

    def head_row(k, g):
        row = jnp.zeros(grp.shape, F32)
        for r in range(GQA):
            row = jnp.where(grp == r, tbl_ref[k * N_HEADS + g * GQA + r], row)
        return row * LOG2E

    for g in range(N_KV):
        acc = jnp.full(b.shape, NEG, F32)
        base = head_row(N_BUCKETS - 1, g) if rel_far else None
        for k in range(N_BUCKETS):
            row = head_row(k, g)
            if rel_far:
                row = row - base
            acc = jnp.where(b == k, row, acc)
        out_ref[g] = acc


def _bias_tiles(rel_table, bkt, qb, rel_far=False):
    r, c = bkt.shape
    rb = r
    for cand in (512, 256, 128, 64, 32, 16, 8):
        if r % cand == 0:
            rb = cand
            break
    return pl.pallas_call(
        functools.partial(_bias_kernel, qb=qb, rel_far=rel_far),
        grid=(r // rb,),
        in_specs=[pl.BlockSpec(memory_space=pltpu.SMEM), pl.BlockSpec((rb, c), lambda i: (i, 0))],
        out_specs=pl.BlockSpec((N_KV, rb, c), lambda i: (0, i, 0)),
        out_shape=jax.ShapeDtypeStruct((N_KV, r, c), F32),
        compiler_params=pltpu.CompilerParams(dimension_semantics=("parallel",)),
        name="bias_tiles",
    )(rel_table.reshape(-1), jnp.asarray(bkt))


def _compress_kernel(tbl_ref, *refs, pg):
    del tbl_ref
    page_refs = refs[:pg + 1]
    w1_ref, pe_ref, b1_ref, w2_ref, b2_ref, kc_ref, vct_ref, tok_ref, x_ref = refs[pg + 1:]
    mp = (pg + 1) * 8
    nrow = pg * 8
    left = lax.broadcasted_iota(jnp.int32, (8, 128), 1) < HEAD_DIM
    for k, pr in enumerate(page_refs):
        for kv in range(2):
            tok_ref[k, kv] = pr[0, kv * KV_W:(kv + 1) * KV_W, :].T
        for t in range(CMP_STRIDE // 2):
            for kv in range(2):
                e = tok_ref[k, kv, pl.ds(2 * t, 8, stride=CMP_STRIDE), :]
                o = tok_ref[k, kv, pl.ds(2 * t + 1, 8, stride=CMP_STRIDE), :]
                x_ref[kv, 8 * k:8 * k + 8, 128 * t:128 * t + 128] = jnp.where(
                    left, e, pltpu.roll(o, HEAD_DIM, 1))
                x_ref[kv, mp + 8 * k:mp + 8 * k + 8, 128 * t:128 * t + 128] = jnp.where(
                    left, pltpu.roll(e, HEAD_DIM, 1), o)
    outs = []
    for kv in range(2):
        x_ref[kv, 2 * mp:2 * mp + 8, :] = pe_ref[kv]
        p = _dot(x_ref[kv].astype(BF16), w1_ref[kv])
        hc = (b1_ref[kv] + p[2 * mp:2 * mp + 1, 0:CMP_HIDDEN]
              + p[2 * mp + 1:2 * mp + 2, CMP_HIDDEN:2 * CMP_HIDDEN])
        per_g = []
        for g in range(N_KV):
            base = g * mp
            h = (p[base:base + nrow, 0:CMP_HIDDEN]
                 + p[base + 1:base + nrow + 1, CMP_HIDDEN:2 * CMP_HIDDEN] + hc)
            per_g.append(_dot(_gelu(h).astype(BF16), w2_ref[kv]) + b2_ref[kv])
        outs.append(jnp.concatenate(per_g, axis=1))
    kc_ref[0] = outs[0].astype(BF16)
    vct_ref[0] = outs[1].T.astype(BF16)


def _compress(table, pages, w1cat, pe8, b1, w2, b2):
    if table is None:
        nb, npg = pages.shape[0], pages.shape[2] // PAGE_SIZE
        table = jnp.zeros((1, 1), jnp.int32)
        index = lambda b, idx, tbl: (b, 0, idx)
    else:
        nb, npg = table.shape
        index = lambda b, idx, tbl: (tbl[b, idx], 0, 0)
    pg = min(32, npg)
    assert npg % pg == 0
    mp = (pg + 1) * 8

    def page_spec(k):
        return pl.BlockSpec(
            (1, KV_ROW, PAGE_SIZE),
            lambda b, j, tbl: index(b, jnp.minimum(j * pg + k, npg - 1), tbl))

    full = lambda a: pl.BlockSpec(a.shape, lambda b, j, tbl: (0,) * a.ndim)
    grid_spec = pltpu.PrefetchScalarGridSpec(
        num_scalar_prefetch=1,
        grid=(nb, npg // pg),
        in_specs=[page_spec(k) for k in range(pg + 1)] + [full(a) for a in (w1cat, pe8, b1, w2, b2)],
        out_specs=[pl.BlockSpec((1, pg * 8, KV_W), lambda b, j, tbl: (b, j, 0)),
                   pl.BlockSpec((1, KV_W, pg * 8), lambda b, j, tbl: (b, 0, j))],
        scratch_shapes=[pltpu.VMEM((pg + 1, 2, PAGE_SIZE, KV_W), F32),
                        pltpu.VMEM((2, 2 * mp + 8, CMP_FLAT), F32)],
    )
    return pl.pallas_call(
        functools.partial(_compress_kernel, pg=pg),
        grid_spec=grid_spec,
        out_shape=[jax.ShapeDtypeStruct((nb, npg * 8, KV_W), BF16),
                   jax.ShapeDtypeStruct((nb, KV_W, npg * 8), BF16)],
        compiler_params=pltpu.CompilerParams(dimension_semantics=("parallel", "parallel"),
                                             vmem_limit_bytes=VMEM_LIMIT),
        name="compress",
    )(table, *([pages] * (pg + 1)), w1cat, pe8, b1, w2, b2)


def _topk_rows(imp, n_sel):
    ns = imp.shape[0]
    blk = lax.broadcasted_iota(jnp.int32, imp.shape, 0).astype(F32)
    sel = jnp.zeros(imp.shape, F32)
    for _ in range(n_sel):
        mx = jnp.max(imp, axis=0, keepdims=True)
        idx = jnp.min(jnp.where(imp == mx, blk, float(ns)), axis=0, keepdims=True)
        hit = blk == idx
        sel = jnp.where(hit, 1.0, sel)
        imp = jnp.where(hit, -jnp.inf, imp)
    return sel


def _select_mask(imp, qpos):
    blk = lax.broadcasted_iota(jnp.int32, imp.shape, 0)
    cur = qpos // SEL_BLOCK
    forced = (blk == 0) | (blk == cur) | (blk == cur - 1)
    valid = blk * SEL_BLOCK <= qpos
    imp = imp + jnp.where(forced, FORCE_BONUS, 0.0)
    imp = jnp.where(valid, imp, NEG)
    sel = _topk_rows(imp, N_SEL)
    return jnp.where((sel > 0.5) & valid, 0.0, NEG)


def _softmax_rows(s):
    m = jnp.max(s, axis=0, keepdims=True)
    e = jnp.exp2(s - m)
    inv = jnp.where(m > NEG / 2, 1.0 / jnp.sum(e, axis=0, keepdims=True), 0.0)
    return e * inv


def _online_update(s, v_dot, m, l, acc):
    m_new = jnp.maximum(m, jnp.max(s, axis=0, keepdims=True))
    alpha = jnp.exp2(m - m_new)
    p = jnp.exp2(s - m_new)
    l = alpha * l + jnp.sum(p, axis=0, keepdims=True)
    acc = alpha * acc + v_dot(p.astype(BF16))
    return m_new, l, acc


def _nsa_prompt_kernel(qt_ref, gt_ref, kc_ref, vct_ref, ks_ref, vst_ref, kw_ref, vwt_ref,
                       cband_ref, far_ref, seld_ref, wbias_ref, cover_ref,
                       o_ref, bias_ref, msk_ref, s_ref, *, nsub):
    g = pl.program_id(1)
    i = pl.program_id(2)
    cols = GQA * Q_BLOCK
    qblk = qt_ref[...]
    q64 = jnp.concatenate([qblk[r * HEAD_DIM:(r + 1) * HEAD_DIM, :] for r in range(GQA)], axis=1)
    zero = jnp.zeros_like(q64)
    q = jnp.concatenate([jnp.where(g == 0, q64, zero), jnp.where(g == 1, q64, zero)], axis=0)
    ncp = kc_ref.shape[1]
    far = far_ref[0]
    qpos = i * Q_BLOCK + lax.broadcasted_iota(jnp.int32, (1, Q_BLOCK), 1)

    c0 = pl.multiple_of(jnp.maximum(i * 8 - 8, 0), 8)
    crow = lax.broadcasted_iota(jnp.int32, (ncp, 1), 0)
    bias_ref[...] = jnp.where(crow < c0, far, NEG)
    bias_ref[pl.ds(c0, 16), :] = cband_ref[0, jnp.minimum(i, 1)]
    p_c = _softmax_rows(_dot(kc_ref[0], q) + bias_ref[...])
    o_c = _dot(vct_ref[0], p_c.astype(BF16))
    psum = (p_c[:, 0:Q_BLOCK] + p_c[:, Q_BLOCK:2 * Q_BLOCK]
            + p_c[:, 2 * Q_BLOCK:3 * Q_BLOCK] + p_c[:, 3 * Q_BLOCK:4 * Q_BLOCK])
    imp = _split_dot_l(cover_ref[...], psum)

    woff = pl.multiple_of(jnp.maximum(i * Q_BLOCK - WINDOW, 0), Q_BLOCK)
    nwk = WINDOW + Q_BLOCK
    p_w = _softmax_rows(_dot(kw_ref[pl.ds(woff, nwk), :], q) + wbias_ref[0, 0])
    o_w = _dot(vwt_ref[:, pl.ds(woff, nwk)], p_w.astype(BF16))

    mask = _select_mask(imp, qpos)
    msk_ref[...] = jnp.concatenate([mask] * GQA, axis=1) + far

    n_trips = i // nsub + 1
    tk = nsub * CHUNK

    def scores(t, slot):
        pen = jnp.where(t < n_trips, 0.0, NEG)
        t = jnp.minimum(t, n_trips - 1)
        for u in range(nsub):
            kb = t * nsub + u
            off = pl.multiple_of(kb * CHUNK, CHUNK)
            for hb in range(2):
                rows = pl.ds(off + hb * SEL_BLOCK, SEL_BLOCK)
                s = _dot(ks_ref[rows, :], q) + (msk_ref[pl.ds(2 * kb + hb, 1), :] + pen)
                s_ref[slot, (2 * u + hb) * SEL_BLOCK:(2 * u + hb + 1) * SEL_BLOCK, :] = s

    def attend(t, slot, near, carry):
        s = s_ref[slot]
        if near:
            s = s + jnp.concatenate(
                [seld_ref[0, jnp.clip(t * nsub + u - i + 2, 0, 2)] for u in range(nsub)], axis=0)
        koff = pl.multiple_of(jnp.minimum(t, n_trips - 1) * tk, tk)
        vt = jnp.concatenate([vst_ref[:, pl.ds(koff, tk)], jnp.ones((16, tk), BF16)], axis=0)
        m, acc = carry
        m_new = jnp.maximum(m, jnp.max(s, axis=0, keepdims=True))
        p = jnp.exp2(s - m_new).astype(BF16)
        return m_new, jnp.exp2(m - m_new) * acc + _dot(vt, p)

    def make_pair(near):
        def pair(tt, carry):
            scores(2 * tt + 1, 1)
            carry = attend(2 * tt, 0, near, carry)
            scores(2 * tt + 2, 0)
            return attend(2 * tt + 1, 1, near, carry)
        return pair

    n_far_pairs = (jnp.maximum(i - 1, 0) // nsub) // 2
    init = (jnp.full((1, cols), -jnp.inf, F32), jnp.zeros((HEAD_DIM + 16, cols), F32))
    scores(0, 0)
    carry = lax.fori_loop(0, n_far_pairs, make_pair(False), init)
    _, acc_s = lax.fori_loop(n_far_pairs, (n_trips + 1) // 2, make_pair(True), carry)
    o_s = acc_s[0:HEAD_DIM] * (1.0 / acc_s[HEAD_DIM:HEAD_DIM + 1])

    gt = gt_ref[...]

    def gate(j):
        return jnp.concatenate(
            [jnp.where(g == 0, gt[r * 3 + j:r * 3 + j + 1, :],
                       gt[GQA * 3 + r * 3 + j:GQA * 3 + r * 3 + j + 1, :]) for r in range(GQA)], axis=1)

    o = gate(0) * o_c + gate(1) * o_s + gate(2) * o_w
    o_ref[...] = jnp.concatenate([o[:, r * Q_BLOCK:(r + 1) * Q_BLOCK].T for r in range(GQA)], axis=1)


def _nsa_prompt(qt, gt, kc, vct, ks, vst, kw, vwt, cband, far, seld, wbias, cover_t, bsz, seq):
    nblk = seq // Q_BLOCK
    ncp = kc.shape[1]
    ns = cover_t.shape[0]
    cols = GQA * Q_BLOCK
    nsub = next(c for c in (4, 2, 1) if nblk % c == 0)
    per_g = lambda a: pl.BlockSpec((1,) + a.shape[1:], lambda b, g, i: (g,) + (0,) * (a.ndim - 1))
    in_specs = [
        pl.BlockSpec((GQA * HEAD_DIM, Q_BLOCK), lambda b, g, i: (g, b * nblk + i)),
        pl.BlockSpec((GATE_ROWS, Q_BLOCK), lambda b, g, i: (0, b * nblk + i)),
        pl.BlockSpec((1, ncp, KV_W), lambda b, g, i: (b, 0, 0)),
        pl.BlockSpec((1, HEAD_DIM, ncp), lambda b, g, i: (b, g, 0)),
        pl.BlockSpec((seq, KV_W), lambda b, g, i: (b, 0)),
        pl.BlockSpec((HEAD_DIM, seq), lambda b, g, i: (g, b)),
        pl.BlockSpec((seq, KV_W), lambda b, g, i: (b, 0)),
        pl.BlockSpec((HEAD_DIM, seq), lambda b, g, i: (g, b)),
        per_g(cband), per_g(far), per_g(seld),
        pl.BlockSpec((1, 1) + wbias.shape[2:],
                     lambda b, g, i: (g, jnp.minimum(i, WIN_VARIANTS - 1), 0, 0)),
        pl.BlockSpec(cover_t.shape, lambda b, g, i: (0, 0)),
    ]
    return pl.pallas_call(
        functools.partial(_nsa_prompt_kernel, nsub=nsub),
        grid=(bsz, N_KV, nblk),
        in_specs=in_specs,
        out_specs=pl.BlockSpec((Q_BLOCK, GQA * HEAD_DIM), lambda b, g, i: (b * nblk + i, g)),
        out_shape=jax.ShapeDtypeStruct((bsz * seq, D_B), F32),
        scratch_shapes=[pltpu.VMEM((ncp, cols), F32), pltpu.VMEM((ns, cols), F32),
                        pltpu.VMEM((2, nsub * CHUNK, cols), F32)],
        compiler_params=pltpu.CompilerParams(
            dimension_semantics=("parallel", "parallel", "arbitrary"),
            vmem_limit_bytes=VMEM_LIMIT),
        name="nsa_prompt",
    )(qt, gt, kc, vct, ks, vst, kw, vwt, cband, far, seld, wbias, cover_t)


def _nsa_sample_kernel(tbl_ref, *refs, pgs, npg, past, ds):
    del tbl_ref
    page_refs = refs[:pgs]
    (qbd_ref, gt_ref, kc_ref, vct_ref, win_ref, new_ref, cbias_ref, wbias_ref, sfar_ref, slast_ref,
     snew_ref, cover_ref, rsum_ref, o_ref, msk_ref, m_ref, l_ref, acc_ref, ocw_ref) = refs[pgs:]
    j = pl.program_id(1)
    qbd = qbd_ref[0]
    ncol = qbd.shape[1]
    qcols = ncol // (N_KV * GQA)
    g = gt_ref[0]

    @pl.when(j == 0)
    def _():
        p_c = _softmax_rows(_dot(kc_ref[0], qbd) + cbias_ref[...])
        o_c = _dot(vct_ref[0], p_c.astype(BF16))
        imp = _split_dot(_split_dot_l(cover_ref[...], p_c), rsum_ref[...])
        lane = lax.broadcasted_iota(jnp.int32, (1, ncol), 1)
        qpos = past + (lane % qcols) % ds
        msk_ref[...] = _select_mask(imp, qpos)
        kw = win_ref[0, :, 0:KV_W].astype(BF16)
        vw = win_ref[0, :, KV_W:KV_ROW].astype(BF16)
        p_w = _softmax_rows(_dot(kw, qbd) + wbias_ref[...])
        o_w = _dot_t(vw, p_w.astype(BF16))
        ocw_ref[...] = g[0:1, :] * o_c + g[2:3, :] * o_w
        m_ref[...] = jnp.full(m_ref.shape, -jnp.inf, F32)
        l_ref[...] = jnp.zeros(l_ref.shape, F32)
        acc_ref[...] = jnp.zeros(acc_ref.shape, F32)

    half = lax.broadcasted_iota(jnp.int32, (PAGE_SIZE, 1), 0) < SEL_BLOCK
    carry = (m_ref[...], l_ref[...], acc_ref[...])
    for k, pr in enumerate(page_refs):
        pidx = j * pgs + k
        kk = pr[0, 0:KV_W, :].T.astype(BF16)
        vt = pr[0, KV_W:KV_ROW, :].astype(BF16)
        s = _dot(kk, qbd)
        s = s + jnp.where(pidx == npg - 1, slast_ref[...], sfar_ref[...])
        s = s + jnp.where(half, msk_ref[pl.ds(2 * pidx, 1), :], msk_ref[pl.ds(2 * pidx + 1, 1), :])
        carry = _online_update(s, lambda p, vt=vt: _dot(vt, p), *carry)
    m_ref[...], l_ref[...], acc_ref[...] = carry

    @pl.when(j == pl.num_programs(1) - 1)
    def _():
        kn = new_ref[0, :, 0:KV_W].astype(BF16)
        vn = new_ref[0, :, KV_W:KV_ROW].astype(BF16)
        s = _dot(kn, qbd) + snew_ref[...] + msk_ref[pl.ds(2 * npg, 1), :]
        _, l, acc = _online_update(s, lambda p: _dot_t(vn, p), m_ref[...], l_ref[...], acc_ref[...])
        o = ocw_ref[...] + g[1:2, :] * (acc * (1.0 / l))
        for gi in range(N_KV):
            o_ref[0, gi] = o[gi * HEAD_DIM:(gi + 1) * HEAD_DIM, gi * GQA * qcols:(gi + 1) * GQA * qcols]


def _nsa_sample(table, pages, qbd, gt, kc, vct, win, new, cbias, wbias, sfar, slast, snew,
                cover_t, rsum, past, ds):
    nb, npg = table.shape
    pgs = min(16, npg)
    assert npg % pgs == 0
    ncol = qbd.shape[2]
    ns = cover_t.shape[0]

    def page_spec(k):
        return pl.BlockSpec((1, KV_ROW, PAGE_SIZE), lambda b, j, tbl: (tbl[b, j * pgs + k], 0, 0))

    per_b = lambda a: pl.BlockSpec((1,) + a.shape[1:], lambda b, j, tbl: (b,) + (0,) * (a.ndim - 1))
    full = lambda a: pl.BlockSpec(a.shape, lambda b, j, tbl: (0,) * a.ndim)
    grid_spec = pltpu.PrefetchScalarGridSpec(
        num_scalar_prefetch=1,
        grid=(nb, npg // pgs),
        in_specs=[page_spec(k) for k in range(pgs)]
        + [per_b(a) for a in (qbd, gt, kc, vct, win, new)]
        + [full(a) for a in (cbias, wbias, sfar, slast, snew, cover_t, rsum)],
        out_specs=pl.BlockSpec((1, N_KV, HEAD_DIM, ncol // N_KV), lambda b, j, tbl: (b, 0, 0, 0)),
        scratch_shapes=[pltpu.VMEM((ns, ncol), F32), pltpu.VMEM((1, ncol), F32),
                        pltpu.VMEM((1, ncol), F32), pltpu.VMEM((KV_W, ncol), F32),
                        pltpu.VMEM((KV_W, ncol), F32)],
    )
    return pl.pallas_call(
        functools.partial(_nsa_sample_kernel, pgs=pgs, npg=npg, past=past, ds=ds),
        grid_spec=grid_spec,
        out_shape=jax.ShapeDtypeStruct((nb, N_KV, HEAD_DIM, ncol // N_KV), F32),
        compiler_params=pltpu.CompilerParams(dimension_semantics=("parallel", "arbitrary"),
                                             vmem_limit_bytes=VMEM_LIMIT),
        name="nsa_sample",
    )(table, *([pages] * pgs), qbd, gt, kc, vct, win, new, cbias, wbias, sfar, slast, snew,
      cover_t, rsum)


def _cover_t(nc, ns, nc_pad, ns_pad):
    c0 = np.arange(nc) * CMP_STRIDE
    s0 = np.arange(ns) * SEL_BLOCK
    m = (c0[None, :] < s0[:, None] + SEL_BLOCK) & (c0[None, :] + CMP_BLOCK > s0[:, None])
    out = np.zeros((ns_pad, nc_pad), np.float32)
    out[:ns, :nc] = m
    return jnp.asarray(out, BF16)


def _prep_weights(norm_g, w_in, ln_v_g, spatial_w, spatial_b, cmp_pe, cmp_w1, cmp_b1, cmp_w2, cmp_b2,
                  w_out, final_g):
    offs = np.cumsum((D_A, D_A, D_A, D_B, 2 * KV_W, 2 * KV_W, 2 * KV_W, 3 * N_HEADS, D_B))
    g0, g1 = int(offs[6]), int(offs[7])
    w_perm = jnp.concatenate(
        [w_in[:, :g0], w_in[:, g1:], w_in[:, g0:g1],
         jnp.zeros((D_MODEL, GATE_PAD - 3 * N_HEADS), w_in.dtype)], axis=1).astype(BF16)
    pavg = jnp.asarray(np.kron(np.eye(A_GROUPS), np.full((HEAD_DIM, HEAD_DIM), 1.0 / HEAD_DIM)), BF16)
    w1cat = jnp.concatenate([cmp_w1[:, s].reshape(2, CMP_FLAT, CMP_HIDDEN) for s in range(CMP_R)],
                            axis=2).astype(BF16)
    pe8 = jnp.concatenate([cmp_pe.reshape(2, CMP_R, CMP_FLAT),
                           jnp.zeros((2, 8 - CMP_R, CMP_FLAT), F32)], axis=1)
    return dict(
        norm_g=norm_g.reshape(1, D_MODEL), w_perm=w_perm, ln_g=ln_v_g.reshape(1, D_A), pavg=pavg,
        w1cat=w1cat, pe8=pe8, b1=cmp_b1.reshape(2, 1, CMP_HIDDEN), w2=cmp_w2.astype(BF16),
        b2=cmp_b2.reshape(2, 1, HEAD_DIM), wo=w_out.astype(BF16), fg=final_g.reshape(1, D_MODEL),
        spatial_w=spatial_w, spatial_b=spatial_b)


def _spatial_operands(spatial_w, spatial_b, n):
    reps = CHUNK // n
    w = jnp.tril(spatial_w[:, :n, :n])
    eye = jnp.eye(reps, dtype=w.dtype)
    wsp = jnp.einsum('ab,gts->gatbs', eye, w).reshape(A_GROUPS, CHUNK, CHUNK).astype(BF16)
    b = jnp.tile(spatial_b[:, :n].T, (reps, 1))
    bsp = jnp.repeat(b, HEAD_DIM, axis=1)
    return wsp, bsp


def _prompt_bias_buckets():
    ql = np.arange(Q_BLOCK)[None, :]
    cl = np.arange(16)[:, None]
    band = np.stack([ql - CMP_STRIDE * cl - (CMP_BLOCK - 1),
                     ql + 97 - CMP_STRIDE * cl])
    kl = np.arange(CHUNK)[:, None]
    seld = np.stack([np.full((CHUNK, Q_BLOCK), FAR_DIST), CHUNK + ql - kl, ql - kl])
    wl = np.arange(WINDOW + Q_BLOCK)[:, None]
    dw = np.stack([Q_BLOCK * v + ql - wl for v in range(WIN_VARIANTS)])
    dw = np.where(dw < WINDOW, dw, -1)
    far = np.full((8, Q_BLOCK), FAR_DIST)
    tile4 = lambda d: np.tile(_t5_bucket_np(d.reshape(-1, Q_BLOCK)), (1, GQA))
    return tile4(band), tile4(seld), tile4(dw), tile4(far)


def _sample_bias_buckets(past, ds, qpad, ncp, nwin_pad):
    ql = (np.arange(qpad) % ds)[None, :]
    c = np.arange(ncp)[:, None]
    dc = past + ql - (CMP_STRIDE * c + CMP_BLOCK - 1)
    wl = np.arange(nwin_pad)[:, None]
    dw = WINDOW + ql - wl
    dw = np.where((dw < WINDOW) & (wl < WINDOW + ds), dw, -1)
    kl = np.arange(PAGE_SIZE)[:, None]
    dlast = PAGE_SIZE + ql - kl
    nl = np.arange(16)[:, None]
    dnew = np.where(nl < ds, ql - nl, -1)
    far = np.full((8, qpad), FAR_DIST)
    tile4 = lambda d: np.tile(_t5_bucket_np(d), (1, GQA))
    return tile4(dc), tile4(dw), tile4(dlast), tile4(dnew), tile4(far)


def kernel(x_prompt, x_sample, cache_cmp_kv, cache_sel_kv, state_win_kv, page_table, norm_g, w_in,
           ln_v_g, spatial_w, spatial_b, cmp_pe, cmp_w1, cmp_b1, cmp_w2, cmp_b2, rel_table, w_out,
           final_g):
    depth = norm_g.shape[0]
    assert depth == 1
    bsz, seq = x_prompt.shape[:2]
    db, ds = x_sample.shape[:2]
    npg = page_table.shape[1]
    past = npg * PAGE_SIZE
    win_buf = state_win_kv.shape[2]
    assert win_buf == WINDOW and past >= WINDOW and seq % Q_BLOCK == 0 and seq >= WINDOW + Q_BLOCK
    assert CHUNK % ds == 0 and ds <= 8 and (db * ds) % CHUNK == 0
    nblk = seq // Q_BLOCK
    l = 0
    wts = _prep_weights(norm_g[l], w_in[l], ln_v_g[l], spatial_w[l], spatial_b[l], cmp_pe[l],
                        cmp_w1[l], cmp_b1[l], cmp_w2[l], cmp_b2[l], w_out[l], final_g)
    inproj = lambda x, nb: _inproj(x, nb, wts['norm_g'], wts['w_perm'], wts['ln_g'], wts['pavg'])
    compress = lambda tbl, pages: _compress(tbl, pages, wts['w1cat'], wts['pe8'], wts['b1'],
                                            wts['w2'], wts['b2'])
    kv_out = lambda a: a.reshape(a.shape[0], 2, N_KV, HEAD_DIM, a.shape[2]).transpose(0, 4, 1, 2, 3)[None]
    pages_t = lambda c: c.transpose(0, 2, 3, 4, 1).reshape(c.shape[0], KV_ROW, PAGE_SIZE)

    n_p = bsz * seq
    xp = x_prompt.reshape(n_p, D_MODEL)
    ua, vn, zas, zbs, kvc, kvs, kvw, qt, gt, ksk, vst, kwk, vwt = inproj(xp, bsz)
    kc, vct = compress(None, kvc)
    ncp = seq // CMP_STRIDE
    cover_p = _cover_t(ncp - CMP_R + 1, seq // SEL_BLOCK, ncp, seq // SEL_BLOCK)
    band_b, seld_b, win_b, far_b = _prompt_bias_buckets()
    cols = GQA * Q_BLOCK
    cband = _bias_tiles(rel_table, band_b, Q_BLOCK).reshape(N_KV, 2, 16, cols)
    seld = _bias_tiles(rel_table, seld_b, Q_BLOCK, rel_far=True).reshape(N_KV, 3, CHUNK, cols)
    wbias = _bias_tiles(rel_table, win_b, Q_BLOCK).reshape(N_KV, WIN_VARIANTS, WINDOW + Q_BLOCK, cols)
    far = _bias_tiles(rel_table, far_b, Q_BLOCK)[:, 0:1]
    ob = _nsa_prompt(qt, gt, kc, vct, ksk, vst, kwk, vwt, cband, far, seld, wbias, cover_p, bsz, seq)
    wsp, bsp = _spatial_operands(wts['spatial_w'], wts['spatial_b'], CHUNK)
    y_prompt = _mixout(xp, ua, vn, zas, ob, zbs, wsp, bsp, wts['wo'], wts['fg']).reshape(bsz, seq, D_MODEL)
    new_cmp_p = kv_out(kvc)
    new_sel_p = kv_out(kvs)
    new_win_p = kv_out(kvw[:, :, seq - win_buf:])

    n_s = db * ds
    xs = x_sample.reshape(n_s, D_MODEL)
    ua, vn, zas, zbs, kvc, kvs, kvw, qt, gt, _, _, _, _ = inproj(xs, 1)
    kvc, kvs, kvw = (a[0].T for a in (kvc, kvs, kvw))
    kc, vct = compress(page_table, pages_t(cache_cmp_kv[l]))
    ncs = past // CMP_STRIDE
    t_all = past + ds
    nss = -(-t_all // SEL_BLOCK)
    nss_pad = -(-nss // 8) * 8
    cover_s = _cover_t(t_all // CMP_STRIDE - CMP_R + 1, nss, ncs, nss_pad)
    qpad = 32
    ncol = N_KV * GQA * qpad
    nwin_pad = -(-(win_buf + ds) // 16) * 16
    dc_b, dw_b, dl_b, dn_b, far_b = _sample_bias_buckets(past, ds, qpad, ncs, nwin_pad)
    both = lambda t: jnp.concatenate([t[0], t[1]], axis=1)
    cbias_s = both(_bias_tiles(rel_table, dc_b, qpad))
    wbias_s = both(_bias_tiles(rel_table, dw_b, qpad))
    slast = both(_bias_tiles(rel_table, dl_b, qpad))
    snew = both(_bias_tiles(rel_table, dn_b, qpad))
    sfar = both(_bias_tiles(rel_table, far_b, qpad))[0:1]

    qg = qt.reshape(N_KV, GQA, HEAD_DIM, db, ds).transpose(3, 0, 2, 1, 4)
    qg = jnp.tile(qg, (1, 1, 1, 1, qpad // ds)).reshape(db, N_KV, HEAD_DIM, GQA * qpad)
    zq = jnp.zeros_like(qg[:, 0])
    qbd = jnp.concatenate([jnp.concatenate([qg[:, 0], zq], axis=2),
                           jnp.concatenate([zq, qg[:, 1]], axis=2)], axis=1)
    gts = gt[:3 * N_HEADS].reshape(N_KV, GQA, 3, db, ds).transpose(3, 2, 0, 1, 4)
    gts = jnp.tile(gts, (1, 1, 1, 1, qpad // ds)).reshape(db, 3, ncol)
    win = jnp.concatenate([state_win_kv[l].reshape(db, win_buf, KV_ROW),
                           kvw.reshape(db, ds, KV_ROW)], axis=1)
    win_pad = jnp.pad(win, ((0, 0), (0, nwin_pad - win_buf - ds), (0, 0)))
    new_pad = jnp.pad(kvs.reshape(db, ds, KV_ROW), ((0, 0), (0, 16 - ds), (0, 0)))
    rsum = jnp.asarray(np.kron(np.eye(N_KV), np.kron(np.ones((GQA, GQA)), np.eye(qpad))), BF16)
    o_s = _nsa_sample(page_table, pages_t(cache_sel_kv[l]), qbd, gts,
                      kc, vct, win_pad, new_pad, cbias_s, wbias_s, sfar, slast, snew,
                      cover_s, rsum, past, ds)
    ob = (o_s.reshape(db, N_KV, HEAD_DIM, GQA, qpad)[..., :ds].transpose(0, 4, 1, 3, 2)
          .reshape(n_s, D_B))
    wsp, bsp = _spatial_operands(wts['spatial_w'], wts['spatial_b'], ds)
    y_sample = _mixout(xs, ua, vn, zas, ob, zbs, wsp, bsp, wts['wo'], wts['fg']).reshape(db, ds, D_MODEL)
    kv5 = lambda a: a.reshape(1, db, ds, 2, N_KV, HEAD_DIM)
    new_cmp_s = kv5(kvc)
    new_sel_s = kv5(kvs)
    new_win_s = win[:, ds:].reshape(1, db, win_buf, 2, N_KV, HEAD_DIM)
    new_chunk_v = vn.reshape(1, db, ds, D_A)
    return (y_prompt, y_sample, new_cmp_p, new_sel_p, new_win_p, new_cmp_s, new_sel_s, new_win_s,
            new_chunk_v)
```

```python
import functools
import math

import numpy as np
import jax
import jax.numpy as jnp
from jax import lax
from jax.experimental import pallas as pl
from jax.experimental.pallas import tpu as pltpu

F32 = jnp.float32
BF16 = jnp.bfloat16

D_MODEL = 1024
HEAD_DIM = 64
D_A = 512
D_B = 512
A_GROUPS = D_A // HEAD_DIM
CHUNK = 128
N_HEADS = D_B // HEAD_DIM
N_KV = 2
GQA = N_HEADS // N_KV
KV_W = N_KV * HEAD_DIM
KV_ROW = 2 * KV_W
CMP_STRIDE = 16
CMP_BLOCK = 32
CMP_R = CMP_BLOCK // CMP_STRIDE
CMP_HIDDEN = 256
CMP_FLAT = CMP_STRIDE * HEAD_DIM
SEL_BLOCK = 64
N_SEL = 16
WINDOW = 512
N_BUCKETS = 32
MAX_DISTANCE = 128
Q_BLOCK = 128
PAGE_SIZE = 128
RMS_EPS = 1e-6
LN_EPS = 1e-5
NEG = -1e30
FORCE_BONUS = 1e6
LOG2E = 1.4426950408889634
Q_SCALE = HEAD_DIM ** -0.5 * LOG2E
FAR_DIST = 1 << 20
WIN_VARIANTS = WINDOW // Q_BLOCK + 1

_OFF_U, _OFF_V, _OFF_ZA, _OFF_Q = 0, 512, 1024, 1536
_OFF_KVC, _OFF_KVS, _OFF_KVW, _OFF_ZB, _OFF_G = 2048, 2304, 2560, 2816, 3328
D_IN_PAD = 3456
GATE_PAD = 128
GATE_ROWS = 32

VMEM_LIMIT = 52 * 1024 * 1024


def _gelu(x):
    return x * (0.5 * (1.0 + jnp.tanh(0.7978845608028654 * (x + 0.044715 * (x * x * x)))))


def _sigmoid(x):
    return 1.0 / (1.0 + jnp.exp(-x))


def _dot(a, b):
    return jnp.dot(a, b, preferred_element_type=F32)


def _dot_t(a, b):
    return lax.dot_general(a, b, (((0,), (0,)), ((), ())), preferred_element_type=F32)


def _dot_nt(a, b):
    return lax.dot_general(a, b, (((1,), (1,)), ((), ())), preferred_element_type=F32)


def _split_dot(a, b):
    hi = a.astype(BF16)
    lo = (a - hi.astype(F32)).astype(BF16)
    return _dot(hi, b) + _dot(lo, b)


def _split_dot_l(a, b):
    hi = b.astype(BF16)
    lo = (b - hi.astype(F32)).astype(BF16)
    return _dot(a, hi) + _dot(a, lo)


def _inproj_kernel(x_ref, ng_ref, w_ref, lng_ref, pavg_ref,
                   ua_ref, vn_ref, zas_ref, zbs_ref, kvc_ref, kvs_ref, kvw_ref,
                   qt_ref, gt_ref, ksk_ref, vst_ref, kwk_ref, vwt_ref):
    x = x_ref[...]
    ms = jnp.mean(x * x, axis=-1, keepdims=True)
    h = (x * lax.rsqrt(ms + RMS_EPS) * ng_ref[...]).astype(BF16)

    def proj(a, b):
        return _dot(h, w_ref[:, a:b])

    ua_ref[...] = _gelu(proj(_OFF_U, _OFF_V))
    v = _gelu(proj(_OFF_V, _OFF_ZA))
    mu = _split_dot(v, pavg_ref[...])
    d = v - mu
    var = _split_dot(d * d, pavg_ref[...])
    vn_ref[...] = d * lax.rsqrt(var + LN_EPS) * lng_ref[...]
    za = proj(_OFF_ZA, _OFF_Q)
    zas_ref[...] = za * _sigmoid(za)
    zb = proj(_OFF_ZB, _OFF_G)
    zbs_ref[...] = zb * _sigmoid(zb)
    qt_ref[...] = (proj(_OFF_Q, _OFF_KVC) * Q_SCALE).T.astype(BF16)
    gt_ref[...] = _sigmoid(proj(_OFF_G, D_IN_PAD)).T[0:GATE_ROWS, :]
    kvc_ref[0] = proj(_OFF_KVC, _OFF_KVS).T
    kvs = proj(_OFF_KVS, _OFF_KVW)
    kvs_t = kvs.T
    kvs_ref[0] = kvs_t
    ksk_ref[...] = kvs[:, 0:KV_W].astype(BF16)
    vst_ref[...] = kvs_t[KV_W:KV_ROW, :].astype(BF16)
    kvw = proj(_OFF_KVW, _OFF_ZB)
    kvw_t = kvw.T
    kvw_ref[0] = kvw_t
    kwk_ref[...] = kvw[:, 0:KV_W].astype(BF16)
    vwt_ref[...] = kvw_t[KV_W:KV_ROW, :].astype(BF16)


def _inproj(x, nb, norm_g, w_perm, ln_g, pavg):
    n = x.shape[0]
    s = n // nb
    tm = min(256, s)
    assert s % tm == 0
    per = s // tm
    row = lambda w: pl.BlockSpec((tm, w), lambda i: (i, 0))
    col = lambda h: pl.BlockSpec((h, tm), lambda i: (0, i))
    kvt = pl.BlockSpec((1, KV_ROW, tm), lambda i: (i // per, 0, i % per))
    full = lambda a: pl.BlockSpec(a.shape, lambda i: (0,) * a.ndim)
    rows = [(D_A, F32), (D_A, F32), (D_A, F32), (D_B, F32)]
    sds = lambda shape, dt: jax.ShapeDtypeStruct(shape, dt)
    out_specs = ([row(w) for w, _ in rows] + [kvt, kvt, kvt]
                 + [col(D_B), col(GATE_ROWS), row(KV_W), col(KV_W), row(KV_W), col(KV_W)])
    out_shape = ([sds((n, w), dt) for w, dt in rows] + [sds((nb, KV_ROW, s), F32)] * 3
                 + [sds((D_B, n), BF16), sds((GATE_ROWS, n), F32), sds((n, KV_W), BF16),
                    sds((KV_W, n), BF16), sds((n, KV_W), BF16), sds((KV_W, n), BF16)])
    return pl.pallas_call(
        _inproj_kernel,
        grid=(n // tm,),
        in_specs=[row(D_MODEL), full(norm_g), full(w_perm), full(ln_g), full(pavg)],
        out_specs=out_specs,
        out_shape=out_shape,
        compiler_params=pltpu.CompilerParams(dimension_semantics=("parallel",),
                                             vmem_limit_bytes=VMEM_LIMIT),
        name="inproj",
    )(x, norm_g, w_perm, ln_g, pavg)


def _mixout_kernel(x_ref, ua_ref, vn_ref, zas_ref, ob_ref, zbs_ref, wsp_ref, bsp_ref, wo_ref, fg_ref,
                   y_ref, s_ref):
    tm = x_ref.shape[0]
    for c in range(tm // CHUNK):
        rows = slice(c * CHUNK, (c + 1) * CHUNK)
        vc = vn_ref[rows, :].astype(BF16)
        for g in range(A_GROUPS):
            cols = slice(g * HEAD_DIM, (g + 1) * HEAD_DIM)
            s_ref[rows, cols] = _dot(wsp_ref[g], vc[:, cols])
        s_ref[rows, :] = s_ref[rows, :] + bsp_ref[...]
    mix_a = (ua_ref[...] * s_ref[...] * zas_ref[...]).astype(BF16)
    mix_b = (ob_ref[...] * zbs_ref[...]).astype(BF16)
    y = x_ref[...] + _dot(mix_a, wo_ref[0:D_A, :]) + _dot(mix_b, wo_ref[D_A:D_A + D_B, :])
    ms = jnp.mean(y * y, axis=-1, keepdims=True)
    y_ref[...] = y * lax.rsqrt(ms + RMS_EPS) * fg_ref[...]


def _mixout(x, ua, vn, zas, ob, zbs, wsp, bsp, wo, fg):
    n = x.shape[0]
    tm = min(256, n)
    row = lambda w: pl.BlockSpec((tm, w), lambda i: (i, 0))
    full = lambda a: pl.BlockSpec(a.shape, lambda i: (0,) * a.ndim)
    return pl.pallas_call(
        _mixout_kernel,
        grid=(n // tm,),
        in_specs=[row(D_MODEL), row(D_A), row(D_A), row(D_A), row(D_B), row(D_B),
                  full(wsp), full(bsp), full(wo), full(fg)],
        out_specs=row(D_MODEL),
        out_shape=jax.ShapeDtypeStruct((n, D_MODEL), F32),
        scratch_shapes=[pltpu.VMEM((tm, D_A), F32)],
        compiler_params=pltpu.CompilerParams(dimension_semantics=("parallel",),
                                             vmem_limit_bytes=VMEM_LIMIT),
        name="mixout",
    )(x, ua, vn, zas, ob, zbs, wsp, bsp, wo, fg)


def _t5_bucket_np(dist):
    dist = np.asarray(dist, np.int64)
    n = np.maximum(dist, 0)
    max_exact = N_BUCKETS // 2
    nf = np.maximum(n, max_exact).astype(np.float64)
    large = max_exact + (np.log(nf / max_exact) / math.log(MAX_DISTANCE / max_exact)
                         * (N_BUCKETS - max_exact)).astype(np.int64)
    b = np.where(n < max_exact, n, np.minimum(large, N_BUCKETS - 1))
    return np.where(dist < 0, -1, b).astype(np.int32)


def _bias_kernel(tbl_ref, bkt_ref, out_ref, *, qb, rel_far):
    b = bkt_ref[...]
    grp = lax.broadcasted_iota(jnp.int32, (1, b.shape[1]), 1) // qb

    def head_row(k, g):
        row = jnp.zeros(grp.shape, F32)
        for r in range(GQA):
            row = jnp.where(grp == r, tbl_ref[k * N_HEADS + g * GQA + r], row)
        return row * LOG2E

    for g in range(N_KV):
        acc = jnp.full(b.shape, NEG, F32)
        base = head_row(N_BUCKETS - 1, g) if rel_far else None
        for k in range(N_BUCKETS):
            row = head_row(k, g)
            if rel_far:
                row = row - base
            acc = jnp.where(b == k, row, acc)
        out_ref[g] = acc


def _bias_tiles(rel_table, bkt, qb, rel_far=False):
    r, c = bkt.shape
    rb = r
    for cand in (512, 256, 128, 64, 32, 16, 8):
        if r % cand == 0:
            rb = cand
            break
    return pl.pallas_call(
        functools.partial(_bias_kernel, qb=qb, rel_far=rel_far),
        grid=(r // rb,),
        in_specs=[pl.BlockSpec(memory_space=pltpu.SMEM), pl.BlockSpec((rb, c), lambda i: (i, 0))],
        out_specs=pl.BlockSpec((N_KV, rb, c), lambda i: (0, i, 0)),
        out_shape=jax.ShapeDtypeStruct((N_KV, r, c), F32),
        compiler_params=pltpu.CompilerParams(dimension_semantics=("parallel",)),
        name="bias_tiles",
    )(rel_table.reshape(-1), jnp.asarray(bkt))


def _compress_kernel(tbl_ref, *refs, pg):
    del tbl_ref
    page_refs = refs[:pg + 1]
    perm_ref, w1_ref, pe_ref, b1_ref, w2_ref, b2_ref, kc_ref, vct_ref, x_ref = refs[pg + 1:]
    mp = (pg + 1) * 8
    nrow = pg * 8
    left = lax.broadcasted_iota(jnp.int32, (8, 128), 1) < HEAD_DIM
    perm = perm_ref[...]
    for k, pr in enumerate(page_refs):
        for kv in range(2):
            half = pr[0, kv * KV_W:(kv + 1) * KV_W, :].astype(BF16)
            tok = _dot_nt(perm, half)
            for t in range(CMP_STRIDE // 2):
                e = tok[16 * t:16 * t + 8, :]
                o = tok[16 * t + 8:16 * t + 16, :]
                sw = pltpu.roll(jnp.where(left, o, e), HEAD_DIM, 1)
                x_ref[kv, 8 * k:8 * k + 8, 128 * t:128 * t + 128] = jnp.where(left, e, sw)
                x_ref[kv, mp + 8 * k:mp + 8 * k + 8, 128 * t:128 * t + 128] = jnp.where(left, sw, o)
    outs = []
    for kv in range(2):
        x_ref[kv, 2 * mp:2 * mp + 8, :] = pe_ref[kv]
        p = _dot(x_ref[kv].astype(BF16), w1_ref[kv])
        hc = (b1_ref[kv] + p[2 * mp:2 * mp + 1, 0:CMP_HIDDEN]
              + p[2 * mp + 1:2 * mp + 2, CMP_HIDDEN:2 * CMP_HIDDEN])
        per_g = []
        for g in range(N_KV):
            base = g * mp
            h = (p[base:base + nrow, 0:CMP_HIDDEN]
                 + p[base + 1:base + nrow + 1, CMP_HIDDEN:2 * CMP_HIDDEN] + hc)
            per_g.append(_dot(_gelu(h).astype(BF16), w2_ref[kv]) + b2_ref[kv])
        outs.append(jnp.concatenate(per_g, axis=1))
    kc_ref[0] = outs[0].astype(BF16)
    vct_ref[0] = outs[1].T.astype(BF16)


def _compress(table, pages, w1cat, pe8, b1, w2, b2):
    if table is None:
        nb, npg = pages.shape[0], pages.shape[2] // PAGE_SIZE
        table = jnp.zeros((1, 1), jnp.int32)
        index = lambda b, idx, tbl: (b, 0, idx)
    else:
        nb, npg = table.shape
        index = lambda b, idx, tbl: (tbl[b, idx], 0, 0)
    pg = min(32, npg)
    assert npg % pg == 0
    mp = (pg + 1) * 8

    def page_spec(k):
        return pl.BlockSpec(
            (1, KV_ROW, PAGE_SIZE),
            lambda b, j, tbl: index(b, jnp.minimum(j * pg + k, npg - 1), tbl))

    tok = np.arange(PAGE_SIZE)
    perm_np = np.zeros((PAGE_SIZE, PAGE_SIZE), np.float32)
    perm_np[(tok % CMP_STRIDE) * (PAGE_SIZE // CMP_STRIDE) + tok // CMP_STRIDE, tok] = 1.0
    perm = jnp.asarray(perm_np, BF16)
    full = lambda a: pl.BlockSpec(a.shape, lambda b, j, tbl: (0,) * a.ndim)
    grid_spec = pltpu.PrefetchScalarGridSpec(
        num_scalar_prefetch=1,
        grid=(nb, npg // pg),
        in_specs=[page_spec(k) for k in range(pg + 1)]
        + [full(a) for a in (perm, w1cat, pe8, b1, w2, b2)],
        out_specs=[pl.BlockSpec((1, pg * 8, KV_W), lambda b, j, tbl: (b, j, 0)),
                   pl.BlockSpec((1, KV_W, pg * 8), lambda b, j, tbl: (b, 0, j))],
        scratch_shapes=[pltpu.VMEM((2, 2 * mp + 8, CMP_FLAT), F32)],
    )
    return pl.pallas_call(
        functools.partial(_compress_kernel, pg=pg),
        grid_spec=grid_spec,
        out_shape=[jax.ShapeDtypeStruct((nb, npg * 8, KV_W), BF16),
                   jax.ShapeDtypeStruct((nb, KV_W, npg * 8), BF16)],
        compiler_params=pltpu.CompilerParams(dimension_semantics=("parallel", "parallel"),
                                             vmem_limit_bytes=VMEM_LIMIT),
        name="compress",
    )(table, *([pages] * (pg + 1)), perm, w1cat, pe8, b1, w2, b2)


def _topk_rows(imp, n_sel):
    ns = imp.shape[0]
    blk = lax.broadcasted_iota(jnp.int32, imp.shape, 0).astype(F32)
    sel = jnp.zeros(imp.shape, F32)
    for _ in range(n_sel):
        mx = jnp.max(imp, axis=0, keepdims=True)
        idx = jnp.min(jnp.where(imp == mx, blk, float(ns)), axis=0, keepdims=True)
        hit = blk == idx
        sel = jnp.where(hit, 1.0, sel)
        imp = jnp.where(hit, -jnp.inf, imp)
    return sel


def _select_mask(imp, qpos):
    blk = lax.broadcasted_iota(jnp.int32, imp.shape, 0)
    cur = qpos // SEL_BLOCK
    forced = (blk == 0) | (blk == cur) | (blk == cur - 1)
    valid = blk * SEL_BLOCK <= qpos
    imp = imp + jnp.where(forced, FORCE_BONUS, 0.0)
    imp = jnp.where(valid, imp, NEG)
    sel = _topk_rows(imp, N_SEL)
    return jnp.where((sel > 0.5) & valid, 0.0, NEG)


def _softmax_rows(s):
    m = jnp.max(s, axis=0, keepdims=True)
    e = jnp.exp2(s - m)
    inv = jnp.where(m > NEG / 2, 1.0 / jnp.sum(e, axis=0, keepdims=True), 0.0)
    return e * inv


def _online_update(s, v_dot, m, l, acc):
    m_new = jnp.maximum(m, jnp.max(s, axis=0, keepdims=True))
    alpha = jnp.exp2(m - m_new)
    p = jnp.exp2(s - m_new)
    l = alpha * l + jnp.sum(p, axis=0, keepdims=True)
    acc = alpha * acc + v_dot(p.astype(BF16))
    return m_new, l, acc


def _nsa_prompt_kernel(qt_ref, gt_ref, kc_ref, vct_ref, ks_ref, vst_ref, kw_ref, vwt_ref,
                       cband_ref, far_ref, seld_ref, wbias_ref, cover_ref,
                       o_ref, bias_ref, msk_ref, s_ref, *, nsub):
    i = pl.program_id(1)
    cols = GQA * Q_BLOCK
    ncp = kc_ref.shape[1]
    gw = GQA * HEAD_DIM
    qpos = i * Q_BLOCK + lax.broadcasted_iota(jnp.int32, (1, Q_BLOCK), 1)
    n_trips = i // nsub + 1
    tk = nsub * CHUNK
    c0 = pl.multiple_of(jnp.maximum(i * 8 - 8, 0), 8)
    crow = lax.broadcasted_iota(jnp.int32, (ncp, 1), 0)
    woff = pl.multiple_of(jnp.maximum(i * Q_BLOCK - WINDOW, 0), Q_BLOCK)
    nwk = WINDOW + Q_BLOCK

    def query(g):
        qblk = qt_ref[g * gw:(g + 1) * gw, :]
        q64 = jnp.concatenate([qblk[r * HEAD_DIM:(r + 1) * HEAD_DIM, :] for r in range(GQA)], axis=1)
        zero = jnp.zeros_like(q64)
        return jnp.concatenate([q64, zero] if g == 0 else [zero, q64], axis=0)

    def dense_branches(g, q):
        far = far_ref[g]
        bias_ref[g] = jnp.where(crow < c0, far, NEG)
        bias_ref[g, pl.ds(c0, 16), :] = cband_ref[g, jnp.minimum(i, 1)]
        p_c = _softmax_rows(_dot(kc_ref[0], q) + bias_ref[g])
        o_c = _dot(vct_ref[0, g * HEAD_DIM:(g + 1) * HEAD_DIM, :], p_c.astype(BF16))
        psum = (p_c[:, 0:Q_BLOCK] + p_c[:, Q_BLOCK:2 * Q_BLOCK]
                + p_c[:, 2 * Q_BLOCK:3 * Q_BLOCK] + p_c[:, 3 * Q_BLOCK:4 * Q_BLOCK])
        imp = _split_dot_l(cover_ref[...], psum)
        p_w = _softmax_rows(_dot(kw_ref[pl.ds(woff, nwk), :], q) + wbias_ref[g, 0])
        o_w = _dot(vwt_ref[g * HEAD_DIM:(g + 1) * HEAD_DIM, pl.ds(woff, nwk)], p_w.astype(BF16))
        mask = _select_mask(imp, qpos)
        msk_ref[g] = jnp.concatenate([mask] * GQA, axis=1) + far
        return o_c, o_w

    def scores(g, q, t, slot):
        pen = jnp.where(t < n_trips, 0.0, NEG)
        t = jnp.minimum(t, n_trips - 1)
        for u in range(nsub):
            kb = t * nsub + u
            off = pl.multiple_of(kb * CHUNK, CHUNK)
            for hb in range(2):
                rows = pl.ds(off + hb * SEL_BLOCK, SEL_BLOCK)
                s = _dot(ks_ref[rows, :], q) + (msk_ref[g, pl.ds(2 * kb + hb, 1), :] + pen)
                s_ref[g, slot, (2 * u + hb) * SEL_BLOCK:(2 * u + hb + 1) * SEL_BLOCK, :] = s

    def attend(g, t, slot, near, carry):
        s = s_ref[g, slot]
        if near:
            s = s + jnp.concatenate(
                [seld_ref[g, jnp.clip(t * nsub + u - i + 2, 0, 2)] for u in range(nsub)], axis=0)
        koff = pl.multiple_of(jnp.minimum(t, n_trips - 1) * tk, tk)
        vt = jnp.concatenate([vst_ref[g * HEAD_DIM:(g + 1) * HEAD_DIM, pl.ds(koff, tk)],
                              jnp.ones((16, tk), BF16)], axis=0)
        m, acc = carry
        m_new = jnp.maximum(m, jnp.max(s, axis=0, keepdims=True))
        p = jnp.exp2(s - m_new).astype(BF16)
        return m_new, jnp.exp2(m - m_new) * acc + _dot(vt, p)

    def selected(g, q):
        def make_pair(near):
            def pair(tt, carry):
                scores(g, q, 2 * tt + 1, 1)
                carry = attend(g, 2 * tt, 0, near, carry)
                scores(g, q, 2 * tt + 2, 0)
                return attend(g, 2 * tt + 1, 1, near, carry)
            return pair

        n_far_pairs = (jnp.maximum(i - 1, 0) // nsub) // 2
        init = (jnp.full((1, cols), -jnp.inf, F32), jnp.zeros((HEAD_DIM + 16, cols), F32))
        carry = lax.fori_loop(0, n_far_pairs, make_pair(False), init)
        _, acc_s = lax.fori_loop(n_far_pairs, (n_trips + 1) // 2, make_pair(True), carry)
        return acc_s[0:HEAD_DIM] * (1.0 / acc_s[HEAD_DIM:HEAD_DIM + 1])

    gt = gt_ref[...]

    def gate(g, j):
        return jnp.concatenate([gt[(g * GQA + r) * 3 + j:(g * GQA + r) * 3 + j + 1, :]
                                for r in range(GQA)], axis=1)

    qs = [query(g) for g in range(N_KV)]
    dense = [dense_branches(g, qs[g]) for g in range(N_KV)]
    for g in range(N_KV):
        scores(g, qs[g], 0, 0)
    for g in range(N_KV):
        o_c, o_w = dense[g]
        o = gate(g, 0) * o_c + gate(g, 1) * selected(g, qs[g]) + gate(g, 2) * o_w
        o_ref[:, g * gw:(g + 1) * gw] = jnp.concatenate(
            [o[:, r * Q_BLOCK:(r + 1) * Q_BLOCK].T for r in range(GQA)], axis=1)


def _nsa_prompt(qt, gt, kc, vct, ks, vst, kw, vwt, cband, far, seld, wbias, cover_t, bsz, seq):
    nblk = seq // Q_BLOCK
    ncp = kc.shape[1]
    ns = cover_t.shape[0]
    cols = GQA * Q_BLOCK
    nsub = next(c for c in (4, 2, 1) if nblk % c == 0)
    full = lambda a: pl.BlockSpec(a.shape, lambda b, i: (0,) * a.ndim)
    in_specs = [
        pl.BlockSpec((D_B, Q_BLOCK), lambda b, i: (0, b * nblk + i)),
        pl.BlockSpec((GATE_ROWS, Q_BLOCK), lambda b, i: (0, b * nblk + i)),
        pl.BlockSpec((1, ncp, KV_W), lambda b, i: (b, 0, 0)),
        pl.BlockSpec((1, KV_W, ncp), lambda b, i: (b, 0, 0)),
        pl.BlockSpec((seq, KV_W), lambda b, i: (b, 0)),
        pl.BlockSpec((KV_W, seq), lambda b, i: (0, b)),
        pl.BlockSpec((seq, KV_W), lambda b, i: (b, 0)),
        pl.BlockSpec((KV_W, seq), lambda b, i: (0, b)),
        full(cband), full(far), full(seld),
        pl.BlockSpec((N_KV, 1) + wbias.shape[2:],
                     lambda b, i: (0, jnp.minimum(i, WIN_VARIANTS - 1), 0, 0)),
        full(cover_t),
    ]
    return pl.pallas_call(
        functools.partial(_nsa_prompt_kernel, nsub=nsub),
        grid=(bsz, nblk),
        in_specs=in_specs,
        out_specs=pl.BlockSpec((Q_BLOCK, D_B), lambda b, i: (b * nblk + i, 0)),
        out_shape=jax.ShapeDtypeStruct((bsz * seq, D_B), F32),
        scratch_shapes=[pltpu.VMEM((N_KV, ncp, cols), F32), pltpu.VMEM((N_KV, ns, cols), F32),
                        pltpu.VMEM((N_KV, 2, nsub * CHUNK, cols), F32)],
        compiler_params=pltpu.CompilerParams(
            dimension_semantics=("parallel", "arbitrary"),
            vmem_limit_bytes=VMEM_LIMIT),
        name="nsa_prompt",
    )(qt, gt, kc, vct, ks, vst, kw, vwt, cband, far, seld, wbias, cover_t)


def _nsa_sample_kernel(tbl_ref, *refs, pgs, npg, past, ds):
    del tbl_ref
    page_refs = refs[:pgs]
    (qbd_ref, gt_ref, kc_ref, vct_ref, win_ref, new_ref, cbias_ref, wbias_ref, sfar_ref, slast_ref,
     snew_ref, cover_ref, rsum_ref, o_ref, msk_ref, m_ref, acc_ref, ocw_ref) = refs[pgs:]
    j = pl.program_id(1)
    qbd = qbd_ref[0]
    ncol = qbd.shape[1]
    qcols = ncol // (N_KV * GQA)
    g = gt_ref[0]

    @pl.when(j == 0)
    def _():
        p_c = _softmax_rows(_dot(kc_ref[0], qbd) + cbias_ref[...])
        o_c = _dot(vct_ref[0], p_c.astype(BF16))
        imp = _split_dot(_split_dot_l(cover_ref[...], p_c), rsum_ref[...])
        lane = lax.broadcasted_iota(jnp.int32, (1, ncol), 1)
        qpos = past + (lane % qcols) % ds
        msk_ref[...] = _select_mask(imp, qpos) + sfar_ref[...]
        kw = win_ref[0, :, 0:KV_W].astype(BF16)
        vw = win_ref[0, :, KV_W:KV_ROW].astype(BF16)
        p_w = _softmax_rows(_dot(kw, qbd) + wbias_ref[...])
        o_w = _dot_t(vw, p_w.astype(BF16))
        ocw_ref[...] = g[0:1, :] * o_c + g[2:3, :] * o_w
        m_ref[...] = jnp.full(m_ref.shape, -jnp.inf, F32)
        acc_ref[...] = jnp.zeros(acc_ref.shape, F32)

    last = j == pl.num_programs(1) - 1
    blocks, vts = [], []
    for k, pr in enumerate(page_refs):
        pidx = j * pgs + k
        kk = pr[0, 0:KV_W, :].T.astype(BF16)
        vts.append(pr[0, KV_W:KV_ROW, :].astype(BF16))
        s = _dot(kk, qbd)
        if k == pgs - 1:
            s = s + jnp.where(last, slast_ref[...], 0.0)
        for hb in range(2):
            blocks.append(s[hb * SEL_BLOCK:(hb + 1) * SEL_BLOCK, :]
                          + msk_ref[pl.ds(2 * pidx + hb, 1), :])
    top = blocks[0]
    for s in blocks[1:]:
        top = jnp.maximum(top, s)
    m = m_ref[...]
    m_new = jnp.maximum(m, jnp.max(top, axis=0, keepdims=True))
    p = jnp.concatenate([jnp.exp2(s - m_new).astype(BF16) for s in blocks], axis=0)
    vt = jnp.concatenate([jnp.concatenate(vts, axis=1), jnp.ones((16, pgs * PAGE_SIZE), BF16)], axis=0)
    acc_ref[...] = jnp.exp2(m - m_new) * acc_ref[...] + _dot(vt, p)
    m_ref[...] = m_new

    @pl.when(last)
    def _():
        kn = new_ref[0, :, 0:KV_W].astype(BF16)
        vn = new_ref[0, :, KV_W:KV_ROW].astype(BF16)
        s = _dot(kn, qbd) + snew_ref[...] + msk_ref[pl.ds(2 * npg, 1), :]
        m = m_ref[...]
        m_new = jnp.maximum(m, jnp.max(s, axis=0, keepdims=True))
        p = jnp.exp2(s - m_new).astype(BF16)
        acc = jnp.exp2(m - m_new) * acc_ref[...]
        num = acc[0:KV_W] + _dot_t(vn, p)
        den = acc[KV_W:KV_W + 1] + jnp.sum(p.astype(F32), axis=0, keepdims=True)
        o = ocw_ref[...] + g[1:2, :] * (num * (1.0 / den))
        for gi in range(N_KV):
            o_ref[0, gi] = o[gi * HEAD_DIM:(gi + 1) * HEAD_DIM, gi * GQA * qcols:(gi + 1) * GQA * qcols]


def _nsa_sample(table, pages, qbd, gt, kc, vct, win, new, cbias, wbias, sfar, slast, snew,
                cover_t, rsum, past, ds):
    nb, npg = table.shape
    pgs = min(16, npg)
    assert npg % pgs == 0
    ncol = qbd.shape[2]
    ns = cover_t.shape[0]

    def page_spec(k):
        return pl.BlockSpec((1, KV_ROW, PAGE_SIZE), lambda b, j, tbl: (tbl[b, j * pgs + k], 0, 0))

    per_b = lambda a: pl.BlockSpec((1,) + a.shape[1:], lambda b, j, tbl: (b,) + (0,) * (a.ndim - 1))
    full = lambda a: pl.BlockSpec(a.shape, lambda b, j, tbl: (0,) * a.ndim)
    grid_spec = pltpu.PrefetchScalarGridSpec(
        num_scalar_prefetch=1,
        grid=(nb, npg // pgs),
        in_specs=[page_spec(k) for k in range(pgs)]
        + [per_b(a) for a in (qbd, gt, kc, vct, win, new)]
        + [full(a) for a in (cbias, wbias, sfar, slast, snew, cover_t, rsum)],
        out_specs=pl.BlockSpec((1, N_KV, HEAD_DIM, ncol // N_KV), lambda b, j, tbl: (b, 0, 0, 0)),
        scratch_shapes=[pltpu.VMEM((ns, ncol), F32), pltpu.VMEM((1, ncol), F32),
                        pltpu.VMEM((KV_W + 16, ncol), F32), pltpu.VMEM((KV_W, ncol), F32)],
    )
    return pl.pallas_call(
        functools.partial(_nsa_sample_kernel, pgs=pgs, npg=npg, past=past, ds=ds),
        grid_spec=grid_spec,
        out_shape=jax.ShapeDtypeStruct((nb, N_KV, HEAD_DIM, ncol // N_KV), F32),
        compiler_params=pltpu.CompilerParams(dimension_semantics=("parallel", "arbitrary"),
                                             vmem_limit_bytes=VMEM_LIMIT),
        name="nsa_sample",
    )(table, *([pages] * pgs), qbd, gt, kc, vct, win, new, cbias, wbias, sfar, slast, snew,
      cover_t, rsum)


def _cover_t(nc, ns, nc_pad, ns_pad):
    c0 = np.arange(nc) * CMP_STRIDE
    s0 = np.arange(ns) * SEL_BLOCK
    m = (c0[None, :] < s0[:, None] + SEL_BLOCK) & (c0[None, :] + CMP_BLOCK > s0[:, None])
    out = np.zeros((ns_pad, nc_pad), np.float32)
    out[:ns, :nc] = m
    return jnp.asarray(out, BF16)


def _prep_weights(norm_g, w_in, ln_v_g, spatial_w, spatial_b, cmp_pe, cmp_w1, cmp_b1, cmp_w2, cmp_b2,
                  w_out, final_g):
    offs = np.cumsum((D_A, D_A, D_A, D_B, 2 * KV_W, 2 * KV_W, 2 * KV_W, 3 * N_HEADS, D_B))
    g0, g1 = int(offs[6]), int(offs[7])
    w_perm = jnp.concatenate(
        [w_in[:, :g0], w_in[:, g1:], w_in[:, g0:g1],
         jnp.zeros((D_MODEL, GATE_PAD - 3 * N_HEADS), w_in.dtype)], axis=1).astype(BF16)
    pavg = jnp.asarray(np.kron(np.eye(A_GROUPS), np.full((HEAD_DIM, HEAD_DIM), 1.0 / HEAD_DIM)), BF16)
    w1cat = jnp.concatenate([cmp_w1[:, s].reshape(2, CMP_FLAT, CMP_HIDDEN) for s in range(CMP_R)],
                            axis=2).astype(BF16)
    pe8 = jnp.concatenate([cmp_pe.reshape(2, CMP_R, CMP_FLAT),
                           jnp.zeros((2, 8 - CMP_R, CMP_FLAT), F32)], axis=1)
    return dict(
        norm_g=norm_g.reshape(1, D_MODEL), w_perm=w_perm, ln_g=ln_v_g.reshape(1, D_A), pavg=pavg,
        w1cat=w1cat, pe8=pe8, b1=cmp_b1.reshape(2, 1, CMP_HIDDEN), w2=cmp_w2.astype(BF16),
        b2=cmp_b2.reshape(2, 1, HEAD_DIM), wo=w_out.astype(BF16), fg=final_g.reshape(1, D_MODEL),
        spatial_w=spatial_w, spatial_b=spatial_b)


def _spatial_operands(spatial_w, spatial_b, n):
    reps = CHUNK // n
    w = jnp.tril(spatial_w[:, :n, :n])
    eye = jnp.eye(reps, dtype=w.dtype)
    wsp = jnp.einsum('ab,gts->gatbs', eye, w).reshape(A_GROUPS, CHUNK, CHUNK).astype(BF16)
    b = jnp.tile(spatial_b[:, :n].T, (reps, 1))
    bsp = jnp.repeat(b, HEAD_DIM, axis=1)
    return wsp, bsp


def _prompt_bias_buckets():
    ql = np.arange(Q_BLOCK)[None, :]
    cl = np.arange(16)[:, None]
    band = np.stack([ql - CMP_STRIDE * cl - (CMP_BLOCK - 1),
                     ql + 97 - CMP_STRIDE * cl])
    kl = np.arange(CHUNK)[:, None]
    seld = np.stack([np.full((CHUNK, Q_BLOCK), FAR_DIST), CHUNK + ql - kl, ql - kl])
    wl = np.arange(WINDOW + Q_BLOCK)[:, None]
    dw = np.stack([Q_BLOCK * v + ql - wl for v in range(WIN_VARIANTS)])
    dw = np.where(dw < WINDOW, dw, -1)
    far = np.full((8, Q_BLOCK), FAR_DIST)
    tile4 = lambda d: np.tile(_t5_bucket_np(d.reshape(-1, Q_BLOCK)), (1, GQA))
    return tile4(band), tile4(seld), tile4(dw), tile4(far)


def _sample_bias_buckets(past, ds, qpad, ncp, nwin_pad):
    ql = (np.arange(qpad) % ds)[None, :]
    c = np.arange(ncp)[:, None]
    dc = past + ql - (CMP_STRIDE * c + CMP_BLOCK - 1)
    wl = np.arange(nwin_pad)[:, None]
    dw = WINDOW + ql - wl
    dw = np.where((dw < WINDOW) & (wl < WINDOW + ds), dw, -1)
    kl = np.arange(PAGE_SIZE)[:, None]
    dlast = PAGE_SIZE + ql - kl
    nl = np.arange(16)[:, None]
    dnew = np.where(nl < ds, ql - nl, -1)
    far = np.full((8, qpad), FAR_DIST)
    tile4 = lambda d: np.tile(_t5_bucket_np(d), (1, GQA))
    return tile4(dc), tile4(dw), tile4(dlast), tile4(dnew), tile4(far)


def kernel(x_prompt, x_sample, cache_cmp_kv, cache_sel_kv, state_win_kv, page_table, norm_g, w_in,
           ln_v_g, spatial_w, spatial_b, cmp_pe, cmp_w1, cmp_b1, cmp_w2, cmp_b2, rel_table, w_out,
           final_g):
    depth = norm_g.shape[0]
    assert depth == 1
    bsz, seq = x_prompt.shape[:2]
    db, ds = x_sample.shape[:2]
    npg = page_table.shape[1]
    past = npg * PAGE_SIZE
    win_buf = state_win_kv.shape[2]
    assert win_buf == WINDOW and past >= WINDOW and seq % Q_BLOCK == 0 and seq >= WINDOW + Q_BLOCK
    assert CHUNK % ds == 0 and ds <= 8 and (db * ds) % CHUNK == 0
    nblk = seq // Q_BLOCK
    l = 0
    wts = _prep_weights(norm_g[l], w_in[l], ln_v_g[l], spatial_w[l], spatial_b[l], cmp_pe[l],
                        cmp_w1[l], cmp_b1[l], cmp_w2[l], cmp_b2[l], w_out[l], final_g)
    inproj = lambda x, nb: _inproj(x, nb, wts['norm_g'], wts['w_perm'], wts['ln_g'], wts['pavg'])
    compress = lambda tbl, pages: _compress(tbl, pages, wts['w1cat'], wts['pe8'], wts['b1'],
                                            wts['w2'], wts['b2'])
    kv_out = lambda a: a.reshape(a.shape[0], 2, N_KV, HEAD_DIM, a.shape[2]).transpose(0, 4, 1, 2, 3)[None]
    pages_t = lambda c: c.transpose(0, 2, 3, 4, 1).reshape(c.shape[0], KV_ROW, PAGE_SIZE)

    n_p = bsz * seq
    xp = x_prompt.reshape(n_p, D_MODEL)
    ua, vn, zas, zbs, kvc, kvs, kvw, qt, gt, ksk, vst, kwk, vwt = inproj(xp, bsz)
    kc, vct = compress(None, kvc)
    ncp = seq // CMP_STRIDE
    cover_p = _cover_t(ncp - CMP_R + 1, seq // SEL_BLOCK, ncp, seq // SEL_BLOCK)
    band_b, seld_b, win_b, far_b = _prompt_bias_buckets()
    cols = GQA * Q_BLOCK
    cband = _bias_tiles(rel_table, band_b, Q_BLOCK).reshape(N_KV, 2, 16, cols)
    seld = _bias_tiles(rel_table, seld_b, Q_BLOCK, rel_far=True).reshape(N_KV, 3, CHUNK, cols)
    wbias = _bias_tiles(rel_table, win_b, Q_BLOCK).reshape(N_KV, WIN_VARIANTS, WINDOW + Q_BLOCK, cols)
    far = _bias_tiles(rel_table, far_b, Q_BLOCK)[:, 0:1]
    ob = _nsa_prompt(qt, gt, kc, vct, ksk, vst, kwk, vwt, cband, far, seld, wbias, cover_p, bsz, seq)
    wsp, bsp = _spatial_operands(wts['spatial_w'], wts['spatial_b'], CHUNK)
    y_prompt = _mixout(xp, ua, vn, zas, ob, zbs, wsp, bsp, wts['wo'], wts['fg']).reshape(bsz, seq, D_MODEL)
    new_cmp_p = kv_out(kvc)
    new_sel_p = kv_out(kvs)
    new_win_p = kv_out(kvw[:, :, seq - win_buf:])

    n_s = db * ds
    xs = x_sample.reshape(n_s, D_MODEL)
    ua, vn, zas, zbs, kvc, kvs, kvw, qt, gt, _, _, _, _ = inproj(xs, 1)
    kvc, kvs, kvw = (a[0].T for a in (kvc, kvs, kvw))
    kc, vct = compress(page_table, pages_t(cache_cmp_kv[l]))
    ncs = past // CMP_STRIDE
    t_all = past + ds
    nss = -(-t_all // SEL_BLOCK)
    nss_pad = -(-nss // 8) * 8
    cover_s = _cover_t(t_all // CMP_STRIDE - CMP_R + 1, nss, ncs, nss_pad)
    qpad = 32
    ncol = N_KV * GQA * qpad
    nwin_pad = -(-(win_buf + ds) // 16) * 16
    dc_b, dw_b, dl_b, dn_b, far_b = _sample_bias_buckets(past, ds, qpad, ncs, nwin_pad)
    both = lambda t: jnp.concatenate([t[0], t[1]], axis=1)
    cbias_s = both(_bias_tiles(rel_table, dc_b, qpad))
    wbias_s = both(_bias_tiles(rel_table, dw_b, qpad))
    slast = both(_bias_tiles(rel_table, dl_b, qpad, rel_far=True))
    snew = both(_bias_tiles(rel_table, dn_b, qpad, rel_far=True))
    sfar = both(_bias_tiles(rel_table, far_b, qpad))[0:1]

    qg = qt.reshape(N_KV, GQA, HEAD_DIM, db, ds).transpose(3, 0, 2, 1, 4)
    qg = jnp.tile(qg, (1, 1, 1, 1, qpad // ds)).reshape(db, N_KV, HEAD_DIM, GQA * qpad)
    zq = jnp.zeros_like(qg[:, 0])
    qbd = jnp.concatenate([jnp.concatenate([qg[:, 0], zq], axis=2),
                           jnp.concatenate([zq, qg[:, 1]], axis=2)], axis=1)
    gts = gt[:3 * N_HEADS].reshape(N_KV, GQA, 3, db, ds).transpose(3, 2, 0, 1, 4)
    gts = jnp.tile(gts, (1, 1, 1, 1, qpad // ds)).reshape(db, 3, ncol)
    win = jnp.concatenate([state_win_kv[l].reshape(db, win_buf, KV_ROW),
                           kvw.reshape(db, ds, KV_ROW)], axis=1)
    win_pad = jnp.pad(win, ((0, 0), (0, nwin_pad - win_buf - ds), (0, 0)))
    new_pad = jnp.pad(kvs.reshape(db, ds, KV_ROW), ((0, 0), (0, 16 - ds), (0, 0)))
    rsum = jnp.asarray(np.kron(np.eye(N_KV), np.kron(np.ones((GQA, GQA)), np.eye(qpad))), BF16)
    o_s = _nsa_sample(page_table, pages_t(cache_sel_kv[l]), qbd, gts,
                      kc, vct, win_pad, new_pad, cbias_s, wbias_s, sfar, slast, snew,
                      cover_s, rsum, past, ds)
    ob = (o_s.reshape(db, N_KV, HEAD_DIM, GQA, qpad)[..., :ds].transpose(0, 4, 1, 3, 2)
          .reshape(n_s, D_B))
    wsp, bsp = _spatial_operands(wts['spatial_w'], wts['spatial_b'], ds)
    y_sample = _mixout(xs, ua, vn, zas, ob, zbs, wsp, bsp, wts['wo'], wts['fg']).reshape(db, ds, D_MODEL)
    kv5 = lambda a: a.reshape(1, db, ds, 2, N_KV, HEAD_DIM)
    new_cmp_s = kv5(kvc)
    new_sel_s = kv5(kvs)
    new_win_s = win[:, ds:].reshape(1, db, win_buf, 2, N_KV, HEAD_DIM)
    new_chunk_v = vn.reshape(1, db, ds, D_A)
    return (y_prompt, y_sample, new_cmp_p, new_sel_p, new_win_p, new_cmp_s, new_sel_s, new_win_s,
            new_chunk_v)
```

```python
import functools
import math

import numpy as np
import jax
import jax.numpy as jnp
from jax import lax
from jax.experimental import pallas as pl
from jax.experimental.pallas import tpu as pltpu

F32 = jnp.float32
BF16 = jnp.bfloat16

D_MODEL = 1024
HEAD_DIM = 64
D_A = 512
D_B = 512
A_GROUPS = D_A // HEAD_DIM
CHUNK = 128
N_HEADS = D_B // HEAD_DIM
N_KV = 2
GQA = N_HEADS // N_KV
KV_W = N_KV * HEAD_DIM
KV_ROW = 2 * KV_W
CMP_STRIDE = 16
CMP_BLOCK = 32
CMP_R = CMP_BLOCK // CMP_STRIDE
CMP_HIDDEN = 256
CMP_FLAT = CMP_STRIDE * HEAD_DIM
SEL_BLOCK = 64
N_SEL = 16
WINDOW = 512
N_BUCKETS = 32
MAX_DISTANCE = 128
Q_BLOCK = 128
PAGE_SIZE = 128
RMS_EPS = 1e-6
LN_EPS = 1e-5
NEG = -1e30
FORCE_BONUS = 1e6
LOG2E = 1.4426950408889634
Q_SCALE = HEAD_DIM ** -0.5 * LOG2E
FAR_DIST = 1 << 20
WIN_VARIANTS = WINDOW // Q_BLOCK + 1

_OFF_U, _OFF_V, _OFF_ZA, _OFF_Q = 0, 512, 1024, 1536
_OFF_KVC, _OFF_KVS, _OFF_KVW, _OFF_ZB, _OFF_G = 2048, 2304, 2560, 2816, 3328
D_IN_PAD = 3456
GATE_PAD = 128
GATE_ROWS = 32

VMEM_LIMIT = 52 * 1024 * 1024


def _gelu(x):
    return x * (0.5 * (1.0 + jnp.tanh(0.7978845608028654 * (x + 0.044715 * (x * x * x)))))


def _sigmoid(x):
    return 1.0 / (1.0 + jnp.exp(-x))


def _dot(a, b):
    return jnp.dot(a, b, preferred_element_type=F32)


def _dot_t(a, b):
    return lax.dot_general(a, b, (((0,), (0,)), ((), ())), preferred_element_type=F32)


def _dot_nt(a, b):
    return lax.dot_general(a, b, (((1,), (1,)), ((), ())), preferred_element_type=F32)


def _split_dot(a, b):
    hi = a.astype(BF16)
    lo = (a - hi.astype(F32)).astype(BF16)
    return _dot(hi, b) + _dot(lo, b)


def _split_dot_l(a, b):
    hi = b.astype(BF16)
    lo = (b - hi.astype(F32)).astype(BF16)
    return _dot(a, hi) + _dot(a, lo)


def _inproj_kernel(x_ref, ng_ref, w_ref, lng_ref, pavg_ref,
                   ua_ref, vn_ref, vnb_ref, zas_ref, zbs_ref, kvc_ref, kvs_ref, kvw_ref,
                   qt_ref, gt_ref, ksk_ref, vst_ref, kwk_ref, vwt_ref):
    x = x_ref[...]
    ms = jnp.mean(x * x, axis=-1, keepdims=True)
    h = (x * lax.rsqrt(ms + RMS_EPS) * ng_ref[...]).astype(BF16)

    def proj(a, b):
        return _dot(h, w_ref[:, a:b])

    ua_ref[...] = _gelu(proj(_OFF_U, _OFF_V)).astype(BF16)
    v = _gelu(proj(_OFF_V, _OFF_ZA))
    mu = _split_dot(v, pavg_ref[...])
    d = v - mu
    var = _split_dot(d * d, pavg_ref[...])
    vn = d * lax.rsqrt(var + LN_EPS) * lng_ref[...]
    vn_ref[...] = vn
    vnb_ref[...] = vn.astype(BF16)
    za = proj(_OFF_ZA, _OFF_Q)
    zas_ref[...] = (za * _sigmoid(za)).astype(BF16)
    zb = proj(_OFF_ZB, _OFF_G)
    zbs_ref[...] = (zb * _sigmoid(zb)).astype(BF16)
    qt_ref[...] = (proj(_OFF_Q, _OFF_KVC) * Q_SCALE).T.astype(BF16)
    gt_ref[...] = _sigmoid(proj(_OFF_G, D_IN_PAD)).T[0:GATE_ROWS, :]
    kvc_ref[0] = proj(_OFF_KVC, _OFF_KVS).T
    kvs = proj(_OFF_KVS, _OFF_KVW)
    kvs_t = kvs.T
    kvs_ref[0] = kvs_t
    ksk_ref[...] = kvs[:, 0:KV_W].astype(BF16)
    vst_ref[...] = kvs_t[KV_W:KV_ROW, :].astype(BF16)
    kvw = proj(_OFF_KVW, _OFF_ZB)
    kvw_t = kvw.T
    kvw_ref[0] = kvw_t
    kwk_ref[...] = kvw[:, 0:KV_W].astype(BF16)
    vwt_ref[...] = kvw_t[KV_W:KV_ROW, :].astype(BF16)


def _inproj(x, nb, norm_g, w_perm, ln_g, pavg):
    n = x.shape[0]
    s = n // nb
    tm = min(256, s)
    assert s % tm == 0
    per = s // tm
    row = lambda w: pl.BlockSpec((tm, w), lambda i: (i, 0))
    col = lambda h: pl.BlockSpec((h, tm), lambda i: (0, i))
    kvt = pl.BlockSpec((1, KV_ROW, tm), lambda i: (i // per, 0, i % per))
    full = lambda a: pl.BlockSpec(a.shape, lambda i: (0,) * a.ndim)
    rows = [(D_A, BF16), (D_A, F32), (D_A, BF16), (D_A, BF16), (D_B, BF16)]
    sds = lambda shape, dt: jax.ShapeDtypeStruct(shape, dt)
    out_specs = ([row(w) for w, _ in rows] + [kvt, kvt, kvt]
                 + [col(D_B), col(GATE_ROWS), row(KV_W), col(KV_W), row(KV_W), col(KV_W)])
    out_shape = ([sds((n, w), dt) for w, dt in rows] + [sds((nb, KV_ROW, s), F32)] * 3
                 + [sds((D_B, n), BF16), sds((GATE_ROWS, n), F32), sds((n, KV_W), BF16),
                    sds((KV_W, n), BF16), sds((n, KV_W), BF16), sds((KV_W, n), BF16)])
    return pl.pallas_call(
        _inproj_kernel,
        grid=(n // tm,),
        in_specs=[row(D_MODEL), full(norm_g), full(w_perm), full(ln_g), full(pavg)],
        out_specs=out_specs,
        out_shape=out_shape,
        compiler_params=pltpu.CompilerParams(dimension_semantics=("parallel",),
                                             vmem_limit_bytes=VMEM_LIMIT),
        name="inproj",
    )(x, norm_g, w_perm, ln_g, pavg)


def _mixout_kernel(x_ref, ua_ref, vn_ref, zas_ref, ob_ref, zbs_ref, wsp_ref, bsp_ref, wo_ref, fg_ref,
                   y_ref, s_ref):
    tm = x_ref.shape[0]
    for c in range(tm // CHUNK):
        rows = slice(c * CHUNK, (c + 1) * CHUNK)
        vc = vn_ref[rows, :]
        for g in range(A_GROUPS):
            cols = slice(g * HEAD_DIM, (g + 1) * HEAD_DIM)
            s_ref[rows, cols] = _dot(wsp_ref[g], vc[:, cols])
        s_ref[rows, :] = s_ref[rows, :] + bsp_ref[...]
    mix_a = (ua_ref[...].astype(F32) * s_ref[...] * zas_ref[...].astype(F32)).astype(BF16)
    mix_b = (ob_ref[...].astype(F32) * zbs_ref[...].astype(F32)).astype(BF16)
    y = x_ref[...] + _dot(mix_a, wo_ref[0:D_A, :]) + _dot(mix_b, wo_ref[D_A:D_A + D_B, :])
    ms = jnp.mean(y * y, axis=-1, keepdims=True)
    y_ref[...] = y * lax.rsqrt(ms + RMS_EPS) * fg_ref[...]


def _mixout(x, ua, vn, zas, ob, zbs, wsp, bsp, wo, fg):
    n = x.shape[0]
    tm = min(256, n)
    row = lambda w: pl.BlockSpec((tm, w), lambda i: (i, 0))
    full = lambda a: pl.BlockSpec(a.shape, lambda i: (0,) * a.ndim)
    return pl.pallas_call(
        _mixout_kernel,
        grid=(n // tm,),
        in_specs=[row(D_MODEL), row(D_A), row(D_A), row(D_A), row(D_B), row(D_B),
                  full(wsp), full(bsp), full(wo), full(fg)],
        out_specs=row(D_MODEL),
        out_shape=jax.ShapeDtypeStruct((n, D_MODEL), F32),
        scratch_shapes=[pltpu.VMEM((tm, D_A), F32)],
        compiler_params=pltpu.CompilerParams(dimension_semantics=("parallel",),
                                             vmem_limit_bytes=VMEM_LIMIT),
        name="mixout",
    )(x, ua, vn, zas, ob, zbs, wsp, bsp, wo, fg)


def _t5_bucket_np(dist):
    dist = np.asarray(dist, np.int64)
    n = np.maximum(dist, 0)
    max_exact = N_BUCKETS // 2
    nf = np.maximum(n, max_exact).astype(np.float64)
    large = max_exact + (np.log(nf / max_exact) / math.log(MAX_DISTANCE / max_exact)
                         * (N_BUCKETS - max_exact)).astype(np.int64)
    b = np.where(n < max_exact, n, np.minimum(large, N_BUCKETS - 1))
    return np.where(dist < 0, -1, b).astype(np.int32)


def _bias_kernel(tbl_ref, bkt_ref, out_ref, *, qb, rel_far):
    b = bkt_ref[...]
    grp = lax.broadcasted_iota(jnp.int32, (1, b.shape[1]), 1) // qb

    def head_row(k, g):
        row = jnp.zeros(grp.shape, F32)
        for r in range(GQA):
            row = jnp.where(grp == r, tbl_ref[k * N_HEADS + g * GQA + r], row)
        return row * LOG2E

    for g in range(N_KV):
        acc = jnp.full(b.shape, NEG, F32)
        base = head_row(N_BUCKETS - 1, g) if rel_far else None
        for k in range(N_BUCKETS):
            row = head_row(k, g)
            if rel_far:
                row = row - base
            acc = jnp.where(b == k, row, acc)
        out_ref[g] = acc


def _bias_tiles(rel_table, bkt, qb, rel_far=False):
    r, c = bkt.shape
    rb = r
    for cand in (512, 256, 128, 64, 32, 16, 8):
        if r % cand == 0:
            rb = cand
            break
    return pl.pallas_call(
        functools.partial(_bias_kernel, qb=qb, rel_far=rel_far),
        grid=(r // rb,),
        in_specs=[pl.BlockSpec(memory_space=pltpu.SMEM), pl.BlockSpec((rb, c), lambda i: (i, 0))],
        out_specs=pl.BlockSpec((N_KV, rb, c), lambda i: (0, i, 0)),
        out_shape=jax.ShapeDtypeStruct((N_KV, r, c), F32),
        compiler_params=pltpu.CompilerParams(dimension_semantics=("parallel",)),
        name="bias_tiles",
    )(rel_table.reshape(-1), jnp.asarray(bkt))


def _compress_kernel(tbl_ref, *refs, pg):
    del tbl_ref
    page_refs = refs[:pg + 1]
    perm_ref, w1_ref, pe_ref, b1_ref, w2_ref, b2_ref, kc_ref, vct_ref, x_ref = refs[pg + 1:]
    mp = (pg + 1) * 8
    nrow = pg * 8
    left = lax.broadcasted_iota(jnp.int32, (8, 128), 1) < HEAD_DIM
    perm = perm_ref[...]
    for k, pr in enumerate(page_refs):
        for kv in range(2):
            half = pr[0, kv * KV_W:(kv + 1) * KV_W, :].astype(BF16)
            tok = _dot_nt(perm, half)
            for t in range(CMP_STRIDE // 2):
                e = tok[16 * t:16 * t + 8, :]
                o = tok[16 * t + 8:16 * t + 16, :]
                sw = pltpu.roll(jnp.where(left, o, e), HEAD_DIM, 1)
                x_ref[kv, 8 * k:8 * k + 8, 128 * t:128 * t + 128] = jnp.where(left, e, sw)
                x_ref[kv, mp + 8 * k:mp + 8 * k + 8, 128 * t:128 * t + 128] = jnp.where(left, sw, o)
    outs = []
    for kv in range(2):
        x_ref[kv, 2 * mp:2 * mp + 8, :] = pe_ref[kv]
        p = _dot(x_ref[kv].astype(BF16), w1_ref[kv])
        hc = (b1_ref[kv] + p[2 * mp:2 * mp + 1, 0:CMP_HIDDEN]
              + p[2 * mp + 1:2 * mp + 2, CMP_HIDDEN:2 * CMP_HIDDEN])
        per_g = []
        for g in range(N_KV):
            base = g * mp
            h = (p[base:base + nrow, 0:CMP_HIDDEN]
                 + p[base + 1:base + nrow + 1, CMP_HIDDEN:2 * CMP_HIDDEN] + hc)
            per_g.append(_dot(_gelu(h).astype(BF16), w2_ref[kv]) + b2_ref[kv])
        outs.append(jnp.concatenate(per_g, axis=1))
    kc_ref[0] = outs[0].astype(BF16)
    vct_ref[0] = outs[1].T.astype(BF16)


def _compress(table, pages, w1cat, pe8, b1, w2, b2):
    if table is None:
        nb, npg = pages.shape[0], pages.shape[2] // PAGE_SIZE
        table = jnp.zeros((1, 1), jnp.int32)
        index = lambda b, idx, tbl: (b, 0, idx)
    else:
        nb, npg = table.shape
        index = lambda b, idx, tbl: (tbl[b, idx], 0, 0)
    pg = min(64, npg)
    assert npg % pg == 0
    mp = (pg + 1) * 8

    def page_spec(k):
        return pl.BlockSpec(
            (1, KV_ROW, PAGE_SIZE),
            lambda b, j, tbl: index(b, jnp.minimum(j * pg + k, npg - 1), tbl))

    tok = np.arange(PAGE_SIZE)
    perm_np = np.zeros((PAGE_SIZE, PAGE_SIZE), np.float32)
    perm_np[(tok % CMP_STRIDE) * (PAGE_SIZE // CMP_STRIDE) + tok // CMP_STRIDE, tok] = 1.0
    perm = jnp.asarray(perm_np, BF16)
    full = lambda a: pl.BlockSpec(a.shape, lambda b, j, tbl: (0,) * a.ndim)
    grid_spec = pltpu.PrefetchScalarGridSpec(
        num_scalar_prefetch=1,
        grid=(nb, npg // pg),
        in_specs=[page_spec(k) for k in range(pg + 1)]
        + [full(a) for a in (perm, w1cat, pe8, b1, w2, b2)],
        out_specs=[pl.BlockSpec((1, pg * 8, KV_W), lambda b, j, tbl: (b, j, 0)),
                   pl.BlockSpec((1, KV_W, pg * 8), lambda b, j, tbl: (b, 0, j))],
        scratch_shapes=[pltpu.VMEM((2, 2 * mp + 8, CMP_FLAT), F32)],
    )
    return pl.pallas_call(
        functools.partial(_compress_kernel, pg=pg),
        grid_spec=grid_spec,
        out_shape=[jax.ShapeDtypeStruct((nb, npg * 8, KV_W), BF16),
                   jax.ShapeDtypeStruct((nb, KV_W, npg * 8), BF16)],
        compiler_params=pltpu.CompilerParams(dimension_semantics=("parallel", "parallel"),
                                             vmem_limit_bytes=VMEM_LIMIT),
        name="compress",
    )(table, *([pages] * (pg + 1)), perm, w1cat, pe8, b1, w2, b2)


def _topk_rows(imp, n_sel):
    ns = imp.shape[0]
    blk = lax.broadcasted_iota(jnp.int32, imp.shape, 0).astype(F32)
    for _ in range(n_sel):
        mx = jnp.max(imp, axis=0, keepdims=True)
        idx = jnp.min(jnp.where(imp == mx, blk, float(ns)), axis=0, keepdims=True)
        imp = jnp.where(blk == idx, -jnp.inf, imp)
    return imp == -jnp.inf


def _select_mask(imp, qpos):
    blk = lax.broadcasted_iota(jnp.int32, imp.shape, 0)
    cur = qpos // SEL_BLOCK
    forced = (blk == 0) | (blk == cur) | (blk == cur - 1)
    valid = blk * SEL_BLOCK <= qpos
    imp = imp + jnp.where(forced, FORCE_BONUS, 0.0)
    imp = jnp.where(valid, imp, NEG)
    sel = _topk_rows(imp, N_SEL)
    return jnp.where(sel & valid, 0.0, NEG)


def _softmax_rows(s):
    m = jnp.max(s, axis=0, keepdims=True)
    e = jnp.exp2(s - m)
    inv = jnp.where(m > NEG / 2, 1.0 / jnp.sum(e, axis=0, keepdims=True), 0.0)
    return e * inv


def _online_update(s, v_dot, m, l, acc):
    m_new = jnp.maximum(m, jnp.max(s, axis=0, keepdims=True))
    alpha = jnp.exp2(m - m_new)
    p = jnp.exp2(s - m_new)
    l = alpha * l + jnp.sum(p, axis=0, keepdims=True)
    acc = alpha * acc + v_dot(p.astype(BF16))
    return m_new, l, acc


def _nsa_prompt_kernel(qt_ref, gt_ref, kc_ref, vct_ref, ks_ref, vst_ref, kw_ref, vwt_ref,
                       cband_ref, far_ref, seld_ref, wbias_ref, cover_ref,
                       o_ref, bias_ref, msk_ref, s_ref, *, nsub):
    i = pl.program_id(1)
    cols = GQA * Q_BLOCK
    ncp = kc_ref.shape[1]
    gw = GQA * HEAD_DIM
    qpos = i * Q_BLOCK + lax.broadcasted_iota(jnp.int32, (1, Q_BLOCK), 1)
    n_trips = i // nsub + 1
    tk = nsub * CHUNK
    c0 = pl.multiple_of(jnp.maximum(i * 8 - 8, 0), 8)
    crow = lax.broadcasted_iota(jnp.int32, (ncp, 1), 0)
    woff = pl.multiple_of(jnp.maximum(i * Q_BLOCK - WINDOW, 0), Q_BLOCK)
    nwk = WINDOW + Q_BLOCK

    def query(g):
        qblk = qt_ref[g * gw:(g + 1) * gw, :]
        q64 = jnp.concatenate([qblk[r * HEAD_DIM:(r + 1) * HEAD_DIM, :] for r in range(GQA)], axis=1)
        zero = jnp.zeros_like(q64)
        return jnp.concatenate([q64, zero] if g == 0 else [zero, q64], axis=0)

    def dense_branches(g, q):
        far = far_ref[g]
        bias_ref[g] = jnp.where(crow < c0, far, NEG)
        bias_ref[g, pl.ds(c0, 16), :] = cband_ref[g, jnp.minimum(i, 1)]
        p_c = _softmax_rows(_dot(kc_ref[0], q) + bias_ref[g])
        o_c = _dot(vct_ref[0, g * HEAD_DIM:(g + 1) * HEAD_DIM, :], p_c.astype(BF16))
        psum = (p_c[:, 0:Q_BLOCK] + p_c[:, Q_BLOCK:2 * Q_BLOCK]
                + p_c[:, 2 * Q_BLOCK:3 * Q_BLOCK] + p_c[:, 3 * Q_BLOCK:4 * Q_BLOCK])
        imp = _split_dot_l(cover_ref[...], psum)
        p_w = _softmax_rows(_dot(kw_ref[pl.ds(woff, nwk), :], q) + wbias_ref[g, 0])
        o_w = _dot(vwt_ref[g * HEAD_DIM:(g + 1) * HEAD_DIM, pl.ds(woff, nwk)], p_w.astype(BF16))
        mask = _select_mask(imp, qpos)
        msk_ref[g] = jnp.concatenate([mask] * GQA, axis=1) + far
        return o_c, o_w

    def scores(g, q, t, slot):
        pen = jnp.where(t < n_trips, 0.0, NEG)
        t = jnp.minimum(t, n_trips - 1)
        for u in range(nsub):
            kb = t * nsub + u
            off = pl.multiple_of(kb * CHUNK, CHUNK)
            for hb in range(2):
                rows = pl.ds(off + hb * SEL_BLOCK, SEL_BLOCK)
                s = _dot(ks_ref[rows, :], q) + (msk_ref[g, pl.ds(2 * kb + hb, 1), :] + pen)
                s_ref[g, slot, (2 * u + hb) * SEL_BLOCK:(2 * u + hb + 1) * SEL_BLOCK, :] = s

    def attend(g, t, slot, near, carry):
        s = s_ref[g, slot]
        if near:
            s = s + jnp.concatenate(
                [seld_ref[g, jnp.clip(t * nsub + u - i + 2, 0, 2)] for u in range(nsub)], axis=0)
        koff = pl.multiple_of(jnp.minimum(t, n_trips - 1) * tk, tk)
        vt = jnp.concatenate([vst_ref[g * HEAD_DIM:(g + 1) * HEAD_DIM, pl.ds(koff, tk)],
                              jnp.ones((16, tk), BF16)], axis=0)
        m, acc = carry
        m_new = jnp.maximum(m, jnp.max(s, axis=0, keepdims=True))
        p = jnp.exp2(s - m_new).astype(BF16)
        return m_new, jnp.exp2(m - m_new) * acc + _dot(vt, p)

    def selected(g, q):
        def make_pair(near):
            def pair(tt, carry):
                scores(g, q, 2 * tt + 1, 1)
                carry = attend(g, 2 * tt, 0, near, carry)
                scores(g, q, 2 * tt + 2, 0)
                return attend(g, 2 * tt + 1, 1, near, carry)
            return pair

        n_far_pairs = (jnp.maximum(i - 1, 0) // nsub) // 2
        init = (jnp.full((1, cols), -jnp.inf, F32), jnp.zeros((HEAD_DIM + 16, cols), F32))
        carry = lax.fori_loop(0, n_far_pairs, make_pair(False), init)
        _, acc_s = lax.fori_loop(n_far_pairs, (n_trips + 1) // 2, make_pair(True), carry)
        return acc_s[0:HEAD_DIM] * (1.0 / acc_s[HEAD_DIM:HEAD_DIM + 1])

    gt = gt_ref[...]

    def gate(g, j):
        return jnp.concatenate([gt[(g * GQA + r) * 3 + j:(g * GQA + r) * 3 + j + 1, :]
                                for r in range(GQA)], axis=1)

    qs = [query(g) for g in range(N_KV)]
    dense = [dense_branches(g, qs[g]) for g in range(N_KV)]
    for g in range(N_KV):
        scores(g, qs[g], 0, 0)
    for g in range(N_KV):
        o_c, o_w = dense[g]
        o = gate(g, 0) * o_c + gate(g, 1) * selected(g, qs[g]) + gate(g, 2) * o_w
        o_ref[:, g * gw:(g + 1) * gw] = jnp.concatenate(
            [o[:, r * Q_BLOCK:(r + 1) * Q_BLOCK].T for r in range(GQA)], axis=1).astype(BF16)


def _nsa_prompt(qt, gt, kc, vct, ks, vst, kw, vwt, cband, far, seld, wbias, cover_t, bsz, seq):
    nblk = seq // Q_BLOCK
    ncp = kc.shape[1]
    ns = cover_t.shape[0]
    cols = GQA * Q_BLOCK
    nsub = next(c for c in (4, 2, 1) if nblk % c == 0)
    full = lambda a: pl.BlockSpec(a.shape, lambda b, i: (0,) * a.ndim)
    in_specs = [
        pl.BlockSpec((D_B, Q_BLOCK), lambda b, i: (0, b * nblk + i)),
        pl.BlockSpec((GATE_ROWS, Q_BLOCK), lambda b, i: (0, b * nblk + i)),
        pl.BlockSpec((1, ncp, KV_W), lambda b, i: (b, 0, 0)),
        pl.BlockSpec((1, KV_W, ncp), lambda b, i: (b, 0, 0)),
        pl.BlockSpec((seq, KV_W), lambda b, i: (b, 0)),
        pl.BlockSpec((KV_W, seq), lambda b, i: (0, b)),
        pl.BlockSpec((seq, KV_W), lambda b, i: (b, 0)),
        pl.BlockSpec((KV_W, seq), lambda b, i: (0, b)),
        full(cband), full(far), full(seld),
        pl.BlockSpec((N_KV, 1) + wbias.shape[2:],
                     lambda b, i: (0, jnp.minimum(i, WIN_VARIANTS - 1), 0, 0)),
        full(cover_t),
    ]
    return pl.pallas_call(
        functools.partial(_nsa_prompt_kernel, nsub=nsub),
        grid=(bsz, nblk),
        in_specs=in_specs,
        out_specs=pl.BlockSpec((Q_BLOCK, D_B), lambda b, i: (b * nblk + i, 0)),
        out_shape=jax.ShapeDtypeStruct((bsz * seq, D_B), BF16),
        scratch_shapes=[pltpu.VMEM((N_KV, ncp, cols), F32), pltpu.VMEM((N_KV, ns, cols), F32),
                        pltpu.VMEM((N_KV, 2, nsub * CHUNK, cols), F32)],
        compiler_params=pltpu.CompilerParams(
            dimension_semantics=("parallel", "arbitrary"),
            vmem_limit_bytes=VMEM_LIMIT),
        name="nsa_prompt",
    )(qt, gt, kc, vct, ks, vst, kw, vwt, cband, far, seld, wbias, cover_t)


def _nsa_sample_kernel(tbl_ref, *refs, pgs, npg, past, ds):
    del tbl_ref
    page_refs = refs[:pgs]
    (qbd_ref, gt_ref, kc_ref, vct_ref, win_ref, new_ref, cbias_ref, wbias_ref, sfar_ref, slast_ref,
     snew_ref, cover_ref, rsum_ref, o_ref, msk_ref, m_ref, acc_ref, ocw_ref) = refs[pgs:]
    j = pl.program_id(1)
    qbd = qbd_ref[0]
    ncol = qbd.shape[1]
    qcols = ncol // (N_KV * GQA)
    g = gt_ref[0]

    @pl.when(j == 0)
    def _():
        p_c = _softmax_rows(_dot(kc_ref[0], qbd) + cbias_ref[...])
        o_c = _dot(vct_ref[0], p_c.astype(BF16))
        imp = _split_dot(_split_dot_l(cover_ref[...], p_c), rsum_ref[...])
        lane = lax.broadcasted_iota(jnp.int32, (1, ncol), 1)
        qpos = past + (lane % qcols) % ds
        msk_ref[...] = _select_mask(imp, qpos) + sfar_ref[...]
        kw = win_ref[0, :, 0:KV_W].astype(BF16)
        vw = win_ref[0, :, KV_W:KV_ROW].astype(BF16)
        p_w = _softmax_rows(_dot(kw, qbd) + wbias_ref[...])
        o_w = _dot_t(vw, p_w.astype(BF16))
        ocw_ref[...] = g[0:1, :] * o_c + g[2:3, :] * o_w
        m_ref[...] = jnp.full(m_ref.shape, -jnp.inf, F32)
        acc_ref[...] = jnp.zeros(acc_ref.shape, F32)

    last = j == pl.num_programs(1) - 1
    blocks, vts = [], []
    for k, pr in enumerate(page_refs):
        pidx = j * pgs + k
        kk = pr[0, 0:KV_W, :].T.astype(BF16)
        vts.append(pr[0, KV_W:KV_ROW, :].astype(BF16))
        s = _dot(kk, qbd)
        if k == pgs - 1:
            s = s + jnp.where(last, slast_ref[...], 0.0)
        for hb in range(2):
            blocks.append(s[hb * SEL_BLOCK:(hb + 1) * SEL_BLOCK, :]
                          + msk_ref[pl.ds(2 * pidx + hb, 1), :])
    top = blocks[0]
    for s in blocks[1:]:
        top = jnp.maximum(top, s)
    m = m_ref[...]
    m_new = jnp.maximum(m, jnp.max(top, axis=0, keepdims=True))
    p = jnp.concatenate([jnp.exp2(s - m_new).astype(BF16) for s in blocks], axis=0)
    vt = jnp.concatenate([jnp.concatenate(vts, axis=1), jnp.ones((16, pgs * PAGE_SIZE), BF16)], axis=0)
    acc_ref[...] = jnp.exp2(m - m_new) * acc_ref[...] + _dot(vt, p)
    m_ref[...] = m_new

    @pl.when(last)
    def _():
        kn = new_ref[0, :, 0:KV_W].astype(BF16)
        vn = new_ref[0, :, KV_W:KV_ROW].astype(BF16)
        s = _dot(kn, qbd) + snew_ref[...] + msk_ref[pl.ds(2 * npg, 1), :]
        m = m_ref[...]
        m_new = jnp.maximum(m, jnp.max(s, axis=0, keepdims=True))
        p = jnp.exp2(s - m_new).astype(BF16)
        acc = jnp.exp2(m - m_new) * acc_ref[...]
        num = acc[0:KV_W] + _dot_t(vn, p)
        den = acc[KV_W:KV_W + 1] + jnp.sum(p.astype(F32), axis=0, keepdims=True)
        o = ocw_ref[...] + g[1:2, :] * (num * (1.0 / den))
        for gi in range(N_KV):
            o_ref[0, gi] = o[gi * HEAD_DIM:(gi + 1) * HEAD_DIM, gi * GQA * qcols:(gi + 1) * GQA * qcols]


def _nsa_sample(table, pages, qbd, gt, kc, vct, win, new, cbias, wbias, sfar, slast, snew,
                cover_t, rsum, past, ds):
    nb, npg = table.shape
    pgs = min(64, npg)
    assert npg % pgs == 0
    ncol = qbd.shape[2]
    ns = cover_t.shape[0]

    def page_spec(k):
        return pl.BlockSpec((1, KV_ROW, PAGE_SIZE), lambda b, j, tbl: (tbl[b, j * pgs + k], 0, 0))

    per_b = lambda a: pl.BlockSpec((1,) + a.shape[1:], lambda b, j, tbl: (b,) + (0,) * (a.ndim - 1))
    full = lambda a: pl.BlockSpec(a.shape, lambda b, j, tbl: (0,) * a.ndim)
    grid_spec = pltpu.PrefetchScalarGridSpec(
        num_scalar_prefetch=1,
        grid=(nb, npg // pgs),
        in_specs=[page_spec(k) for k in range(pgs)]
        + [per_b(a) for a in (qbd, gt, kc, vct, win, new)]
        + [full(a) for a in (cbias, wbias, sfar, slast, snew, cover_t, rsum)],
        out_specs=pl.BlockSpec((1, N_KV, HEAD_DIM, ncol // N_KV), lambda b, j, tbl: (b, 0, 0, 0)),
        scratch_shapes=[pltpu.VMEM((ns, ncol), F32), pltpu.VMEM((1, ncol), F32),
                        pltpu.VMEM((KV_W + 16, ncol), F32), pltpu.VMEM((KV_W, ncol), F32)],
    )
    return pl.pallas_call(
        functools.partial(_nsa_sample_kernel, pgs=pgs, npg=npg, past=past, ds=ds),
        grid_spec=grid_spec,
        out_shape=jax.ShapeDtypeStruct((nb, N_KV, HEAD_DIM, ncol // N_KV), F32),
        compiler_params=pltpu.CompilerParams(dimension_semantics=("parallel", "arbitrary"),
                                             vmem_limit_bytes=VMEM_LIMIT),
        name="nsa_sample",
    )(table, *([pages] * pgs), qbd, gt, kc, vct, win, new, cbias, wbias, sfar, slast, snew,
      cover_t, rsum)


def _cover_t(nc, ns, nc_pad, ns_pad):
    c0 = np.arange(nc) * CMP_STRIDE
    s0 = np.arange(ns) * SEL_BLOCK
    m = (c0[None, :] < s0[:, None] + SEL_BLOCK) & (c0[None, :] + CMP_BLOCK > s0[:, None])
    out = np.zeros((ns_pad, nc_pad), np.float32)
    out[:ns, :nc] = m
    return jnp.asarray(out, BF16)


def _prep_weights(norm_g, w_in, ln_v_g, spatial_w, spatial_b, cmp_pe, cmp_w1, cmp_b1, cmp_w2, cmp_b2,
                  w_out, final_g):
    offs = np.cumsum((D_A, D_A, D_A, D_B, 2 * KV_W, 2 * KV_W, 2 * KV_W, 3 * N_HEADS, D_B))
    g0, g1 = int(offs[6]), int(offs[7])
    w_perm = jnp.concatenate(
        [w_in[:, :g0], w_in[:, g1:], w_in[:, g0:g1],
         jnp.zeros((D_MODEL, GATE_PAD - 3 * N_HEADS), w_in.dtype)], axis=1).astype(BF16)
    pavg = jnp.asarray(np.kron(np.eye(A_GROUPS), np.full((HEAD_DIM, HEAD_DIM), 1.0 / HEAD_DIM)), BF16)
    w1cat = jnp.concatenate([cmp_w1[:, s].reshape(2, CMP_FLAT, CMP_HIDDEN) for s in range(CMP_R)],
                            axis=2).astype(BF16)
    pe8 = jnp.concatenate([cmp_pe.reshape(2, CMP_R, CMP_FLAT),
                           jnp.zeros((2, 8 - CMP_R, CMP_FLAT), F32)], axis=1)
    return dict(
        norm_g=norm_g.reshape(1, D_MODEL), w_perm=w_perm, ln_g=ln_v_g.reshape(1, D_A), pavg=pavg,
        w1cat=w1cat, pe8=pe8, b1=cmp_b1.reshape(2, 1, CMP_HIDDEN), w2=cmp_w2.astype(BF16),
        b2=cmp_b2.reshape(2, 1, HEAD_DIM), wo=w_out.astype(BF16), fg=final_g.reshape(1, D_MODEL),
        spatial_w=spatial_w, spatial_b=spatial_b)


def _spatial_operands(spatial_w, spatial_b, n):
    reps = CHUNK // n
    w = jnp.tril(spatial_w[:, :n, :n])
    eye = jnp.eye(reps, dtype=w.dtype)
    wsp = jnp.einsum('ab,gts->gatbs', eye, w).reshape(A_GROUPS, CHUNK, CHUNK).astype(BF16)
    b = jnp.tile(spatial_b[:, :n].T, (reps, 1))
    bsp = jnp.repeat(b, HEAD_DIM, axis=1)
    return wsp, bsp


def _prompt_bias_buckets():
    ql = np.arange(Q_BLOCK)[None, :]
    cl = np.arange(16)[:, None]
    band = np.stack([ql - CMP_STRIDE * cl - (CMP_BLOCK - 1),
                     ql + 97 - CMP_STRIDE * cl])
    kl = np.arange(CHUNK)[:, None]
    seld = np.stack([np.full((CHUNK, Q_BLOCK), FAR_DIST), CHUNK + ql - kl, ql - kl])
    wl = np.arange(WINDOW + Q_BLOCK)[:, None]
    dw = np.stack([Q_BLOCK * v + ql - wl for v in range(WIN_VARIANTS)])
    dw = np.where(dw < WINDOW, dw, -1)
    far = np.full((8, Q_BLOCK), FAR_DIST)
    tile4 = lambda d: np.tile(_t5_bucket_np(d.reshape(-1, Q_BLOCK)), (1, GQA))
    return tile4(band), tile4(seld), tile4(dw), tile4(far)


def _sample_bias_buckets(past, ds, qpad, ncp, nwin_pad):
    ql = (np.arange(qpad) % ds)[None, :]
    c = np.arange(ncp)[:, None]
    dc = past + ql - (CMP_STRIDE * c + CMP_BLOCK - 1)
    wl = np.arange(nwin_pad)[:, None]
    dw = WINDOW + ql - wl
    dw = np.where((dw < WINDOW) & (wl < WINDOW + ds), dw, -1)
    kl = np.arange(PAGE_SIZE)[:, None]
    dlast = PAGE_SIZE + ql - kl
    nl = np.arange(16)[:, None]
    dnew = np.where(nl < ds, ql - nl, -1)
    far = np.full((8, qpad), FAR_DIST)
    tile4 = lambda d: np.tile(_t5_bucket_np(d), (1, GQA))
    return tile4(dc), tile4(dw), tile4(dlast), tile4(dnew), tile4(far)


def kernel(x_prompt, x_sample, cache_cmp_kv, cache_sel_kv, state_win_kv, page_table, norm_g, w_in,
           ln_v_g, spatial_w, spatial_b, cmp_pe, cmp_w1, cmp_b1, cmp_w2, cmp_b2, rel_table, w_out,
           final_g):
    depth = norm_g.shape[0]
    assert depth == 1
    bsz, seq = x_prompt.shape[:2]
    db, ds = x_sample.shape[:2]
    npg = page_table.shape[1]
    past = npg * PAGE_SIZE
    win_buf = state_win_kv.shape[2]
    assert win_buf == WINDOW and past >= WINDOW and seq % Q_BLOCK == 0 and seq >= WINDOW + Q_BLOCK
    assert CHUNK % ds == 0 and ds <= 8 and (db * ds) % CHUNK == 0
    nblk = seq // Q_BLOCK
    l = 0
    wts = _prep_weights(norm_g[l], w_in[l], ln_v_g[l], spatial_w[l], spatial_b[l], cmp_pe[l],
                        cmp_w1[l], cmp_b1[l], cmp_w2[l], cmp_b2[l], w_out[l], final_g)
    inproj = lambda x, nb: _inproj(x, nb, wts['norm_g'], wts['w_perm'], wts['ln_g'], wts['pavg'])
    compress = lambda tbl, pages: _compress(tbl, pages, wts['w1cat'], wts['pe8'], wts['b1'],
                                            wts['w2'], wts['b2'])
    kv_out = lambda a: a.reshape(a.shape[0], 2, N_KV, HEAD_DIM, a.shape[2]).transpose(0, 4, 1, 2, 3)[None]
    pages_t = lambda c: c.transpose(0, 2, 3, 4, 1).reshape(c.shape[0], KV_ROW, PAGE_SIZE)

    n_p = bsz * seq
    xp = x_prompt.reshape(n_p, D_MODEL)
    ua, _, vnb, zas, zbs, kvc, kvs, kvw, qt, gt, ksk, vst, kwk, vwt = inproj(xp, bsz)
    kc, vct = compress(None, kvc)
    ncp = seq // CMP_STRIDE
    cover_p = _cover_t(ncp - CMP_R + 1, seq // SEL_BLOCK, ncp, seq // SEL_BLOCK)
    band_b, seld_b, win_b, far_b = _prompt_bias_buckets()
    cols = GQA * Q_BLOCK
    cband = _bias_tiles(rel_table, band_b, Q_BLOCK).reshape(N_KV, 2, 16, cols)
    seld = _bias_tiles(rel_table, seld_b, Q_BLOCK, rel_far=True).reshape(N_KV, 3, CHUNK, cols)
    wbias = _bias_tiles(rel_table, win_b, Q_BLOCK).reshape(N_KV, WIN_VARIANTS, WINDOW + Q_BLOCK, cols)
    far = _bias_tiles(rel_table, far_b, Q_BLOCK)[:, 0:1]
    ob = _nsa_prompt(qt, gt, kc, vct, ksk, vst, kwk, vwt, cband, far, seld, wbias, cover_p, bsz, seq)
    wsp, bsp = _spatial_operands(wts['spatial_w'], wts['spatial_b'], CHUNK)
    y_prompt = _mixout(xp, ua, vnb, zas, ob, zbs, wsp, bsp, wts['wo'], wts['fg']).reshape(bsz, seq, D_MODEL)
    new_cmp_p = kv_out(kvc)
    new_sel_p = kv_out(kvs)
    new_win_p = kv_out(kvw[:, :, seq - win_buf:])

    n_s = db * ds
    xs = x_sample.reshape(n_s, D_MODEL)
    ua, vn, vnb, zas, zbs, kvc, kvs, kvw, qt, gt, _, _, _, _ = inproj(xs, 1)
    kvc, kvs, kvw = (a[0].T for a in (kvc, kvs, kvw))
    kc, vct = compress(page_table, pages_t(cache_cmp_kv[l]))
    ncs = past // CMP_STRIDE
    t_all = past + ds
    nss = -(-t_all // SEL_BLOCK)
    nss_pad = -(-nss // 8) * 8
    cover_s = _cover_t(t_all // CMP_STRIDE - CMP_R + 1, nss, ncs, nss_pad)
    qpad = 16
    ncol = N_KV * GQA * qpad
    nwin_pad = -(-(win_buf + ds) // 16) * 16
    dc_b, dw_b, dl_b, dn_b, far_b = _sample_bias_buckets(past, ds, qpad, ncs, nwin_pad)
    both = lambda t: jnp.concatenate([t[0], t[1]], axis=1)
    cbias_s = both(_bias_tiles(rel_table, dc_b, qpad))
    wbias_s = both(_bias_tiles(rel_table, dw_b, qpad))
    slast = both(_bias_tiles(rel_table, dl_b, qpad, rel_far=True))
    snew = both(_bias_tiles(rel_table, dn_b, qpad, rel_far=True))
    sfar = both(_bias_tiles(rel_table, far_b, qpad))[0:1]

    qg = qt.reshape(N_KV, GQA, HEAD_DIM, db, ds).transpose(3, 0, 2, 1, 4)
    qg = jnp.tile(qg, (1, 1, 1, 1, qpad // ds)).reshape(db, N_KV, HEAD_DIM, GQA * qpad)
    zq = jnp.zeros_like(qg[:, 0])
    qbd = jnp.concatenate([jnp.concatenate([qg[:, 0], zq], axis=2),
                           jnp.concatenate([zq, qg[:, 1]], axis=2)], axis=1)
    gts = gt[:3 * N_HEADS].reshape(N_KV, GQA, 3, db, ds).transpose(3, 2, 0, 1, 4)
    gts = jnp.tile(gts, (1, 1, 1, 1, qpad // ds)).reshape(db, 3, ncol)
    win = jnp.concatenate([state_win_kv[l].reshape(db, win_buf, KV_ROW),
                           kvw.reshape(db, ds, KV_ROW)], axis=1)
    win_pad = jnp.pad(win, ((0, 0), (0, nwin_pad - win_buf - ds), (0, 0)))
    new_pad = jnp.pad(kvs.reshape(db, ds, KV_ROW), ((0, 0), (0, 16 - ds), (0, 0)))
    rsum = jnp.asarray(np.kron(np.eye(N_KV), np.kron(np.ones((GQA, GQA)), np.eye(qpad))), BF16)
    o_s = _nsa_sample(page_table, pages_t(cache_sel_kv[l]), qbd, gts,
                      kc, vct, win_pad, new_pad, cbias_s, wbias_s, sfar, slast, snew,
                      cover_s, rsum, past, ds)
    ob = (o_s.reshape(db, N_KV, HEAD_DIM, GQA, qpad)[..., :ds].transpose(0, 4, 1, 3, 2)
          .reshape(n_s, D_B))
    wsp, bsp = _spatial_operands(wts['spatial_w'], wts['spatial_b'], ds)
    y_sample = _mixout(xs, ua, vnb, zas, ob, zbs, wsp, bsp, wts['wo'], wts['fg']).reshape(db, ds, D_MODEL)
    kv5 = lambda a: a.reshape(1, db, ds, 2, N_KV, HEAD_DIM)
    new_cmp_s = kv5(kvc)
    new_sel_s = kv5(kvs)
    new_win_s = win[:, ds:].reshape(1, db, win_buf, 2, N_KV, HEAD_DIM)
    new_chunk_v = vn.reshape(1, db, ds, D_A)
    return (y_prompt, y_sample, new_cmp_p, new_sel_p, new_win_p, new_cmp_s, new_sel_s, new_win_s,
            new_chunk_v)
```

```python
import functools
import math

import numpy as np
import jax
import jax.numpy as jnp
from jax import lax
from jax.experimental import pallas as pl
from jax.experimental.pallas import tpu as pltpu

F32 = jnp.float32
BF16 = jnp.bfloat16

D_MODEL = 1024
HEAD_DIM = 64
D_A = 512
D_B = 512
A_GROUPS = D_A // HEAD_DIM
CHUNK = 128
N_HEADS = D_B // HEAD_DIM
N_KV = 2
GQA = N_HEADS // N_KV
KV_W = N_KV * HEAD_DIM
KV_ROW = 2 * KV_W
CMP_STRIDE = 16
CMP_BLOCK = 32
CMP_R = CMP_BLOCK // CMP_STRIDE
CMP_HIDDEN = 256
CMP_FLAT = CMP_STRIDE * HEAD_DIM
SEL_BLOCK = 64
N_SEL = 16
WINDOW = 512
N_BUCKETS = 32
MAX_DISTANCE = 128
Q_BLOCK = 128
PAGE_SIZE = 128
RMS_EPS = 1e-6
LN_EPS = 1e-5
NEG = -1e30
FORCE_BONUS = 1e6
LOG2E = 1.4426950408889634
Q_SCALE = HEAD_DIM ** -0.5 * LOG2E
FAR_DIST = 1 << 20
WIN_VARIANTS = WINDOW // Q_BLOCK + 1

_OFF_U, _OFF_V, _OFF_ZA, _OFF_Q = 0, 512, 1024, 1536
_OFF_KVC, _OFF_KVS, _OFF_KVW, _OFF_ZB, _OFF_G = 2048, 2304, 2560, 2816, 3328
D_IN_PAD = 3456
GATE_PAD = 128
GATE_ROWS = 32

VMEM_LIMIT = 52 * 1024 * 1024


def _gelu(x):
    return x * (0.5 * (1.0 + jnp.tanh(0.7978845608028654 * (x + 0.044715 * (x * x * x)))))


def _sigmoid(x):
    return 1.0 / (1.0 + jnp.exp(-x))


def _dot(a, b):
    return jnp.dot(a, b, preferred_element_type=F32)


def _dot_t(a, b):
    return lax.dot_general(a, b, (((0,), (0,)), ((), ())), preferred_element_type=F32)


def _dot_nt(a, b):
    return lax.dot_general(a, b, (((1,), (1,)), ((), ())), preferred_element_type=F32)


def _split_dot(a, b):
    hi = a.astype(BF16)
    lo = (a - hi.astype(F32)).astype(BF16)
    return _dot(hi, b) + _dot(lo, b)


def _split_dot_l(a, b):
    hi = b.astype(BF16)
    lo = (b - hi.astype(F32)).astype(BF16)
    return _dot(a, hi) + _dot(a, lo)


def _inproj_kernel(x_ref, ng_ref, w_ref, lng_ref, pavg_ref,
                   ua_ref, vn_ref, vnb_ref, zas_ref, zbs_ref, kvc_ref, kvs_ref, kvw_ref,
                   qt_ref, gt_ref, ksk_ref, vst_ref, kwk_ref, vwt_ref):
    x = x_ref[...]
    ms = jnp.mean(x * x, axis=-1, keepdims=True)
    h = (x * lax.rsqrt(ms + RMS_EPS) * ng_ref[...]).astype(BF16)

    def proj(a, b):
        return _dot(h, w_ref[:, a:b])

    ua_ref[...] = _gelu(proj(_OFF_U, _OFF_V)).astype(BF16)
    v = _gelu(proj(_OFF_V, _OFF_ZA))
    mu = _split_dot(v, pavg_ref[...])
    d = v - mu
    var = _dot((d * d).astype(BF16), pavg_ref[...])
    vn = d * lax.rsqrt(var + LN_EPS) * lng_ref[...]
    vn_ref[...] = vn
    vnb_ref[...] = vn.astype(BF16)
    za = proj(_OFF_ZA, _OFF_Q)
    zas_ref[...] = (za * _sigmoid(za)).astype(BF16)
    zb = proj(_OFF_ZB, _OFF_G)
    zbs_ref[...] = (zb * _sigmoid(zb)).astype(BF16)
    qt_ref[...] = (proj(_OFF_Q, _OFF_KVC) * Q_SCALE).T.astype(BF16)
    gt_ref[...] = _sigmoid(proj(_OFF_G, D_IN_PAD)).T[0:GATE_ROWS, :]
    kvc_ref[0] = proj(_OFF_KVC, _OFF_KVS).T
    kvs = proj(_OFF_KVS, _OFF_KVW)
    kvs_t = kvs.T
    kvs_ref[0] = kvs_t
    ksk_ref[...] = kvs[:, 0:KV_W].astype(BF16)
    vst_ref[...] = kvs_t[KV_W:KV_ROW, :].astype(BF16)
    kvw = proj(_OFF_KVW, _OFF_ZB)
    kvw_t = kvw.T
    kvw_ref[0] = kvw_t
    kwk_ref[...] = kvw[:, 0:KV_W].astype(BF16)
    vwt_ref[...] = kvw_t[KV_W:KV_ROW, :].astype(BF16)


def _inproj(x, nb, norm_g, w_perm, ln_g, pavg):
    n = x.shape[0]
    s = n // nb
    tm = min(256, s)
    assert s % tm == 0
    per = s // tm
    row = lambda w: pl.BlockSpec((tm, w), lambda i: (i, 0))
    col = lambda h: pl.BlockSpec((h, tm), lambda i: (0, i))
    kvt = pl.BlockSpec((1, KV_ROW, tm), lambda i: (i // per, 0, i % per))
    full = lambda a: pl.BlockSpec(a.shape, lambda i: (0,) * a.ndim)
    rows = [(D_A, BF16), (D_A, F32), (D_A, BF16), (D_A, BF16), (D_B, BF16)]
    sds = lambda shape, dt: jax.ShapeDtypeStruct(shape, dt)
    out_specs = ([row(w) for w, _ in rows] + [kvt, kvt, kvt]
                 + [col(D_B), col(GATE_ROWS), row(KV_W), col(KV_W), row(KV_W), col(KV_W)])
    out_shape = ([sds((n, w), dt) for w, dt in rows] + [sds((nb, KV_ROW, s), F32)] * 3
                 + [sds((D_B, n), BF16), sds((GATE_ROWS, n), F32), sds((n, KV_W), BF16),
                    sds((KV_W, n), BF16), sds((n, KV_W), BF16), sds((KV_W, n), BF16)])
    return pl.pallas_call(
        _inproj_kernel,
        grid=(n // tm,),
        in_specs=[row(D_MODEL), full(norm_g), full(w_perm), full(ln_g), full(pavg)],
        out_specs=out_specs,
        out_shape=out_shape,
        compiler_params=pltpu.CompilerParams(dimension_semantics=("parallel",),
                                             vmem_limit_bytes=VMEM_LIMIT),
        name="inproj",
    )(x, norm_g, w_perm, ln_g, pavg)


def _mixout_kernel(x_ref, ua_ref, vn_ref, zas_ref, ob_ref, zbs_ref, wsp_ref, bsp_ref, wo_ref, fg_ref,
                   y_ref, s_ref):
    tm = x_ref.shape[0]
    for c in range(tm // CHUNK):
        rows = slice(c * CHUNK, (c + 1) * CHUNK)
        vc = vn_ref[rows, :]
        for g in range(A_GROUPS):
            cols = slice(g * HEAD_DIM, (g + 1) * HEAD_DIM)
            s_ref[rows, cols] = _dot(wsp_ref[g], vc[:, cols])
        s_ref[rows, :] = s_ref[rows, :] + bsp_ref[...]
    mix_a = (ua_ref[...].astype(F32) * s_ref[...] * zas_ref[...].astype(F32)).astype(BF16)
    mix_b = (ob_ref[...].astype(F32) * zbs_ref[...].astype(F32)).astype(BF16)
    y = x_ref[...] + _dot(mix_a, wo_ref[0:D_A, :]) + _dot(mix_b, wo_ref[D_A:D_A + D_B, :])
    ms = jnp.mean(y * y, axis=-1, keepdims=True)
    y_ref[...] = y * lax.rsqrt(ms + RMS_EPS) * fg_ref[...]


def _mixout(x, ua, vn, zas, ob, zbs, wsp, bsp, wo, fg):
    n = x.shape[0]
    tm = min(256, n)
    row = lambda w: pl.BlockSpec((tm, w), lambda i: (i, 0))
    full = lambda a: pl.BlockSpec(a.shape, lambda i: (0,) * a.ndim)
    return pl.pallas_call(
        _mixout_kernel,
        grid=(n // tm,),
        in_specs=[row(D_MODEL), row(D_A), row(D_A), row(D_A), row(D_B), row(D_B),
                  full(wsp), full(bsp), full(wo), full(fg)],
        out_specs=row(D_MODEL),
        out_shape=jax.ShapeDtypeStruct((n, D_MODEL), F32),
        scratch_shapes=[pltpu.VMEM((tm, D_A), F32)],
        compiler_params=pltpu.CompilerParams(dimension_semantics=("parallel",),
                                             vmem_limit_bytes=VMEM_LIMIT),
        name="mixout",
    )(x, ua, vn, zas, ob, zbs, wsp, bsp, wo, fg)


def _t5_bucket_np(dist):
    dist = np.asarray(dist, np.int64)
    n = np.maximum(dist, 0)
    max_exact = N_BUCKETS // 2
    nf = np.maximum(n, max_exact).astype(np.float64)
    large = max_exact + (np.log(nf / max_exact) / math.log(MAX_DISTANCE / max_exact)
                         * (N_BUCKETS - max_exact)).astype(np.int64)
    b = np.where(n < max_exact, n, np.minimum(large, N_BUCKETS - 1))
    return np.where(dist < 0, -1, b).astype(np.int32)


def _bias_kernel(tbl_ref, bkt_ref, out_ref, *, qb, rel_far):
    b = bkt_ref[...]
    grp = lax.broadcasted_iota(jnp.int32, (1, b.shape[1]), 1) // qb

    def head_row(k, g):
        row = jnp.zeros(grp.shape, F32)
        for r in range(GQA):
            row = jnp.where(grp == r, tbl_ref[k * N_HEADS + g * GQA + r], row)
        return row * LOG2E

    for g in range(N_KV):
        acc = jnp.full(b.shape, NEG, F32)
        base = head_row(N_BUCKETS - 1, g) if rel_far else None
        for k in range(N_BUCKETS):
            row = head_row(k, g)
            if rel_far:
                row = row - base
            acc = jnp.where(b == k, row, acc)
        out_ref[g] = acc


def _bias_tiles(rel_table, bkt, qb, rel_far=False):
    r, c = bkt.shape
    rb = r
    for cand in (512, 256, 128, 64, 32, 16, 8):
        if r % cand == 0:
            rb = cand
            break
    return pl.pallas_call(
        functools.partial(_bias_kernel, qb=qb, rel_far=rel_far),
        grid=(r // rb,),
        in_specs=[pl.BlockSpec(memory_space=pltpu.SMEM), pl.BlockSpec((rb, c), lambda i: (i, 0))],
        out_specs=pl.BlockSpec((N_KV, rb, c), lambda i: (0, i, 0)),
        out_shape=jax.ShapeDtypeStruct((N_KV, r, c), F32),
        compiler_params=pltpu.CompilerParams(dimension_semantics=("parallel",)),
        name="bias_tiles",
    )(rel_table.reshape(-1), jnp.asarray(bkt))


def _compress_kernel(tbl_ref, *refs, pg):
    del tbl_ref
    page_refs = refs[:pg + 1]
    perm_ref, w1_ref, pe_ref, b1_ref, w2_ref, b2_ref, kc_ref, vct_ref, x_ref = refs[pg + 1:]
    mp = (pg + 1) * 8
    nrow = pg * 8
    left = lax.broadcasted_iota(jnp.int32, (8, 128), 1) < HEAD_DIM
    perm = perm_ref[...]
    for k, pr in enumerate(page_refs):
        for kv in range(2):
            half = pr[0, kv * KV_W:(kv + 1) * KV_W, :].astype(BF16)
            tok = _dot_nt(perm, half)
            for t in range(CMP_STRIDE // 2):
                e = tok[16 * t:16 * t + 8, :]
                o = tok[16 * t + 8:16 * t + 16, :]
                sw = pltpu.roll(jnp.where(left, o, e), HEAD_DIM, 1)
                x_ref[kv, 8 * k:8 * k + 8, 128 * t:128 * t + 128] = jnp.where(left, e, sw)
                x_ref[kv, mp + 8 * k:mp + 8 * k + 8, 128 * t:128 * t + 128] = jnp.where(left, sw, o)
    outs = []
    for kv in range(2):
        x_ref[kv, 2 * mp:2 * mp + 8, :] = pe_ref[kv]
        p = _dot(x_ref[kv].astype(BF16), w1_ref[kv])
        hc = (b1_ref[kv] + p[2 * mp:2 * mp + 1, 0:CMP_HIDDEN]
              + p[2 * mp + 1:2 * mp + 2, CMP_HIDDEN:2 * CMP_HIDDEN])
        per_g = []
        for g in range(N_KV):
            base = g * mp
            h = (p[base:base + nrow, 0:CMP_HIDDEN]
                 + p[base + 1:base + nrow + 1, CMP_HIDDEN:2 * CMP_HIDDEN] + hc)
            per_g.append(_dot(_gelu(h).astype(BF16), w2_ref[kv]) + b2_ref[kv])
        outs.append(jnp.concatenate(per_g, axis=1))
    kc_ref[0] = outs[0].astype(BF16)
    vct_ref[0] = outs[1].T.astype(BF16)


def _compress(table, pages, w1cat, pe8, b1, w2, b2):
    if table is None:
        nb, npg = pages.shape[0], pages.shape[2] // PAGE_SIZE
        table = jnp.zeros((1, 1), jnp.int32)
        index = lambda b, idx, tbl: (b, 0, idx)
    else:
        nb, npg = table.shape
        index = lambda b, idx, tbl: (tbl[b, idx], 0, 0)
    pg = min(64, npg)
    assert npg % pg == 0
    mp = (pg + 1) * 8

    def page_spec(k):
        return pl.BlockSpec(
            (1, KV_ROW, PAGE_SIZE),
            lambda b, j, tbl: index(b, jnp.minimum(j * pg + k, npg - 1), tbl))

    tok = np.arange(PAGE_SIZE)
    perm_np = np.zeros((PAGE_SIZE, PAGE_SIZE), np.float32)
    perm_np[(tok % CMP_STRIDE) * (PAGE_SIZE // CMP_STRIDE) + tok // CMP_STRIDE, tok] = 1.0
    perm = jnp.asarray(perm_np, BF16)
    full = lambda a: pl.BlockSpec(a.shape, lambda b, j, tbl: (0,) * a.ndim)
    grid_spec = pltpu.PrefetchScalarGridSpec(
        num_scalar_prefetch=1,
        grid=(nb, npg // pg),
        in_specs=[page_spec(k) for k in range(pg + 1)]
        + [full(a) for a in (perm, w1cat, pe8, b1, w2, b2)],
        out_specs=[pl.BlockSpec((1, pg * 8, KV_W), lambda b, j, tbl: (b, j, 0)),
                   pl.BlockSpec((1, KV_W, pg * 8), lambda b, j, tbl: (b, 0, j))],
        scratch_shapes=[pltpu.VMEM((2, 2 * mp + 8, CMP_FLAT), F32)],
    )
    return pl.pallas_call(
        functools.partial(_compress_kernel, pg=pg),
        grid_spec=grid_spec,
        out_shape=[jax.ShapeDtypeStruct((nb, npg * 8, KV_W), BF16),
                   jax.ShapeDtypeStruct((nb, KV_W, npg * 8), BF16)],
        compiler_params=pltpu.CompilerParams(dimension_semantics=("parallel", "parallel"),
                                             vmem_limit_bytes=VMEM_LIMIT),
        name="compress",
    )(table, *([pages] * (pg + 1)), perm, w1cat, pe8, b1, w2, b2)


def _topk_rows(imp, n_sel):
    ns = imp.shape[0]
    blk = lax.broadcasted_iota(jnp.int32, imp.shape, 0).astype(F32)
    for _ in range(n_sel):
        mx = jnp.max(imp, axis=0, keepdims=True)
        idx = jnp.min(jnp.where(imp == mx, blk, float(ns)), axis=0, keepdims=True)
        imp = jnp.where(blk == idx, -jnp.inf, imp)
    return imp == -jnp.inf


def _select_mask(imp, qpos):
    blk = lax.broadcasted_iota(jnp.int32, imp.shape, 0)
    cur = qpos // SEL_BLOCK
    forced = (blk == 0) | (blk == cur) | (blk == cur - 1)
    valid = blk * SEL_BLOCK <= qpos
    imp = imp + jnp.where(forced, FORCE_BONUS, 0.0)
    imp = jnp.where(valid, imp, NEG)
    sel = _topk_rows(imp, N_SEL)
    return jnp.where(sel & valid, 0.0, NEG)


def _softmax_rows(s):
    m = jnp.max(s, axis=0, keepdims=True)
    e = jnp.exp2(s - m)
    inv = jnp.where(m > NEG / 2, 1.0 / jnp.sum(e, axis=0, keepdims=True), 0.0)
    return e * inv


def _online_update(s, v_dot, m, l, acc):
    m_new = jnp.maximum(m, jnp.max(s, axis=0, keepdims=True))
    alpha = jnp.exp2(m - m_new)
    p = jnp.exp2(s - m_new)
    l = alpha * l + jnp.sum(p, axis=0, keepdims=True)
    acc = alpha * acc + v_dot(p.astype(BF16))
    return m_new, l, acc


def _nsa_prompt_kernel(qt_ref, gt_ref, kc_ref, vct_ref, ks_ref, vst_ref, kw_ref, vwt_ref,
                       cband_ref, far_ref, seld_ref, wbias_ref, cover_ref,
                       o_ref, bias_ref, msk_ref, s_ref, *, nsub):
    i = pl.program_id(1)
    cols = GQA * Q_BLOCK
    ncp = kc_ref.shape[1]
    gw = GQA * HEAD_DIM
    qpos = i * Q_BLOCK + lax.broadcasted_iota(jnp.int32, (1, Q_BLOCK), 1)
    n_trips = i // nsub + 1
    tk = nsub * CHUNK
    c0 = pl.multiple_of(jnp.maximum(i * 8 - 8, 0), 8)
    crow = lax.broadcasted_iota(jnp.int32, (ncp, 1), 0)
    woff = pl.multiple_of(jnp.maximum(i * Q_BLOCK - WINDOW, 0), Q_BLOCK)
    nwk = WINDOW + Q_BLOCK

    def query(g):
        qblk = qt_ref[g * gw:(g + 1) * gw, :]
        q64 = jnp.concatenate([qblk[r * HEAD_DIM:(r + 1) * HEAD_DIM, :] for r in range(GQA)], axis=1)
        zero = jnp.zeros_like(q64)
        return jnp.concatenate([q64, zero] if g == 0 else [zero, q64], axis=0)

    def dense_branches(g, q):
        far = far_ref[g]
        bias_ref[g] = jnp.where(crow < c0, far, NEG)
        bias_ref[g, pl.ds(c0, 16), :] = cband_ref[g, jnp.minimum(i, 1)]
        p_c = _softmax_rows(_dot(kc_ref[0], q) + bias_ref[g])
        o_c = _dot(vct_ref[0, g * HEAD_DIM:(g + 1) * HEAD_DIM, :], p_c.astype(BF16))
        psum = (p_c[:, 0:Q_BLOCK] + p_c[:, Q_BLOCK:2 * Q_BLOCK]
                + p_c[:, 2 * Q_BLOCK:3 * Q_BLOCK] + p_c[:, 3 * Q_BLOCK:4 * Q_BLOCK])
        imp = _split_dot_l(cover_ref[...], psum)
        p_w = _softmax_rows(_dot(kw_ref[pl.ds(woff, nwk), :], q) + wbias_ref[g, 0])
        o_w = _dot(vwt_ref[g * HEAD_DIM:(g + 1) * HEAD_DIM, pl.ds(woff, nwk)], p_w.astype(BF16))
        mask = _select_mask(imp, qpos)
        msk_ref[g] = jnp.concatenate([mask] * GQA, axis=1) + far
        return o_c, o_w

    def scores(g, q, t, slot):
        t = jnp.minimum(t, n_trips - 1)
        for u in range(nsub):
            kb = t * nsub + u
            off = pl.multiple_of(kb * CHUNK, CHUNK)
            for hb in range(2):
                rows = pl.ds(off + hb * SEL_BLOCK, SEL_BLOCK)
                s = _dot(ks_ref[rows, :], q) + msk_ref[g, pl.ds(2 * kb + hb, 1), :]
                s_ref[g, slot, (2 * u + hb) * SEL_BLOCK:(2 * u + hb + 1) * SEL_BLOCK, :] = s

    def attend(g, t, slot, near, carry):
        s = s_ref[g, slot]
        if near:
            s = s + jnp.concatenate(
                [seld_ref[g, jnp.clip(t * nsub + u - i + 2, 0, 2)] for u in range(nsub)], axis=0)
        koff = pl.multiple_of(t * tk, tk)
        vt = jnp.concatenate([vst_ref[g * HEAD_DIM:(g + 1) * HEAD_DIM, pl.ds(koff, tk)],
                              jnp.ones((16, tk), BF16)], axis=0)
        m, acc = carry
        m_new = jnp.maximum(m, jnp.max(s, axis=0, keepdims=True))
        p = jnp.exp2(s - m_new).astype(BF16)
        return m_new, jnp.exp2(m - m_new) * acc + _dot(vt, p)

    def selected(g, q):
        def make_pair(near):
            def pair(tt, carry):
                scores(g, q, 2 * tt + 1, 1)
                carry = attend(g, 2 * tt, 0, near, carry)
                scores(g, q, 2 * tt + 2, 0)
                return attend(g, 2 * tt + 1, 1, near, carry)
            return pair

        n_far_pairs = (jnp.maximum(i - 1, 0) // nsub) // 2
        init = (jnp.full((1, cols), -jnp.inf, F32), jnp.zeros((HEAD_DIM + 16, cols), F32))
        carry = lax.fori_loop(0, n_far_pairs, make_pair(False), init)
        carry = lax.fori_loop(n_far_pairs, n_trips // 2, make_pair(True), carry)
        _, acc_s = lax.cond(n_trips % 2 == 1,
                            lambda c: attend(g, n_trips - 1, 0, True, c), lambda c: c, carry)
        return acc_s[0:HEAD_DIM] * (1.0 / acc_s[HEAD_DIM:HEAD_DIM + 1])

    gt = gt_ref[...]

    def gate(g, j):
        return jnp.concatenate([gt[(g * GQA + r) * 3 + j:(g * GQA + r) * 3 + j + 1, :]
                                for r in range(GQA)], axis=1)

    qs = [query(g) for g in range(N_KV)]
    dense = [dense_branches(g, qs[g]) for g in range(N_KV)]
    for g in range(N_KV):
        scores(g, qs[g], 0, 0)
    for g in range(N_KV):
        o_c, o_w = dense[g]
        o = gate(g, 0) * o_c + gate(g, 1) * selected(g, qs[g]) + gate(g, 2) * o_w
        o_ref[:, g * gw:(g + 1) * gw] = jnp.concatenate(
            [o[:, r * Q_BLOCK:(r + 1) * Q_BLOCK].T for r in range(GQA)], axis=1).astype(BF16)


def _nsa_prompt(qt, gt, kc, vct, ks, vst, kw, vwt, cband, far, seld, wbias, cover_t, bsz, seq):
    nblk = seq // Q_BLOCK
    ncp = kc.shape[1]
    ns = cover_t.shape[0]
    cols = GQA * Q_BLOCK
    nsub = next(c for c in (4, 2, 1) if nblk % c == 0)
    full = lambda a: pl.BlockSpec(a.shape, lambda b, i: (0,) * a.ndim)
    in_specs = [
        pl.BlockSpec((D_B, Q_BLOCK), lambda b, i: (0, b * nblk + i)),
        pl.BlockSpec((GATE_ROWS, Q_BLOCK), lambda b, i: (0, b * nblk + i)),
        pl.BlockSpec((1, ncp, KV_W), lambda b, i: (b, 0, 0)),
        pl.BlockSpec((1, KV_W, ncp), lambda b, i: (b, 0, 0)),
        pl.BlockSpec((seq, KV_W), lambda b, i: (b, 0)),
        pl.BlockSpec((KV_W, seq), lambda b, i: (0, b)),
        pl.BlockSpec((seq, KV_W), lambda b, i: (b, 0)),
        pl.BlockSpec((KV_W, seq), lambda b, i: (0, b)),
        full(cband), full(far), full(seld),
        pl.BlockSpec((N_KV, 1) + wbias.shape[2:],
                     lambda b, i: (0, jnp.minimum(i, WIN_VARIANTS - 1), 0, 0)),
        full(cover_t),
    ]
    return pl.pallas_call(
        functools.partial(_nsa_prompt_kernel, nsub=nsub),
        grid=(bsz, nblk),
        in_specs=in_specs,
        out_specs=pl.BlockSpec((Q_BLOCK, D_B), lambda b, i: (b * nblk + i, 0)),
        out_shape=jax.ShapeDtypeStruct((bsz * seq, D_B), BF16),
        scratch_shapes=[pltpu.VMEM((N_KV, ncp, cols), F32), pltpu.VMEM((N_KV, ns, cols), F32),
                        pltpu.VMEM((N_KV, 2, nsub * CHUNK, cols), F32)],
        compiler_params=pltpu.CompilerParams(
            dimension_semantics=("parallel", "arbitrary"),
            vmem_limit_bytes=VMEM_LIMIT),
        name="nsa_prompt",
    )(qt, gt, kc, vct, ks, vst, kw, vwt, cband, far, seld, wbias, cover_t)


def _nsa_sample_kernel(tbl_ref, *refs, pgs, npg, past, ds):
    del tbl_ref
    page_refs = refs[:pgs]
    (qbd_ref, gt_ref, kc_ref, vct_ref, win_ref, new_ref, cbias_ref, wbias_ref, sfar_ref, slast_ref,
     snew_ref, cover_ref, rsum_ref, o_ref, msk_ref, m_ref, acc_ref, ocw_ref) = refs[pgs:]
    j = pl.program_id(1)
    qbd = qbd_ref[0]
    ncol = qbd.shape[1]
    qcols = ncol // (N_KV * GQA)
    g = gt_ref[0]

    @pl.when(j == 0)
    def _():
        p_c = _softmax_rows(_dot(kc_ref[0], qbd) + cbias_ref[...])
        o_c = _dot(vct_ref[0], p_c.astype(BF16))
        imp = _split_dot(_split_dot_l(cover_ref[...], p_c), rsum_ref[...])
        lane = lax.broadcasted_iota(jnp.int32, (1, ncol), 1)
        qpos = past + (lane % qcols) % ds
        msk_ref[...] = _select_mask(imp, qpos) + sfar_ref[...]
        kw = win_ref[0, :, 0:KV_W].astype(BF16)
        vw = win_ref[0, :, KV_W:KV_ROW].astype(BF16)
        p_w = _softmax_rows(_dot(kw, qbd) + wbias_ref[...])
        o_w = _dot_t(vw, p_w.astype(BF16))
        ocw_ref[...] = g[0:1, :] * o_c + g[2:3, :] * o_w
        m_ref[...] = jnp.full(m_ref.shape, -jnp.inf, F32)
        acc_ref[...] = jnp.zeros(acc_ref.shape, F32)

    last = j == pl.num_programs(1) - 1
    blocks, vts = [], []
    for k, pr in enumerate(page_refs):
        pidx = j * pgs + k
        kk = pr[0, 0:KV_W, :].T.astype(BF16)
        vts.append(pr[0, KV_W:KV_ROW, :].astype(BF16))
        s = _dot(kk, qbd)
        if k == pgs - 1:
            s = s + jnp.where(last, slast_ref[...], 0.0)
        for hb in range(2):
            blocks.append(s[hb * SEL_BLOCK:(hb + 1) * SEL_BLOCK, :]
                          + msk_ref[pl.ds(2 * pidx + hb, 1), :])
    top = blocks[0]
    for s in blocks[1:]:
        top = jnp.maximum(top, s)
    m = m_ref[...]
    m_new = jnp.maximum(m, jnp.max(top, axis=0, keepdims=True))
    p = jnp.concatenate([jnp.exp2(s - m_new).astype(BF16) for s in blocks], axis=0)
    vt = jnp.concatenate([jnp.concatenate(vts, axis=1), jnp.ones((16, pgs * PAGE_SIZE), BF16)], axis=0)
    acc_ref[...] = jnp.exp2(m - m_new) * acc_ref[...] + _dot(vt, p)
    m_ref[...] = m_new

    @pl.when(last)
    def _():
        kn = new_ref[0, :, 0:KV_W].astype(BF16)
        vn = new_ref[0, :, KV_W:KV_ROW].astype(BF16)
        s = _dot(kn, qbd) + snew_ref[...] + msk_ref[pl.ds(2 * npg, 1), :]
        m = m_ref[...]
        m_new = jnp.maximum(m, jnp.max(s, axis=0, keepdims=True))
        p = jnp.exp2(s - m_new).astype(BF16)
        acc = jnp.exp2(m - m_new) * acc_ref[...]
        num = acc[0:KV_W] + _dot_t(vn, p)
        den = acc[KV_W:KV_W + 1] + jnp.sum(p.astype(F32), axis=0, keepdims=True)
        o = ocw_ref[...] + g[1:2, :] * (num * (1.0 / den))
        for gi in range(N_KV):
            o_ref[0, gi] = o[gi * HEAD_DIM:(gi + 1) * HEAD_DIM, gi * GQA * qcols:(gi + 1) * GQA * qcols]


def _nsa_sample(table, pages, qbd, gt, kc, vct, win, new, cbias, wbias, sfar, slast, snew,
                cover_t, rsum, past, ds):
    nb, npg = table.shape
    pgs = min(64, npg)
    assert npg % pgs == 0
    ncol = qbd.shape[2]
    ns = cover_t.shape[0]

    def page_spec(k):
        return pl.BlockSpec((1, KV_ROW, PAGE_SIZE), lambda b, j, tbl: (tbl[b, j * pgs + k], 0, 0))

    per_b = lambda a: pl.BlockSpec((1,) + a.shape[1:], lambda b, j, tbl: (b,) + (0,) * (a.ndim - 1))
    full = lambda a: pl.BlockSpec(a.shape, lambda b, j, tbl: (0,) * a.ndim)
    grid_spec = pltpu.PrefetchScalarGridSpec(
        num_scalar_prefetch=1,
        grid=(nb, npg // pgs),
        in_specs=[page_spec(k) for k in range(pgs)]
        + [per_b(a) for a in (qbd, gt, kc, vct, win, new)]
        + [full(a) for a in (cbias, wbias, sfar, slast, snew, cover_t, rsum)],
        out_specs=pl.BlockSpec((1, N_KV, HEAD_DIM, ncol // N_KV), lambda b, j, tbl: (b, 0, 0, 0)),
        scratch_shapes=[pltpu.VMEM((ns, ncol), F32), pltpu.VMEM((1, ncol), F32),
                        pltpu.VMEM((KV_W + 16, ncol), F32), pltpu.VMEM((KV_W, ncol), F32)],
    )
    return pl.pallas_call(
        functools.partial(_nsa_sample_kernel, pgs=pgs, npg=npg, past=past, ds=ds),
        grid_spec=grid_spec,
        out_shape=jax.ShapeDtypeStruct((nb, N_KV, HEAD_DIM, ncol // N_KV), F32),
        compiler_params=pltpu.CompilerParams(dimension_semantics=("parallel", "arbitrary"),
                                             vmem_limit_bytes=VMEM_LIMIT),
        name="nsa_sample",
    )(table, *([pages] * pgs), qbd, gt, kc, vct, win, new, cbias, wbias, sfar, slast, snew,
      cover_t, rsum)


def _cover_t(nc, ns, nc_pad, ns_pad):
    c0 = np.arange(nc) * CMP_STRIDE
    s0 = np.arange(ns) * SEL_BLOCK
    m = (c0[None, :] < s0[:, None] + SEL_BLOCK) & (c0[None, :] + CMP_BLOCK > s0[:, None])
    out = np.zeros((ns_pad, nc_pad), np.float32)
    out[:ns, :nc] = m
    return jnp.asarray(out, BF16)


def _prep_weights(norm_g, w_in, ln_v_g, spatial_w, spatial_b, cmp_pe, cmp_w1, cmp_b1, cmp_w2, cmp_b2,
                  w_out, final_g):
    offs = np.cumsum((D_A, D_A, D_A, D_B, 2 * KV_W, 2 * KV_W, 2 * KV_W, 3 * N_HEADS, D_B))
    g0, g1 = int(offs[6]), int(offs[7])
    w_perm = jnp.concatenate(
        [w_in[:, :g0], w_in[:, g1:], w_in[:, g0:g1],
         jnp.zeros((D_MODEL, GATE_PAD - 3 * N_HEADS), w_in.dtype)], axis=1).astype(BF16)
    pavg = jnp.asarray(np.kron(np.eye(A_GROUPS), np.full((HEAD_DIM, HEAD_DIM), 1.0 / HEAD_DIM)), BF16)
    w1cat = jnp.concatenate([cmp_w1[:, s].reshape(2, CMP_FLAT, CMP_HIDDEN) for s in range(CMP_R)],
                            axis=2).astype(BF16)
    pe8 = jnp.concatenate([cmp_pe.reshape(2, CMP_R, CMP_FLAT),
                           jnp.zeros((2, 8 - CMP_R, CMP_FLAT), F32)], axis=1)
    return dict(
        norm_g=norm_g.reshape(1, D_MODEL), w_perm=w_perm, ln_g=ln_v_g.reshape(1, D_A), pavg=pavg,
        w1cat=w1cat, pe8=pe8, b1=cmp_b1.reshape(2, 1, CMP_HIDDEN), w2=cmp_w2.astype(BF16),
        b2=cmp_b2.reshape(2, 1, HEAD_DIM), wo=w_out.astype(BF16), fg=final_g.reshape(1, D_MODEL),
        spatial_w=spatial_w, spatial_b=spatial_b)


def _spatial_operands(spatial_w, spatial_b, n):
    reps = CHUNK // n
    w = jnp.tril(spatial_w[:, :n, :n])
    eye = jnp.eye(reps, dtype=w.dtype)
    wsp = jnp.einsum('ab,gts->gatbs', eye, w).reshape(A_GROUPS, CHUNK, CHUNK).astype(BF16)
    b = jnp.tile(spatial_b[:, :n].T, (reps, 1))
    bsp = jnp.repeat(b, HEAD_DIM, axis=1)
    return wsp, bsp


def _prompt_bias_buckets():
    ql = np.arange(Q_BLOCK)[None, :]
    cl = np.arange(16)[:, None]
    band = np.stack([ql - CMP_STRIDE * cl - (CMP_BLOCK - 1),
                     ql + 97 - CMP_STRIDE * cl])
    kl = np.arange(CHUNK)[:, None]
    seld = np.stack([np.full((CHUNK, Q_BLOCK), FAR_DIST), CHUNK + ql - kl, ql - kl])
    wl = np.arange(WINDOW + Q_BLOCK)[:, None]
    dw = np.stack([Q_BLOCK * v + ql - wl for v in range(WIN_VARIANTS)])
    dw = np.where(dw < WINDOW, dw, -1)
    far = np.full((8, Q_BLOCK), FAR_DIST)
    tile4 = lambda d: np.tile(_t5_bucket_np(d.reshape(-1, Q_BLOCK)), (1, GQA))
    return tile4(band), tile4(seld), tile4(dw), tile4(far)


def _sample_bias_buckets(past, ds, qpad, ncp, nwin_pad):
    ql = (np.arange(qpad) % ds)[None, :]
    c = np.arange(ncp)[:, None]
    dc = past + ql - (CMP_STRIDE * c + CMP_BLOCK - 1)
    wl = np.arange(nwin_pad)[:, None]
    dw = WINDOW + ql - wl
    dw = np.where((dw < WINDOW) & (wl < WINDOW + ds), dw, -1)
    kl = np.arange(PAGE_SIZE)[:, None]
    dlast = PAGE_SIZE + ql - kl
    nl = np.arange(16)[:, None]
    dnew = np.where(nl < ds, ql - nl, -1)
    far = np.full((8, qpad), FAR_DIST)
    tile4 = lambda d: np.tile(_t5_bucket_np(d), (1, GQA))
    return tile4(dc), tile4(dw), tile4(dlast), tile4(dnew), tile4(far)


def kernel(x_prompt, x_sample, cache_cmp_kv, cache_sel_kv, state_win_kv, page_table, norm_g, w_in,
           ln_v_g, spatial_w, spatial_b, cmp_pe, cmp_w1, cmp_b1, cmp_w2, cmp_b2, rel_table, w_out,
           final_g):
    depth = norm_g.shape[0]
    assert depth == 1
    bsz, seq = x_prompt.shape[:2]
    db, ds = x_sample.shape[:2]
    npg = page_table.shape[1]
    past = npg * PAGE_SIZE
    win_buf = state_win_kv.shape[2]
    assert win_buf == WINDOW and past >= WINDOW and seq % Q_BLOCK == 0 and seq >= WINDOW + Q_BLOCK
    assert CHUNK % ds == 0 and ds <= 8 and (db * ds) % CHUNK == 0
    nblk = seq // Q_BLOCK
    l = 0
    wts = _prep_weights(norm_g[l], w_in[l], ln_v_g[l], spatial_w[l], spatial_b[l], cmp_pe[l],
                        cmp_w1[l], cmp_b1[l], cmp_w2[l], cmp_b2[l], w_out[l], final_g)
    inproj = lambda x, nb: _inproj(x, nb, wts['norm_g'], wts['w_perm'], wts['ln_g'], wts['pavg'])
    compress = lambda tbl, pages: _compress(tbl, pages, wts['w1cat'], wts['pe8'], wts['b1'],
                                            wts['w2'], wts['b2'])
    kv_out = lambda a: a.reshape(a.shape[0], 2, N_KV, HEAD_DIM, a.shape[2]).transpose(0, 4, 1, 2, 3)[None]
    pages_t = lambda c: c.transpose(0, 2, 3, 4, 1).reshape(c.shape[0], KV_ROW, PAGE_SIZE)

    n_p = bsz * seq
    xp = x_prompt.reshape(n_p, D_MODEL)
    ua, _, vnb, zas, zbs, kvc, kvs, kvw, qt, gt, ksk, vst, kwk, vwt = inproj(xp, bsz)
    kc, vct = compress(None, kvc)
    ncp = seq // CMP_STRIDE
    cover_p = _cover_t(ncp - CMP_R + 1, seq // SEL_BLOCK, ncp, seq // SEL_BLOCK)
    band_b, seld_b, win_b, far_b = _prompt_bias_buckets()
    cols = GQA * Q_BLOCK
    cband = _bias_tiles(rel_table, band_b, Q_BLOCK).reshape(N_KV, 2, 16, cols)
    seld = _bias_tiles(rel_table, seld_b, Q_BLOCK, rel_far=True).reshape(N_KV, 3, CHUNK, cols)
    wbias = _bias_tiles(rel_table, win_b, Q_BLOCK).reshape(N_KV, WIN_VARIANTS, WINDOW + Q_BLOCK, cols)
    far = _bias_tiles(rel_table, far_b, Q_BLOCK)[:, 0:1]
    ob = _nsa_prompt(qt, gt, kc, vct, ksk, vst, kwk, vwt, cband, far, seld, wbias, cover_p, bsz, seq)
    wsp, bsp = _spatial_operands(wts['spatial_w'], wts['spatial_b'], CHUNK)
    y_prompt = _mixout(xp, ua, vnb, zas, ob, zbs, wsp, bsp, wts['wo'], wts['fg']).reshape(bsz, seq, D_MODEL)
    new_cmp_p = kv_out(kvc)
    new_sel_p = kv_out(kvs)
    new_win_p = kv_out(kvw[:, :, seq - win_buf:])

    n_s = db * ds
    xs = x_sample.reshape(n_s, D_MODEL)
    ua, vn, vnb, zas, zbs, kvc, kvs, kvw, qt, gt, _, _, _, _ = inproj(xs, 1)
    kvc, kvs, kvw = (a[0].T for a in (kvc, kvs, kvw))
    kc, vct = compress(page_table, pages_t(cache_cmp_kv[l]))
    ncs = past // CMP_STRIDE
    t_all = past + ds
    nss = -(-t_all // SEL_BLOCK)
    nss_pad = -(-nss // 8) * 8
    cover_s = _cover_t(t_all // CMP_STRIDE - CMP_R + 1, nss, ncs, nss_pad)
    qpad = 16
    ncol = N_KV * GQA * qpad
    nwin_pad = -(-(win_buf + ds) // 16) * 16
    dc_b, dw_b, dl_b, dn_b, far_b = _sample_bias_buckets(past, ds, qpad, ncs, nwin_pad)
    both = lambda t: jnp.concatenate([t[0], t[1]], axis=1)
    cbias_s = both(_bias_tiles(rel_table, dc_b, qpad))
    wbias_s = both(_bias_tiles(rel_table, dw_b, qpad))
    slast = both(_bias_tiles(rel_table, dl_b, qpad, rel_far=True))
    snew = both(_bias_tiles(rel_table, dn_b, qpad, rel_far=True))
    sfar = both(_bias_tiles(rel_table, far_b, qpad))[0:1]

    qg = qt.reshape(N_KV, GQA, HEAD_DIM, db, ds).transpose(3, 0, 2, 1, 4)
    qg = jnp.tile(qg, (1, 1, 1, 1, qpad // ds)).reshape(db, N_KV, HEAD_DIM, GQA * qpad)
    zq = jnp.zeros_like(qg[:, 0])
    qbd = jnp.concatenate([jnp.concatenate([qg[:, 0], zq], axis=2),
                           jnp.concatenate([zq, qg[:, 1]], axis=2)], axis=1)
    gts = gt[:3 * N_HEADS].reshape(N_KV, GQA, 3, db, ds).transpose(3, 2, 0, 1, 4)
    gts = jnp.tile(gts, (1, 1, 1, 1, qpad // ds)).reshape(db, 3, ncol)
    win = jnp.concatenate([state_win_kv[l].reshape(db, win_buf, KV_ROW),
                           kvw.reshape(db, ds, KV_ROW)], axis=1)
    win_pad = jnp.pad(win, ((0, 0), (0, nwin_pad - win_buf - ds), (0, 0)))
    new_pad = jnp.pad(kvs.reshape(db, ds, KV_ROW), ((0, 0), (0, 16 - ds), (0, 0)))
    rsum = jnp.asarray(np.kron(np.eye(N_KV), np.kron(np.ones((GQA, GQA)), np.eye(qpad))), BF16)
    o_s = _nsa_sample(page_table, pages_t(cache_sel_kv[l]), qbd, gts,
                      kc, vct, win_pad, new_pad, cbias_s, wbias_s, sfar, slast, snew,
                      cover_s, rsum, past, ds)
    ob = (o_s.reshape(db, N_KV, HEAD_DIM, GQA, qpad)[..., :ds].transpose(0, 4, 1, 3, 2)
          .reshape(n_s, D_B))
    wsp, bsp = _spatial_operands(wts['spatial_w'], wts['spatial_b'], ds)
    y_sample = _mixout(xs, ua, vnb, zas, ob, zbs, wsp, bsp, wts['wo'], wts['fg']).reshape(db, ds, D_MODEL)
    kv5 = lambda a: a.reshape(1, db, ds, 2, N_KV, HEAD_DIM)
    new_cmp_s = kv5(kvc)
    new_sel_s = kv5(kvs)
    new_win_s = win[:, ds:].reshape(1, db, win_buf, 2, N_KV, HEAD_DIM)
    new_chunk_v = vn.reshape(1, db, ds, D_A)
    return (y_prompt, y_sample, new_cmp_p, new_sel_p, new_win_p, new_cmp_s, new_sel_s, new_win_s,
            new_chunk_v)
```

```python
import functools
import math

import numpy as np
import jax
import jax.numpy as jnp
from jax import lax
from jax.experimental import pallas as pl
from jax.experimental.pallas import tpu as pltpu

F32 = jnp.float32
BF16 = jnp.bfloat16

D_MODEL = 1024
HEAD_DIM = 64
D_A = 512
D_B = 512
A_GROUPS = D_A // HEAD_DIM
CHUNK = 128
N_HEADS = D_B // HEAD_DIM
N_KV = 2
GQA = N_HEADS // N_KV
KV_W = N_KV * HEAD_DIM
KV_ROW = 2 * KV_W
CMP_STRIDE = 16
CMP_BLOCK = 32
CMP_R = CMP_BLOCK // CMP_STRIDE
CMP_HIDDEN = 256
CMP_FLAT = CMP_STRIDE * HEAD_DIM
SEL_BLOCK = 64
N_SEL = 16
WINDOW = 512
N_BUCKETS = 32
MAX_DISTANCE = 128
Q_BLOCK = 128
PAGE_SIZE = 128
RMS_EPS = 1e-6
LN_EPS = 1e-5
NEG = -1e30
FORCE_BONUS = 1e6
LOG2E = 1.4426950408889634
Q_SCALE = HEAD_DIM ** -0.5 * LOG2E
FAR_DIST = 1 << 20
WIN_VARIANTS = WINDOW // Q_BLOCK + 1

_OFF_U, _OFF_V, _OFF_ZA, _OFF_Q = 0, 512, 1024, 1536
_OFF_KVC, _OFF_KVS, _OFF_KVW, _OFF_ZB, _OFF_G = 2048, 2304, 2560, 2816, 3328
D_IN_PAD = 3456
GATE_PAD = 128
GATE_ROWS = 32

VMEM_LIMIT = 52 * 1024 * 1024


def _gelu(x):
    return x * (0.5 * (1.0 + jnp.tanh(0.7978845608028654 * (x + 0.044715 * (x * x * x)))))


def _sigmoid(x):
    return 1.0 / (1.0 + jnp.exp(-x))


def _dot(a, b):
    return jnp.dot(a, b, preferred_element_type=F32)


def _dot_t(a, b):
    return lax.dot_general(a, b, (((0,), (0,)), ((), ())), preferred_element_type=F32)


def _dot_nt(a, b):
    return lax.dot_general(a, b, (((1,), (1,)), ((), ())), preferred_element_type=F32)


def _split_dot(a, b):
    hi = a.astype(BF16)
    lo = (a - hi.astype(F32)).astype(BF16)
    return _dot(hi, b) + _dot(lo, b)


def _split_dot_l(a, b):
    hi = b.astype(BF16)
    lo = (b - hi.astype(F32)).astype(BF16)
    return _dot(a, hi) + _dot(a, lo)


def _inproj_kernel(x_ref, ng_ref, w_ref, lng_ref, pavg_ref,
                   ua_ref, vn_ref, vnb_ref, zas_ref, zbs_ref, kvc_ref, kvs_ref, kvw_ref,
                   qt_ref, gt_ref, ksk_ref, vst_ref, kwk_ref, vwt_ref):
    x = x_ref[...]
    ms = jnp.mean(x * x, axis=-1, keepdims=True)
    h = (x * lax.rsqrt(ms + RMS_EPS) * ng_ref[...]).astype(BF16)

    def proj(a, b):
        return _dot(h, w_ref[:, a:b])

    ua_ref[...] = _gelu(proj(_OFF_U, _OFF_V)).astype(BF16)
    v = _gelu(proj(_OFF_V, _OFF_ZA))
    mu = _split_dot(v, pavg_ref[...])
    d = v - mu
    var = _dot((d * d).astype(BF16), pavg_ref[...])
    vn = d * lax.rsqrt(var + LN_EPS) * lng_ref[...]
    vn_ref[...] = vn
    vnb_ref[...] = vn.astype(BF16)
    za = proj(_OFF_ZA, _OFF_Q)
    zas_ref[...] = (za * _sigmoid(za)).astype(BF16)
    zb = proj(_OFF_ZB, _OFF_G)
    zbs_ref[...] = (zb * _sigmoid(zb)).astype(BF16)
    qt_ref[...] = (proj(_OFF_Q, _OFF_KVC) * Q_SCALE).T.astype(BF16)
    gt_ref[...] = _sigmoid(proj(_OFF_G, D_IN_PAD)).T[0:GATE_ROWS, :]
    kvc_ref[0] = proj(_OFF_KVC, _OFF_KVS).T
    kvs = proj(_OFF_KVS, _OFF_KVW)
    kvs_t = kvs.T
    kvs_ref[0] = kvs_t
    ksk_ref[...] = kvs[:, 0:KV_W].astype(BF16)
    vst_ref[...] = kvs_t[KV_W:KV_ROW, :].astype(BF16)
    kvw = proj(_OFF_KVW, _OFF_ZB)
    kvw_t = kvw.T
    kvw_ref[0] = kvw_t
    kwk_ref[...] = kvw[:, 0:KV_W].astype(BF16)
    vwt_ref[...] = kvw_t[KV_W:KV_ROW, :].astype(BF16)


def _inproj(x, nb, norm_g, w_perm, ln_g, pavg):
    n = x.shape[0]
    s = n // nb
    tm = min(512, s)
    assert s % tm == 0
    per = s // tm
    row = lambda w: pl.BlockSpec((tm, w), lambda i: (i, 0))
    col = lambda h: pl.BlockSpec((h, tm), lambda i: (0, i))
    kvt = pl.BlockSpec((1, KV_ROW, tm), lambda i: (i // per, 0, i % per))
    full = lambda a: pl.BlockSpec(a.shape, lambda i: (0,) * a.ndim)
    rows = [(D_A, BF16), (D_A, F32), (D_A, BF16), (D_A, BF16), (D_B, BF16)]
    sds = lambda shape, dt: jax.ShapeDtypeStruct(shape, dt)
    out_specs = ([row(w) for w, _ in rows] + [kvt, kvt, kvt]
                 + [col(D_B), col(GATE_ROWS), row(KV_W), col(KV_W), row(KV_W), col(KV_W)])
    out_shape = ([sds((n, w), dt) for w, dt in rows] + [sds((nb, KV_ROW, s), F32)] * 3
                 + [sds((D_B, n), BF16), sds((GATE_ROWS, n), F32), sds((n, KV_W), BF16),
                    sds((KV_W, n), BF16), sds((n, KV_W), BF16), sds((KV_W, n), BF16)])
    return pl.pallas_call(
        _inproj_kernel,
        grid=(n // tm,),
        in_specs=[row(D_MODEL), full(norm_g), full(w_perm), full(ln_g), full(pavg)],
        out_specs=out_specs,
        out_shape=out_shape,
        compiler_params=pltpu.CompilerParams(dimension_semantics=("parallel",),
                                             vmem_limit_bytes=VMEM_LIMIT),
        name="inproj",
    )(x, norm_g, w_perm, ln_g, pavg)


def _mixout_kernel(x_ref, ua_ref, vn_ref, zas_ref, ob_ref, zbs_ref, wsp_ref, bsp_ref, wo_ref, fg_ref,
                   y_ref, s_ref):
    tm = x_ref.shape[0]
    for c in range(tm // CHUNK):
        rows = slice(c * CHUNK, (c + 1) * CHUNK)
        vc = vn_ref[rows, :]
        for g in range(A_GROUPS):
            cols = slice(g * HEAD_DIM, (g + 1) * HEAD_DIM)
            s_ref[rows, cols] = _dot(wsp_ref[g], vc[:, cols])
        s_ref[rows, :] = s_ref[rows, :] + bsp_ref[...]
    mix_a = (ua_ref[...].astype(F32) * s_ref[...] * zas_ref[...].astype(F32)).astype(BF16)
    mix_b = (ob_ref[...].astype(F32) * zbs_ref[...].astype(F32)).astype(BF16)
    y = x_ref[...] + _dot(mix_a, wo_ref[0:D_A, :]) + _dot(mix_b, wo_ref[D_A:D_A + D_B, :])
    ms = jnp.mean(y * y, axis=-1, keepdims=True)
    y_ref[...] = y * lax.rsqrt(ms + RMS_EPS) * fg_ref[...]


def _mixout(x, ua, vn, zas, ob, zbs, wsp, bsp, wo, fg):
    n = x.shape[0]
    tm = min(512, n)
    row = lambda w: pl.BlockSpec((tm, w), lambda i: (i, 0))
    full = lambda a: pl.BlockSpec(a.shape, lambda i: (0,) * a.ndim)
    return pl.pallas_call(
        _mixout_kernel,
        grid=(n // tm,),
        in_specs=[row(D_MODEL), row(D_A), row(D_A), row(D_A), row(D_B), row(D_B),
                  full(wsp), full(bsp), full(wo), full(fg)],
        out_specs=row(D_MODEL),
        out_shape=jax.ShapeDtypeStruct((n, D_MODEL), F32),
        scratch_shapes=[pltpu.VMEM((tm, D_A), F32)],
        compiler_params=pltpu.CompilerParams(dimension_semantics=("parallel",),
                                             vmem_limit_bytes=VMEM_LIMIT),
        name="mixout",
    )(x, ua, vn, zas, ob, zbs, wsp, bsp, wo, fg)


def _t5_bucket_np(dist):
    dist = np.asarray(dist, np.int64)
    n = np.maximum(dist, 0)
    max_exact = N_BUCKETS // 2
    nf = np.maximum(n, max_exact).astype(np.float64)
    large = max_exact + (np.log(nf / max_exact) / math.log(MAX_DISTANCE / max_exact)
                         * (N_BUCKETS - max_exact)).astype(np.int64)
    b = np.where(n < max_exact, n, np.minimum(large, N_BUCKETS - 1))
    return np.where(dist < 0, -1, b).astype(np.int32)


def _bias_kernel(tbl_ref, bkt_ref, out_ref, *, qb, rel_far):
    b = bkt_ref[...]
    grp = lax.broadcasted_iota(jnp.int32, (1, b.shape[1]), 1) // qb

    def head_row(k, g):
        row = jnp.zeros(grp.shape, F32)
        for r in range(GQA):
            row = jnp.where(grp == r, tbl_ref[k * N_HEADS + g * GQA + r], row)
        return row * LOG2E

    for g in range(N_KV):
        acc = jnp.full(b.shape, NEG, F32)
        base = head_row(N_BUCKETS - 1, g) if rel_far else None
        for k in range(N_BUCKETS):
            row = head_row(k, g)
            if rel_far:
                row = row - base
            acc = jnp.where(b == k, row, acc)
        out_ref[g] = acc


def _bias_tiles(rel_table, bkt, qb, rel_far=False):
    r, c = bkt.shape
    rb = r
    for cand in (512, 256, 128, 64, 32, 16, 8):
        if r % cand == 0:
            rb = cand
            break
    return pl.pallas_call(
        functools.partial(_bias_kernel, qb=qb, rel_far=rel_far),
        grid=(r // rb,),
        in_specs=[pl.BlockSpec(memory_space=pltpu.SMEM), pl.BlockSpec((rb, c), lambda i: (i, 0))],
        out_specs=pl.BlockSpec((N_KV, rb, c), lambda i: (0, i, 0)),
        out_shape=jax.ShapeDtypeStruct((N_KV, r, c), F32),
        compiler_params=pltpu.CompilerParams(dimension_semantics=("parallel",)),
        name="bias_tiles",
    )(rel_table.reshape(-1), jnp.asarray(bkt))


def _compress_kernel(tbl_ref, *refs, pg):
    del tbl_ref
    page_refs = refs[:pg + 1]
    perm_ref, w1_ref, pe_ref, b1_ref, w2_ref, b2_ref, kc_ref, vct_ref, x_ref = refs[pg + 1:]
    mp = (pg + 1) * 8
    nrow = pg * 8
    left = lax.broadcasted_iota(jnp.int32, (8, 128), 1) < HEAD_DIM
    perm = perm_ref[...]
    for k, pr in enumerate(page_refs):
        tok = _dot_nt(perm, pr[0].astype(BF16))
        for kv in range(2):
            for t in range(CMP_STRIDE // 2):
                e = tok[16 * t:16 * t + 8, kv * KV_W:(kv + 1) * KV_W]
                o = tok[16 * t + 8:16 * t + 16, kv * KV_W:(kv + 1) * KV_W]
                sw = pltpu.roll(jnp.where(left, o, e), HEAD_DIM, 1)
                x_ref[kv, 8 * k:8 * k + 8, 128 * t:128 * t + 128] = jnp.where(left, e, sw)
                x_ref[kv, mp + 8 * k:mp + 8 * k + 8, 128 * t:128 * t + 128] = jnp.where(left, sw, o)
    outs = []
    for kv in range(2):
        x_ref[kv, 2 * mp:2 * mp + 8, :] = pe_ref[kv]
        p = _dot(x_ref[kv].astype(BF16), w1_ref[kv])
        hc = (b1_ref[kv] + p[2 * mp:2 * mp + 1, 0:CMP_HIDDEN]
              + p[2 * mp + 1:2 * mp + 2, CMP_HIDDEN:2 * CMP_HIDDEN])
        per_g = []
        for g in range(N_KV):
            base = g * mp
            h = (p[base:base + nrow, 0:CMP_HIDDEN]
                 + p[base + 1:base + nrow + 1, CMP_HIDDEN:2 * CMP_HIDDEN] + hc)
            per_g.append(_dot(_gelu(h).astype(BF16), w2_ref[kv]) + b2_ref[kv])
        outs.append(jnp.concatenate(per_g, axis=1))
    kc_ref[0] = outs[0].astype(BF16)
    vct_ref[0] = outs[1].T.astype(BF16)


def _compress(table, pages, w1cat, pe8, b1, w2, b2):
    if table is None:
        nb, npg = pages.shape[0], pages.shape[2] // PAGE_SIZE
        table = jnp.zeros((1, 1), jnp.int32)
        index = lambda b, idx, tbl: (b, 0, idx)
    else:
        nb, npg = table.shape
        index = lambda b, idx, tbl: (tbl[b, idx], 0, 0)
    pg = min(64, npg)
    assert npg % pg == 0
    mp = (pg + 1) * 8

    def page_spec(k):
        return pl.BlockSpec(
            (1, KV_ROW, PAGE_SIZE),
            lambda b, j, tbl: index(b, jnp.minimum(j * pg + k, npg - 1), tbl))

    tok = np.arange(PAGE_SIZE)
    perm_np = np.zeros((PAGE_SIZE, PAGE_SIZE), np.float32)
    perm_np[(tok % CMP_STRIDE) * (PAGE_SIZE // CMP_STRIDE) + tok // CMP_STRIDE, tok] = 1.0
    perm = jnp.asarray(perm_np, BF16)
    full = lambda a: pl.BlockSpec(a.shape, lambda b, j, tbl: (0,) * a.ndim)
    grid_spec = pltpu.PrefetchScalarGridSpec(
        num_scalar_prefetch=1,
        grid=(nb, npg // pg),
        in_specs=[page_spec(k) for k in range(pg + 1)]
        + [full(a) for a in (perm, w1cat, pe8, b1, w2, b2)],
        out_specs=[pl.BlockSpec((1, pg * 8, KV_W), lambda b, j, tbl: (b, j, 0)),
                   pl.BlockSpec((1, KV_W, pg * 8), lambda b, j, tbl: (b, 0, j))],
        scratch_shapes=[pltpu.VMEM((2, 2 * mp + 8, CMP_FLAT), F32)],
    )
    return pl.pallas_call(
        functools.partial(_compress_kernel, pg=pg),
        grid_spec=grid_spec,
        out_shape=[jax.ShapeDtypeStruct((nb, npg * 8, KV_W), BF16),
                   jax.ShapeDtypeStruct((nb, KV_W, npg * 8), BF16)],
        compiler_params=pltpu.CompilerParams(dimension_semantics=("parallel", "parallel"),
                                             vmem_limit_bytes=VMEM_LIMIT),
        name="compress",
    )(table, *([pages] * (pg + 1)), perm, w1cat, pe8, b1, w2, b2)


def _topk_rows(imp, n_sel):
    ns = imp.shape[0]
    blk = lax.broadcasted_iota(jnp.int32, imp.shape, 0).astype(F32)
    for _ in range(n_sel):
        mx = jnp.max(imp, axis=0, keepdims=True)
        idx = jnp.min(jnp.where(imp == mx, blk, float(ns)), axis=0, keepdims=True)
        imp = jnp.where(blk == idx, -jnp.inf, imp)
    return imp == -jnp.inf


def _select_mask(imp, qpos):
    blk = lax.broadcasted_iota(jnp.int32, imp.shape, 0)
    cur = qpos // SEL_BLOCK
    forced = (blk == 0) | (blk == cur) | (blk == cur - 1)
    valid = blk * SEL_BLOCK <= qpos
    imp = imp + jnp.where(forced, FORCE_BONUS, 0.0)
    imp = jnp.where(valid, imp, NEG)
    sel = _topk_rows(imp, N_SEL)
    return jnp.where(sel & valid, 0.0, NEG)


def _softmax_rows(s):
    m = jnp.max(s, axis=0, keepdims=True)
    e = jnp.exp2(s - m)
    inv = jnp.where(m > NEG / 2, 1.0 / jnp.sum(e, axis=0, keepdims=True), 0.0)
    return e * inv


def _online_update(s, v_dot, m, l, acc):
    m_new = jnp.maximum(m, jnp.max(s, axis=0, keepdims=True))
    alpha = jnp.exp2(m - m_new)
    p = jnp.exp2(s - m_new)
    l = alpha * l + jnp.sum(p, axis=0, keepdims=True)
    acc = alpha * acc + v_dot(p.astype(BF16))
    return m_new, l, acc


def _nsa_prompt_kernel(qt_ref, gt_ref, kc_ref, vct_ref, ks_ref, vst_ref, kw_ref, vwt_ref,
                       cband_ref, far_ref, seld_ref, wbias_ref, cover_ref,
                       o_ref, bias_ref, msk_ref, s_ref, *, nsub):
    i = pl.program_id(1)
    cols = GQA * Q_BLOCK
    ncp = kc_ref.shape[1]
    gw = GQA * HEAD_DIM
    qpos = i * Q_BLOCK + lax.broadcasted_iota(jnp.int32, (1, Q_BLOCK), 1)
    n_trips = i // nsub + 1
    tk = nsub * CHUNK
    c0 = pl.multiple_of(jnp.maximum(i * 8 - 8, 0), 8)
    crow = lax.broadcasted_iota(jnp.int32, (ncp, 1), 0)
    woff = pl.multiple_of(jnp.maximum(i * Q_BLOCK - WINDOW, 0), Q_BLOCK)
    nwk = WINDOW + Q_BLOCK

    def query(g):
        qblk = qt_ref[g * gw:(g + 1) * gw, :]
        q64 = jnp.concatenate([qblk[r * HEAD_DIM:(r + 1) * HEAD_DIM, :] for r in range(GQA)], axis=1)
        zero = jnp.zeros_like(q64)
        return jnp.concatenate([q64, zero] if g == 0 else [zero, q64], axis=0)

    def dense_branches(g, q):
        far = far_ref[g]
        bias_ref[g] = jnp.where(crow < c0, far, NEG)
        bias_ref[g, pl.ds(c0, 16), :] = cband_ref[g, jnp.minimum(i, 1)]
        p_c = _softmax_rows(_dot(kc_ref[0], q) + bias_ref[g])
        o_c = _dot(vct_ref[0, g * HEAD_DIM:(g + 1) * HEAD_DIM, :], p_c.astype(BF16))
        psum = (p_c[:, 0:Q_BLOCK] + p_c[:, Q_BLOCK:2 * Q_BLOCK]
                + p_c[:, 2 * Q_BLOCK:3 * Q_BLOCK] + p_c[:, 3 * Q_BLOCK:4 * Q_BLOCK])
        imp = _split_dot_l(cover_ref[...], psum)
        s_w = _dot(kw_ref[pl.ds(woff, nwk), :], q) + wbias_ref[g, 0]
        e_w = jnp.exp2(s_w - jnp.max(s_w, axis=0, keepdims=True)).astype(BF16)
        vw = jnp.concatenate([vwt_ref[g * HEAD_DIM:(g + 1) * HEAD_DIM, pl.ds(woff, nwk)],
                              jnp.ones((16, nwk), BF16)], axis=0)
        a_w = _dot(vw, e_w)
        o_w = a_w[0:HEAD_DIM] * (1.0 / a_w[HEAD_DIM:HEAD_DIM + 1])
        mask = _select_mask(imp, qpos)
        msk_ref[g] = jnp.concatenate([mask] * GQA, axis=1) + far
        return o_c, o_w

    def scores(g, q, t, slot):
        t = jnp.minimum(t, n_trips - 1)
        for u in range(nsub):
            kb = t * nsub + u
            off = pl.multiple_of(kb * CHUNK, CHUNK)
            for hb in range(2):
                rows = pl.ds(off + hb * SEL_BLOCK, SEL_BLOCK)
                s = _dot(ks_ref[rows, :], q) + msk_ref[g, pl.ds(2 * kb + hb, 1), :]
                s_ref[g, slot, (2 * u + hb) * SEL_BLOCK:(2 * u + hb + 1) * SEL_BLOCK, :] = s

    def attend(g, t, slot, near, carry):
        s = s_ref[g, slot]
        if near:
            s = s + jnp.concatenate(
                [seld_ref[g, jnp.clip(t * nsub + u - i + 2, 0, 2)] for u in range(nsub)], axis=0)
        koff = pl.multiple_of(t * tk, tk)
        vt = jnp.concatenate([vst_ref[g * HEAD_DIM:(g + 1) * HEAD_DIM, pl.ds(koff, tk)],
                              jnp.ones((16, tk), BF16)], axis=0)
        m, acc = carry
        m_new = jnp.maximum(m, jnp.max(s, axis=0, keepdims=True))
        p = jnp.exp2(s - m_new).astype(BF16)
        return m_new, jnp.exp2(m - m_new) * acc + _dot(vt, p)

    def selected(g, q):
        def make_pair(near):
            def pair(tt, carry):
                scores(g, q, 2 * tt + 1, 1)
                carry = attend(g, 2 * tt, 0, near, carry)
                scores(g, q, 2 * tt + 2, 0)
                return attend(g, 2 * tt + 1, 1, near, carry)
            return pair

        n_far_pairs = (jnp.maximum(i - 1, 0) // nsub) // 2
        init = (jnp.full((1, cols), -jnp.inf, F32), jnp.zeros((HEAD_DIM + 16, cols), F32))
        carry = lax.fori_loop(0, n_far_pairs, make_pair(False), init)
        carry = lax.fori_loop(n_far_pairs, n_trips // 2, make_pair(True), carry)
        _, acc_s = lax.cond(n_trips % 2 == 1,
                            lambda c: attend(g, n_trips - 1, 0, True, c), lambda c: c, carry)
        return acc_s[0:HEAD_DIM] * (1.0 / acc_s[HEAD_DIM:HEAD_DIM + 1])

    gt = gt_ref[...]

    def gate(g, j):
        return jnp.concatenate([gt[(g * GQA + r) * 3 + j:(g * GQA + r) * 3 + j + 1, :]
                                for r in range(GQA)], axis=1)

    qs = [query(g) for g in range(N_KV)]
    dense = [dense_branches(g, qs[g]) for g in range(N_KV)]
    for g in range(N_KV):
        scores(g, qs[g], 0, 0)
    for g in range(N_KV):
        o_c, o_w = dense[g]
        o = gate(g, 0) * o_c + gate(g, 1) * selected(g, qs[g]) + gate(g, 2) * o_w
        o_ref[:, g * gw:(g + 1) * gw] = jnp.concatenate(
            [o[:, r * Q_BLOCK:(r + 1) * Q_BLOCK].T for r in range(GQA)], axis=1).astype(BF16)


def _nsa_prompt(qt, gt, kc, vct, ks, vst, kw, vwt, cband, far, seld, wbias, cover_t, bsz, seq):
    nblk = seq // Q_BLOCK
    ncp = kc.shape[1]
    ns = cover_t.shape[0]
    cols = GQA * Q_BLOCK
    nsub = next(c for c in (4, 2, 1) if nblk % c == 0)
    full = lambda a: pl.BlockSpec(a.shape, lambda b, i: (0,) * a.ndim)
    in_specs = [
        pl.BlockSpec((D_B, Q_BLOCK), lambda b, i: (0, b * nblk + i)),
        pl.BlockSpec((GATE_ROWS, Q_BLOCK), lambda b, i: (0, b * nblk + i)),
        pl.BlockSpec((1, ncp, KV_W), lambda b, i: (b, 0, 0)),
        pl.BlockSpec((1, KV_W, ncp), lambda b, i: (b, 0, 0)),
        pl.BlockSpec((seq, KV_W), lambda b, i: (b, 0)),
        pl.BlockSpec((KV_W, seq), lambda b, i: (0, b)),
        pl.BlockSpec((seq, KV_W), lambda b, i: (b, 0)),
        pl.BlockSpec((KV_W, seq), lambda b, i: (0, b)),
        full(cband), full(far), full(seld),
        pl.BlockSpec((N_KV, 1) + wbias.shape[2:],
                     lambda b, i: (0, jnp.minimum(i, WIN_VARIANTS - 1), 0, 0)),
        full(cover_t),
    ]
    return pl.pallas_call(
        functools.partial(_nsa_prompt_kernel, nsub=nsub),
        grid=(bsz, nblk),
        in_specs=in_specs,
        out_specs=pl.BlockSpec((Q_BLOCK, D_B), lambda b, i: (b * nblk + i, 0)),
        out_shape=jax.ShapeDtypeStruct((bsz * seq, D_B), BF16),
        scratch_shapes=[pltpu.VMEM((N_KV, ncp, cols), F32), pltpu.VMEM((N_KV, ns, cols), F32),
                        pltpu.VMEM((N_KV, 2, nsub * CHUNK, cols), F32)],
        compiler_params=pltpu.CompilerParams(
            dimension_semantics=("parallel", "arbitrary"),
            vmem_limit_bytes=VMEM_LIMIT),
        name="nsa_prompt",
    )(qt, gt, kc, vct, ks, vst, kw, vwt, cband, far, seld, wbias, cover_t)


def _nsa_sample_kernel(tbl_ref, *refs, pgs, npg, past, ds):
    del tbl_ref
    page_refs = refs[:pgs]
    (qbd_ref, gt_ref, kc_ref, vct_ref, win_ref, new_ref, cbias_ref, wbias_ref, sfar_ref, slast_ref,
     snew_ref, cover_ref, rsum_ref, o_ref, msk_ref, m_ref, acc_ref, ocw_ref) = refs[pgs:]
    j = pl.program_id(1)
    qbd = qbd_ref[0]
    ncol = qbd.shape[1]
    qcols = ncol // (N_KV * GQA)
    g = gt_ref[0]

    @pl.when(j == 0)
    def _():
        p_c = _softmax_rows(_dot(kc_ref[0], qbd) + cbias_ref[...])
        o_c = _dot(vct_ref[0], p_c.astype(BF16))
        imp = _split_dot(_split_dot_l(cover_ref[...], p_c), rsum_ref[...])
        lane = lax.broadcasted_iota(jnp.int32, (1, ncol), 1)
        qpos = past + (lane % qcols) % ds
        msk_ref[...] = _select_mask(imp, qpos) + sfar_ref[...]
        kw = win_ref[0, :, 0:KV_W].astype(BF16)
        vw = win_ref[0, :, KV_W:KV_ROW].astype(BF16)
        p_w = _softmax_rows(_dot(kw, qbd) + wbias_ref[...])
        o_w = _dot_t(vw, p_w.astype(BF16))
        ocw_ref[...] = g[0:1, :] * o_c + g[2:3, :] * o_w
        m_ref[...] = jnp.full(m_ref.shape, -jnp.inf, F32)
        acc_ref[...] = jnp.zeros(acc_ref.shape, F32)

    last = j == pl.num_programs(1) - 1
    blocks, vts = [], []
    for k, pr in enumerate(page_refs):
        pidx = j * pgs + k
        kk = pr[0, 0:KV_W, :].T.astype(BF16)
        vts.append(pr[0, KV_W:KV_ROW, :].astype(BF16))
        s = _dot(kk, qbd)
        if k == pgs - 1:
            s = s + jnp.where(last, slast_ref[...], 0.0)
        for hb in range(2):
            blocks.append(s[hb * SEL_BLOCK:(hb + 1) * SEL_BLOCK, :]
                          + msk_ref[pl.ds(2 * pidx + hb, 1), :])
    top = blocks[0]
    for s in blocks[1:]:
        top = jnp.maximum(top, s)
    m = m_ref[...]
    m_new = jnp.maximum(m, jnp.max(top, axis=0, keepdims=True))
    p = jnp.concatenate([jnp.exp2(s - m_new).astype(BF16) for s in blocks], axis=0)
    vt = jnp.concatenate([jnp.concatenate(vts, axis=1), jnp.ones((16, pgs * PAGE_SIZE), BF16)], axis=0)
    acc_ref[...] = jnp.exp2(m - m_new) * acc_ref[...] + _dot(vt, p)
    m_ref[...] = m_new

    @pl.when(last)
    def _():
        kn = new_ref[0, :, 0:KV_W].astype(BF16)
        vn = new_ref[0, :, KV_W:KV_ROW].astype(BF16)
        s = _dot(kn, qbd) + snew_ref[...] + msk_ref[pl.ds(2 * npg, 1), :]
        m = m_ref[...]
        m_new = jnp.maximum(m, jnp.max(s, axis=0, keepdims=True))
        p = jnp.exp2(s - m_new).astype(BF16)
        acc = jnp.exp2(m - m_new) * acc_ref[...]
        num = acc[0:KV_W] + _dot_t(vn, p)
        den = acc[KV_W:KV_W + 1] + jnp.sum(p.astype(F32), axis=0, keepdims=True)
        o = ocw_ref[...] + g[1:2, :] * (num * (1.0 / den))
        for gi in range(N_KV):
            o_ref[0, gi] = o[gi * HEAD_DIM:(gi + 1) * HEAD_DIM, gi * GQA * qcols:(gi + 1) * GQA * qcols]


def _nsa_sample(table, pages, qbd, gt, kc, vct, win, new, cbias, wbias, sfar, slast, snew,
                cover_t, rsum, past, ds):
    nb, npg = table.shape
    pgs = min(64, npg)
    assert npg % pgs == 0
    ncol = qbd.shape[2]
    ns = cover_t.shape[0]

    def page_spec(k):
        return pl.BlockSpec((1, KV_ROW, PAGE_SIZE), lambda b, j, tbl: (tbl[b, j * pgs + k], 0, 0))

    per_b = lambda a: pl.BlockSpec((1,) + a.shape[1:], lambda b, j, tbl: (b,) + (0,) * (a.ndim - 1))
    full = lambda a: pl.BlockSpec(a.shape, lambda b, j, tbl: (0,) * a.ndim)
    grid_spec = pltpu.PrefetchScalarGridSpec(
        num_scalar_prefetch=1,
        grid=(nb, npg // pgs),
        in_specs=[page_spec(k) for k in range(pgs)]
        + [per_b(a) for a in (qbd, gt, kc, vct, win, new)]
        + [full(a) for a in (cbias, wbias, sfar, slast, snew, cover_t, rsum)],
        out_specs=pl.BlockSpec((1, N_KV, HEAD_DIM, ncol // N_KV), lambda b, j, tbl: (b, 0, 0, 0)),
        scratch_shapes=[pltpu.VMEM((ns, ncol), F32), pltpu.VMEM((1, ncol), F32),
                        pltpu.VMEM((KV_W + 16, ncol), F32), pltpu.VMEM((KV_W, ncol), F32)],
    )
    return pl.pallas_call(
        functools.partial(_nsa_sample_kernel, pgs=pgs, npg=npg, past=past, ds=ds),
        grid_spec=grid_spec,
        out_shape=jax.ShapeDtypeStruct((nb, N_KV, HEAD_DIM, ncol // N_KV), F32),
        compiler_params=pltpu.CompilerParams(dimension_semantics=("parallel", "arbitrary"),
                                             vmem_limit_bytes=VMEM_LIMIT),
        name="nsa_sample",
    )(table, *([pages] * pgs), qbd, gt, kc, vct, win, new, cbias, wbias, sfar, slast, snew,
      cover_t, rsum)


def _cover_t(nc, ns, nc_pad, ns_pad):
    c0 = np.arange(nc) * CMP_STRIDE
    s0 = np.arange(ns) * SEL_BLOCK
    m = (c0[None, :] < s0[:, None] + SEL_BLOCK) & (c0[None, :] + CMP_BLOCK > s0[:, None])
    out = np.zeros((ns_pad, nc_pad), np.float32)
    out[:ns, :nc] = m
    return jnp.asarray(out, BF16)


def _prep_weights(norm_g, w_in, ln_v_g, spatial_w, spatial_b, cmp_pe, cmp_w1, cmp_b1, cmp_w2, cmp_b2,
                  w_out, final_g):
    offs = np.cumsum((D_A, D_A, D_A, D_B, 2 * KV_W, 2 * KV_W, 2 * KV_W, 3 * N_HEADS, D_B))
    g0, g1 = int(offs[6]), int(offs[7])
    w_perm = jnp.concatenate(
        [w_in[:, :g0], w_in[:, g1:], w_in[:, g0:g1],
         jnp.zeros((D_MODEL, GATE_PAD - 3 * N_HEADS), w_in.dtype)], axis=1).astype(BF16)
    pavg = jnp.asarray(np.kron(np.eye(A_GROUPS), np.full((HEAD_DIM, HEAD_DIM), 1.0 / HEAD_DIM)), BF16)
    w1cat = jnp.concatenate([cmp_w1[:, s].reshape(2, CMP_FLAT, CMP_HIDDEN) for s in range(CMP_R)],
                            axis=2).astype(BF16)
    pe8 = jnp.concatenate([cmp_pe.reshape(2, CMP_R, CMP_FLAT),
                           jnp.zeros((2, 8 - CMP_R, CMP_FLAT), F32)], axis=1)
    return dict(
        norm_g=norm_g.reshape(1, D_MODEL), w_perm=w_perm, ln_g=ln_v_g.reshape(1, D_A), pavg=pavg,
        w1cat=w1cat, pe8=pe8, b1=cmp_b1.reshape(2, 1, CMP_HIDDEN), w2=cmp_w2.astype(BF16),
        b2=cmp_b2.reshape(2, 1, HEAD_DIM), wo=w_out.astype(BF16), fg=final_g.reshape(1, D_MODEL),
        spatial_w=spatial_w, spatial_b=spatial_b)


def _spatial_operands(spatial_w, spatial_b, n):
    reps = CHUNK // n
    w = jnp.tril(spatial_w[:, :n, :n])
    eye = jnp.eye(reps, dtype=w.dtype)
    wsp = jnp.einsum('ab,gts->gatbs', eye, w).reshape(A_GROUPS, CHUNK, CHUNK).astype(BF16)
    b = jnp.tile(spatial_b[:, :n].T, (reps, 1))
    bsp = jnp.repeat(b, HEAD_DIM, axis=1)
    return wsp, bsp


def _prompt_bias_buckets():
    ql = np.arange(Q_BLOCK)[None, :]
    cl = np.arange(16)[:, None]
    band = np.stack([ql - CMP_STRIDE * cl - (CMP_BLOCK - 1),
                     ql + 97 - CMP_STRIDE * cl])
    kl = np.arange(CHUNK)[:, None]
    seld = np.stack([np.full((CHUNK, Q_BLOCK), FAR_DIST), CHUNK + ql - kl, ql - kl])
    wl = np.arange(WINDOW + Q_BLOCK)[:, None]
    dw = np.stack([Q_BLOCK * v + ql - wl for v in range(WIN_VARIANTS)])
    dw = np.where(dw < WINDOW, dw, -1)
    far = np.full((8, Q_BLOCK), FAR_DIST)
    tile4 = lambda d: np.tile(_t5_bucket_np(d.reshape(-1, Q_BLOCK)), (1, GQA))
    return tile4(band), tile4(seld), tile4(dw), tile4(far)


def _sample_bias_buckets(past, ds, qpad, ncp, nwin_pad):
    ql = (np.arange(qpad) % ds)[None, :]
    c = np.arange(ncp)[:, None]
    dc = past + ql - (CMP_STRIDE * c + CMP_BLOCK - 1)
    wl = np.arange(nwin_pad)[:, None]
    dw = WINDOW + ql - wl
    dw = np.where((dw < WINDOW) & (wl < WINDOW + ds), dw, -1)
    kl = np.arange(PAGE_SIZE)[:, None]
    dlast = PAGE_SIZE + ql - kl
    nl = np.arange(16)[:, None]
    dnew = np.where(nl < ds, ql - nl, -1)
    far = np.full((8, qpad), FAR_DIST)
    tile4 = lambda d: np.tile(_t5_bucket_np(d), (1, GQA))
    return tile4(dc), tile4(dw), tile4(dlast), tile4(dnew), tile4(far)


def kernel(x_prompt, x_sample, cache_cmp_kv, cache_sel_kv, state_win_kv, page_table, norm_g, w_in,
           ln_v_g, spatial_w, spatial_b, cmp_pe, cmp_w1, cmp_b1, cmp_w2, cmp_b2, rel_table, w_out,
           final_g):
    depth = norm_g.shape[0]
    assert depth == 1
    bsz, seq = x_prompt.shape[:2]
    db, ds = x_sample.shape[:2]
    npg = page_table.shape[1]
    past = npg * PAGE_SIZE
    win_buf = state_win_kv.shape[2]
    assert win_buf == WINDOW and past >= WINDOW and seq % Q_BLOCK == 0 and seq >= WINDOW + Q_BLOCK
    assert CHUNK % ds == 0 and ds <= 8 and (db * ds) % CHUNK == 0
    nblk = seq // Q_BLOCK
    l = 0
    wts = _prep_weights(norm_g[l], w_in[l], ln_v_g[l], spatial_w[l], spatial_b[l], cmp_pe[l],
                        cmp_w1[l], cmp_b1[l], cmp_w2[l], cmp_b2[l], w_out[l], final_g)
    inproj = lambda x, nb: _inproj(x, nb, wts['norm_g'], wts['w_perm'], wts['ln_g'], wts['pavg'])
    compress = lambda tbl, pages: _compress(tbl, pages, wts['w1cat'], wts['pe8'], wts['b1'],
                                            wts['w2'], wts['b2'])
    kv_out = lambda a: a.reshape(a.shape[0], 2, N_KV, HEAD_DIM, a.shape[2]).transpose(0, 4, 1, 2, 3)[None]
    pages_t = lambda c: c.transpose(0, 2, 3, 4, 1).reshape(c.shape[0], KV_ROW, PAGE_SIZE)

    n_p = bsz * seq
    xp = x_prompt.reshape(n_p, D_MODEL)
    ua, _, vnb, zas, zbs, kvc, kvs, kvw, qt, gt, ksk, vst, kwk, vwt = inproj(xp, bsz)
    kc, vct = compress(None, kvc)
    ncp = seq // CMP_STRIDE
    cover_p = _cover_t(ncp - CMP_R + 1, seq // SEL_BLOCK, ncp, seq // SEL_BLOCK)
    band_b, seld_b, win_b, far_b = _prompt_bias_buckets()
    cols = GQA * Q_BLOCK
    cband = _bias_tiles(rel_table, band_b, Q_BLOCK).reshape(N_KV, 2, 16, cols)
    seld = _bias_tiles(rel_table, seld_b, Q_BLOCK, rel_far=True).reshape(N_KV, 3, CHUNK, cols)
    wbias = _bias_tiles(rel_table, win_b, Q_BLOCK).reshape(N_KV, WIN_VARIANTS, WINDOW + Q_BLOCK, cols)
    far = _bias_tiles(rel_table, far_b, Q_BLOCK)[:, 0:1]
    ob = _nsa_prompt(qt, gt, kc, vct, ksk, vst, kwk, vwt, cband, far, seld, wbias, cover_p, bsz, seq)
    wsp, bsp = _spatial_operands(wts['spatial_w'], wts['spatial_b'], CHUNK)
    y_prompt = _mixout(xp, ua, vnb, zas, ob, zbs, wsp, bsp, wts['wo'], wts['fg']).reshape(bsz, seq, D_MODEL)
    new_cmp_p = kv_out(kvc)
    new_sel_p = kv_out(kvs)
    new_win_p = kv_out(kvw[:, :, seq - win_buf:])

    n_s = db * ds
    xs = x_sample.reshape(n_s, D_MODEL)
    ua, vn, vnb, zas, zbs, kvc, kvs, kvw, qt, gt, _, _, _, _ = inproj(xs, 1)
    kvc, kvs, kvw = (a[0].T for a in (kvc, kvs, kvw))
    kc, vct = compress(page_table, pages_t(cache_cmp_kv[l]))
    ncs = past // CMP_STRIDE
    t_all = past + ds
    nss = -(-t_all // SEL_BLOCK)
    nss_pad = -(-nss // 8) * 8
    cover_s = _cover_t(t_all // CMP_STRIDE - CMP_R + 1, nss, ncs, nss_pad)
    qpad = 16
    ncol = N_KV * GQA * qpad
    nwin_pad = -(-(win_buf + ds) // 16) * 16
    dc_b, dw_b, dl_b, dn_b, far_b = _sample_bias_buckets(past, ds, qpad, ncs, nwin_pad)
    both = lambda t: jnp.concatenate([t[0], t[1]], axis=1)
    cbias_s = both(_bias_tiles(rel_table, dc_b, qpad))
    wbias_s = both(_bias_tiles(rel_table, dw_b, qpad))
    slast = both(_bias_tiles(rel_table, dl_b, qpad, rel_far=True))
    snew = both(_bias_tiles(rel_table, dn_b, qpad, rel_far=True))
    sfar = both(_bias_tiles(rel_table, far_b, qpad))[0:1]

    qg = qt.reshape(N_KV, GQA, HEAD_DIM, db, ds).transpose(3, 0, 2, 1, 4)
    qg = jnp.tile(qg, (1, 1, 1, 1, qpad // ds)).reshape(db, N_KV, HEAD_DIM, GQA * qpad)
    zq = jnp.zeros_like(qg[:, 0])
    qbd = jnp.concatenate([jnp.concatenate([qg[:, 0], zq], axis=2),
                           jnp.concatenate([zq, qg[:, 1]], axis=2)], axis=1)
    gts = gt[:3 * N_HEADS].reshape(N_KV, GQA, 3, db, ds).transpose(3, 2, 0, 1, 4)
    gts = jnp.tile(gts, (1, 1, 1, 1, qpad // ds)).reshape(db, 3, ncol)
    win = jnp.concatenate([state_win_kv[l].reshape(db, win_buf, KV_ROW),
                           kvw.reshape(db, ds, KV_ROW)], axis=1)
    win_pad = jnp.pad(win, ((0, 0), (0, nwin_pad - win_buf - ds), (0, 0)))
    new_pad = jnp.pad(kvs.reshape(db, ds, KV_ROW), ((0, 0), (0, 16 - ds), (0, 0)))
    rsum = jnp.asarray(np.kron(np.eye(N_KV), np.kron(np.ones((GQA, GQA)), np.eye(qpad))), BF16)
    o_s = _nsa_sample(page_table, pages_t(cache_sel_kv[l]), qbd, gts,
                      kc, vct, win_pad, new_pad, cbias_s, wbias_s, sfar, slast, snew,
                      cover_s, rsum, past, ds)
    ob = (o_s.reshape(db, N_KV, HEAD_DIM, GQA, qpad)[..., :ds].transpose(0, 4, 1, 3, 2)
          .reshape(n_s, D_B))
    wsp, bsp = _spatial_operands(wts['spatial_w'], wts['spatial_b'], ds)
    y_sample = _mixout(xs, ua, vnb, zas, ob, zbs, wsp, bsp, wts['wo'], wts['fg']).reshape(db, ds, D_MODEL)
    kv5 = lambda a: a.reshape(1, db, ds, 2, N_KV, HEAD_DIM)
    new_cmp_s = kv5(kvc)
    new_sel_s = kv5(kvs)
    new_win_s = win[:, ds:].reshape(1, db, win_buf, 2, N_KV, HEAD_DIM)
    new_chunk_v = vn.reshape(1, db, ds, D_A)
    return (y_prompt, y_sample, new_cmp_p, new_sel_p, new_win_p, new_cmp_s, new_sel_s, new_win_s,
            new_chunk_v)
```

```python
import functools
import math

import numpy as np
import jax
import jax.numpy as jnp
from jax import lax
from jax.experimental import pallas as pl
from jax.experimental.pallas import tpu as pltpu

F32 = jnp.float32
BF16 = jnp.bfloat16

D_MODEL = 1024
HEAD_DIM = 64
D_A = 512
D_B = 512
A_GROUPS = D_A // HEAD_DIM
CHUNK = 128
N_HEADS = D_B // HEAD_DIM
N_KV = 2
GQA = N_HEADS // N_KV
KV_W = N_KV * HEAD_DIM
KV_ROW = 2 * KV_W
CMP_STRIDE = 16
CMP_BLOCK = 32
CMP_R = CMP_BLOCK // CMP_STRIDE
CMP_HIDDEN = 256
CMP_FLAT = CMP_STRIDE * HEAD_DIM
SEL_BLOCK = 64
N_SEL = 16
WINDOW = 512
N_BUCKETS = 32
MAX_DISTANCE = 128
Q_BLOCK = 128
PAGE_SIZE = 128
RMS_EPS = 1e-6
LN_EPS = 1e-5
NEG = -1e30
FORCE_BONUS = 1e6
LOG2E = 1.4426950408889634
Q_SCALE = HEAD_DIM ** -0.5 * LOG2E
FAR_DIST = 1 << 20
WIN_VARIANTS = WINDOW // Q_BLOCK + 1

_OFF_U, _OFF_V, _OFF_ZA, _OFF_Q = 0, 512, 1024, 1536
_OFF_KVC, _OFF_KVS, _OFF_KVW, _OFF_ZB, _OFF_G = 2048, 2304, 2560, 2816, 3328
D_IN_PAD = 3456
GATE_PAD = 128
GATE_ROWS = 32

VMEM_LIMIT = 52 * 1024 * 1024


def _gelu(x):
    return x * (0.5 * (1.0 + jnp.tanh(0.7978845608028654 * (x + 0.044715 * (x * x * x)))))


def _sigmoid(x):
    return 1.0 / (1.0 + jnp.exp(-x))


def _dot(a, b):
    return jnp.dot(a, b, preferred_element_type=F32)


def _dot_t(a, b):
    return lax.dot_general(a, b, (((0,), (0,)), ((), ())), preferred_element_type=F32)


def _dot_nt(a, b):
    return lax.dot_general(a, b, (((1,), (1,)), ((), ())), preferred_element_type=F32)


def _split_dot(a, b):
    hi = a.astype(BF16)
    lo = (a - hi.astype(F32)).astype(BF16)
    return _dot(hi, b) + _dot(lo, b)


def _split_dot_l(a, b):
    hi = b.astype(BF16)
    lo = (b - hi.astype(F32)).astype(BF16)
    return _dot(a, hi) + _dot(a, lo)


def _inproj_kernel(x_ref, ng_ref, w_ref, lng_ref, pavg_ref,
                   ua_ref, vn_ref, vnb_ref, zas_ref, zbs_ref, kvc_ref, kvs_ref, kvw_ref,
                   qt_ref, gt_ref, ksk_ref, vst_ref, kwk_ref, vwt_ref):
    x = x_ref[...]
    ms = jnp.mean(x * x, axis=-1, keepdims=True)
    h = (x * lax.rsqrt(ms + RMS_EPS) * ng_ref[...]).astype(BF16)

    def proj(a, b):
        return _dot(h, w_ref[:, a:b])

    ua_ref[...] = _gelu(proj(_OFF_U, _OFF_V)).astype(BF16)
    v = _gelu(proj(_OFF_V, _OFF_ZA))
    mu = _split_dot(v, pavg_ref[...])
    d = v - mu
    var = _dot((d * d).astype(BF16), pavg_ref[...])
    vn = d * lax.rsqrt(var + LN_EPS) * lng_ref[...]
    vn_ref[...] = vn
    vnb_ref[...] = vn.astype(BF16)
    za = proj(_OFF_ZA, _OFF_Q)
    zas_ref[...] = (za * _sigmoid(za)).astype(BF16)
    zb = proj(_OFF_ZB, _OFF_G)
    zbs_ref[...] = (zb * _sigmoid(zb)).astype(BF16)
    qt_ref[...] = (proj(_OFF_Q, _OFF_KVC) * Q_SCALE).T.astype(BF16)
    gt_ref[...] = _sigmoid(proj(_OFF_G, D_IN_PAD)).T[0:GATE_ROWS, :]
    kvc_ref[0] = proj(_OFF_KVC, _OFF_KVS).T
    kvs = proj(_OFF_KVS, _OFF_KVW)
    kvs_t = kvs.T
    kvs_ref[0] = kvs_t
    ksk_ref[...] = kvs[:, 0:KV_W].astype(BF16)
    vst_ref[...] = kvs_t[KV_W:KV_ROW, :].astype(BF16)
    kvw = proj(_OFF_KVW, _OFF_ZB)
    kvw_t = kvw.T
    kvw_ref[0] = kvw_t
    kwk_ref[...] = kvw[:, 0:KV_W].astype(BF16)
    vwt_ref[...] = kvw_t[KV_W:KV_ROW, :].astype(BF16)


def _inproj(x, nb, norm_g, w_perm, ln_g, pavg):
    n = x.shape[0]
    s = n // nb
    tm = min(512, s)
    assert s % tm == 0
    per = s // tm
    row = lambda w: pl.BlockSpec((tm, w), lambda i: (i, 0))
    col = lambda h: pl.BlockSpec((h, tm), lambda i: (0, i))
    kvt = pl.BlockSpec((1, KV_ROW, tm), lambda i: (i // per, 0, i % per))
    full = lambda a: pl.BlockSpec(a.shape, lambda i: (0,) * a.ndim)
    rows = [(D_A, BF16), (D_A, F32), (D_A, BF16), (D_A, BF16), (D_B, BF16)]
    sds = lambda shape, dt: jax.ShapeDtypeStruct(shape, dt)
    out_specs = ([row(w) for w, _ in rows] + [kvt, kvt, kvt]
                 + [col(D_B), col(GATE_ROWS), row(KV_W), col(KV_W), row(KV_W), col(KV_W)])
    out_shape = ([sds((n, w), dt) for w, dt in rows] + [sds((nb, KV_ROW, s), F32)] * 3
                 + [sds((D_B, n), BF16), sds((GATE_ROWS, n), F32), sds((n, KV_W), BF16),
                    sds((KV_W, n), BF16), sds((n, KV_W), BF16), sds((KV_W, n), BF16)])
    return pl.pallas_call(
        _inproj_kernel,
        grid=(n // tm,),
        in_specs=[row(D_MODEL), full(norm_g), full(w_perm), full(ln_g), full(pavg)],
        out_specs=out_specs,
        out_shape=out_shape,
        compiler_params=pltpu.CompilerParams(dimension_semantics=("parallel",),
                                             vmem_limit_bytes=VMEM_LIMIT),
        name="inproj",
    )(x, norm_g, w_perm, ln_g, pavg)


def _mixout_kernel(x_ref, ua_ref, vn_ref, zas_ref, ob_ref, zbs_ref, wsp_ref, bsp_ref, wo_ref, fg_ref,
                   y_ref, s_ref):
    tm = x_ref.shape[0]
    for c in range(tm // CHUNK):
        rows = slice(c * CHUNK, (c + 1) * CHUNK)
        vc = vn_ref[rows, :]
        for g in range(A_GROUPS):
            cols = slice(g * HEAD_DIM, (g + 1) * HEAD_DIM)
            s_ref[rows, cols] = _dot(wsp_ref[g], vc[:, cols])
        s_ref[rows, :] = s_ref[rows, :] + bsp_ref[...]
    mix_a = (ua_ref[...].astype(F32) * s_ref[...] * zas_ref[...].astype(F32)).astype(BF16)
    mix_b = (ob_ref[...].astype(F32) * zbs_ref[...].astype(F32)).astype(BF16)
    y = x_ref[...] + _dot(mix_a, wo_ref[0:D_A, :]) + _dot(mix_b, wo_ref[D_A:D_A + D_B, :])
    ms = jnp.mean(y * y, axis=-1, keepdims=True)
    y_ref[...] = y * lax.rsqrt(ms + RMS_EPS) * fg_ref[...]


def _mixout(x, ua, vn, zas, ob, zbs, wsp, bsp, wo, fg):
    n = x.shape[0]
    tm = min(512, n)
    row = lambda w: pl.BlockSpec((tm, w), lambda i: (i, 0))
    full = lambda a: pl.BlockSpec(a.shape, lambda i: (0,) * a.ndim)
    return pl.pallas_call(
        _mixout_kernel,
        grid=(n // tm,),
        in_specs=[row(D_MODEL), row(D_A), row(D_A), row(D_A), row(D_B), row(D_B),
                  full(wsp), full(bsp), full(wo), full(fg)],
        out_specs=row(D_MODEL),
        out_shape=jax.ShapeDtypeStruct((n, D_MODEL), F32),
        scratch_shapes=[pltpu.VMEM((tm, D_A), F32)],
        compiler_params=pltpu.CompilerParams(dimension_semantics=("parallel",),
                                             vmem_limit_bytes=VMEM_LIMIT),
        name="mixout",
    )(x, ua, vn, zas, ob, zbs, wsp, bsp, wo, fg)


def _t5_bucket_np(dist):
    dist = np.asarray(dist, np.int64)
    n = np.maximum(dist, 0)
    max_exact = N_BUCKETS // 2
    nf = np.maximum(n, max_exact).astype(np.float64)
    large = max_exact + (np.log(nf / max_exact) / math.log(MAX_DISTANCE / max_exact)
                         * (N_BUCKETS - max_exact)).astype(np.int64)
    b = np.where(n < max_exact, n, np.minimum(large, N_BUCKETS - 1))
    return np.where(dist < 0, -1, b).astype(np.int32)


def _bias_kernel(tbl_ref, bkt_ref, out_ref, *, qb, rel_far):
    b = bkt_ref[...]
    grp = lax.broadcasted_iota(jnp.int32, (1, b.shape[1]), 1) // qb

    def head_row(k, g):
        row = jnp.zeros(grp.shape, F32)
        for r in range(GQA):
            row = jnp.where(grp == r, tbl_ref[k * N_HEADS + g * GQA + r], row)
        return row * LOG2E

    for g in range(N_KV):
        acc = jnp.full(b.shape, NEG, F32)
        base = head_row(N_BUCKETS - 1, g) if rel_far else None
        for k in range(N_BUCKETS):
            row = head_row(k, g)
            if rel_far:
                row = row - base
            acc = jnp.where(b == k, row, acc)
        out_ref[g] = acc


def _bias_tiles(rel_table, bkt, qb, rel_far=False):
    r, c = bkt.shape
    rb = r
    for cand in (512, 256, 128, 64, 32, 16, 8):
        if r % cand == 0:
            rb = cand
            break
    return pl.pallas_call(
        functools.partial(_bias_kernel, qb=qb, rel_far=rel_far),
        grid=(r // rb,),
        in_specs=[pl.BlockSpec(memory_space=pltpu.SMEM), pl.BlockSpec((rb, c), lambda i: (i, 0))],
        out_specs=pl.BlockSpec((N_KV, rb, c), lambda i: (0, i, 0)),
        out_shape=jax.ShapeDtypeStruct((N_KV, r, c), F32),
        compiler_params=pltpu.CompilerParams(dimension_semantics=("parallel",)),
        name="bias_tiles",
    )(rel_table.reshape(-1), jnp.asarray(bkt))


def _compress_kernel(tbl_ref, *refs, pg):
    del tbl_ref
    page_refs = refs[:pg + 1]
    perm_ref, w1_ref, pe_ref, b1_ref, w2_ref, b2_ref, kc_ref, vct_ref, x_ref = refs[pg + 1:]
    mp = (pg + 1) * 8
    nrow = pg * 8
    left = lax.broadcasted_iota(jnp.int32, (8, 128), 1) < HEAD_DIM
    perm = perm_ref[...]
    for k, pr in enumerate(page_refs):
        tok = _dot_nt(perm, pr[0].astype(BF16))
        for kv in range(2):
            for t in range(CMP_STRIDE // 2):
                e = tok[16 * t:16 * t + 8, kv * KV_W:(kv + 1) * KV_W]
                o = tok[16 * t + 8:16 * t + 16, kv * KV_W:(kv + 1) * KV_W]
                sw = pltpu.roll(jnp.where(left, o, e), HEAD_DIM, 1)
                x_ref[kv, 8 * k:8 * k + 8, 128 * t:128 * t + 128] = jnp.where(left, e, sw)
                x_ref[kv, mp + 8 * k:mp + 8 * k + 8, 128 * t:128 * t + 128] = jnp.where(left, sw, o)
    outs = []
    for kv in range(2):
        x_ref[kv, 2 * mp:2 * mp + 8, :] = pe_ref[kv]
        p = _dot(x_ref[kv].astype(BF16), w1_ref[kv])
        hc = (b1_ref[kv] + p[2 * mp:2 * mp + 1, 0:CMP_HIDDEN]
              + p[2 * mp + 1:2 * mp + 2, CMP_HIDDEN:2 * CMP_HIDDEN])
        per_g = []
        for g in range(N_KV):
            base = g * mp
            h = (p[base:base + nrow, 0:CMP_HIDDEN]
                 + p[base + 1:base + nrow + 1, CMP_HIDDEN:2 * CMP_HIDDEN] + hc)
            per_g.append(_dot(_gelu(h).astype(BF16), w2_ref[kv]) + b2_ref[kv])
        outs.append(jnp.concatenate(per_g, axis=1))
    kc_ref[0] = outs[0].astype(BF16)
    vct_ref[0] = outs[1].T.astype(BF16)


def _compress(table, pages, w1cat, pe8, b1, w2, b2):
    if table is None:
        nb, npg = pages.shape[0], pages.shape[2] // PAGE_SIZE
        table = jnp.zeros((1, 1), jnp.int32)
        index = lambda b, idx, tbl: (b, 0, idx)
    else:
        nb, npg = table.shape
        index = lambda b, idx, tbl: (tbl[b, idx], 0, 0)
    pg = min(64, npg)
    assert npg % pg == 0
    mp = (pg + 1) * 8

    def page_spec(k):
        return pl.BlockSpec(
            (1, KV_ROW, PAGE_SIZE),
            lambda b, j, tbl: index(b, jnp.minimum(j * pg + k, npg - 1), tbl))

    tok = np.arange(PAGE_SIZE)
    perm_np = np.zeros((PAGE_SIZE, PAGE_SIZE), np.float32)
    perm_np[(tok % CMP_STRIDE) * (PAGE_SIZE // CMP_STRIDE) + tok // CMP_STRIDE, tok] = 1.0
    perm = jnp.asarray(perm_np, BF16)
    full = lambda a: pl.BlockSpec(a.shape, lambda b, j, tbl: (0,) * a.ndim)
    grid_spec = pltpu.PrefetchScalarGridSpec(
        num_scalar_prefetch=1,
        grid=(nb, npg // pg),
        in_specs=[page_spec(k) for k in range(pg + 1)]
        + [full(a) for a in (perm, w1cat, pe8, b1, w2, b2)],
        out_specs=[pl.BlockSpec((1, pg * 8, KV_W), lambda b, j, tbl: (b, j, 0)),
                   pl.BlockSpec((1, KV_W, pg * 8), lambda b, j, tbl: (b, 0, j))],
        scratch_shapes=[pltpu.VMEM((2, 2 * mp + 8, CMP_FLAT), F32)],
    )
    return pl.pallas_call(
        functools.partial(_compress_kernel, pg=pg),
        grid_spec=grid_spec,
        out_shape=[jax.ShapeDtypeStruct((nb, npg * 8, KV_W), BF16),
                   jax.ShapeDtypeStruct((nb, KV_W, npg * 8), BF16)],
        compiler_params=pltpu.CompilerParams(dimension_semantics=("parallel", "parallel"),
                                             vmem_limit_bytes=VMEM_LIMIT),
        name="compress",
    )(table, *([pages] * (pg + 1)), perm, w1cat, pe8, b1, w2, b2)


def _topk_rows(imp, n_sel):
    ns = imp.shape[0]
    blk = lax.broadcasted_iota(jnp.int32, imp.shape, 0).astype(F32)
    for _ in range(n_sel):
        mx = jnp.max(imp, axis=0, keepdims=True)
        idx = jnp.min(jnp.where(imp == mx, blk, float(ns)), axis=0, keepdims=True)
        imp = jnp.where(blk == idx, -jnp.inf, imp)
    return imp == -jnp.inf


def _select_mask(imp, qpos):
    blk = lax.broadcasted_iota(jnp.int32, imp.shape, 0)
    cur = qpos // SEL_BLOCK
    forced = (blk == 0) | (blk == cur) | (blk == cur - 1)
    valid = blk * SEL_BLOCK <= qpos
    imp = imp + jnp.where(forced, FORCE_BONUS, 0.0)
    imp = jnp.where(valid, imp, NEG)
    sel = _topk_rows(imp, N_SEL)
    return jnp.where(sel & valid, 0.0, NEG)


def _softmax_rows(s):
    m = jnp.max(s, axis=0, keepdims=True)
    e = jnp.exp2(s - m)
    inv = jnp.where(m > NEG / 2, 1.0 / jnp.sum(e, axis=0, keepdims=True), 0.0)
    return e * inv


def _online_update(s, v_dot, m, l, acc):
    m_new = jnp.maximum(m, jnp.max(s, axis=0, keepdims=True))
    alpha = jnp.exp2(m - m_new)
    p = jnp.exp2(s - m_new)
    l = alpha * l + jnp.sum(p, axis=0, keepdims=True)
    acc = alpha * acc + v_dot(p.astype(BF16))
    return m_new, l, acc


def _nsa_prompt_kernel(qt_ref, gt_ref, kc_ref, vct_ref, ks_ref, vst_ref, kw_ref, vwt_ref,
                       cband_ref, far_ref, seld_ref, wbias_ref, cover_ref,
                       o_ref, bias_ref, msk_ref, s_ref, *, nsub):
    i = pl.program_id(1)
    cols = GQA * Q_BLOCK
    ncp = kc_ref.shape[1]
    gw = GQA * HEAD_DIM
    qpos = i * Q_BLOCK + lax.broadcasted_iota(jnp.int32, (1, Q_BLOCK), 1)
    n_trips = i // nsub + 1
    tk = nsub * CHUNK
    c0 = pl.multiple_of(jnp.maximum(i * 8 - 8, 0), 8)
    crow = lax.broadcasted_iota(jnp.int32, (ncp, 1), 0)
    woff = pl.multiple_of(jnp.maximum(i * Q_BLOCK - WINDOW, 0), Q_BLOCK)
    nwk = WINDOW + Q_BLOCK

    def query(g):
        qblk = qt_ref[g * gw:(g + 1) * gw, :]
        q64 = jnp.concatenate([qblk[r * HEAD_DIM:(r + 1) * HEAD_DIM, :] for r in range(GQA)], axis=1)
        zero = jnp.zeros_like(q64)
        return jnp.concatenate([q64, zero] if g == 0 else [zero, q64], axis=0)

    def dense_branches(g, q):
        far = far_ref[g]
        bias_ref[g] = jnp.where(crow < c0, far, NEG)
        bias_ref[g, pl.ds(c0, 16), :] = cband_ref[g, jnp.minimum(i, 1)]
        p_c = _softmax_rows(_dot(kc_ref[0], q) + bias_ref[g])
        o_c = _dot(vct_ref[0, g * HEAD_DIM:(g + 1) * HEAD_DIM, :], p_c.astype(BF16))
        psum = (p_c[:, 0:Q_BLOCK] + p_c[:, Q_BLOCK:2 * Q_BLOCK]
                + p_c[:, 2 * Q_BLOCK:3 * Q_BLOCK] + p_c[:, 3 * Q_BLOCK:4 * Q_BLOCK])
        imp = _split_dot_l(cover_ref[...], psum)
        s_w = _dot(kw_ref[pl.ds(woff, nwk), :], q) + wbias_ref[g, 0]
        e_w = jnp.exp2(s_w - jnp.max(s_w, axis=0, keepdims=True)).astype(BF16)
        vw = jnp.concatenate([vwt_ref[g * HEAD_DIM:(g + 1) * HEAD_DIM, pl.ds(woff, nwk)],
                              jnp.ones((16, nwk), BF16)], axis=0)
        a_w = _dot(vw, e_w)
        o_w = a_w[0:HEAD_DIM] * (1.0 / a_w[HEAD_DIM:HEAD_DIM + 1])
        mask = _select_mask(imp, qpos)
        msk_ref[g] = jnp.concatenate([mask] * GQA, axis=1) + far
        return o_c, o_w

    def scores(g, q, t, slot):
        t = jnp.minimum(t, n_trips - 1)
        koff = pl.multiple_of(t * tk, tk)
        s = _dot(ks_ref[pl.ds(koff, tk), :], q)
        for j in range(2 * nsub):
            blk = slice(j * SEL_BLOCK, (j + 1) * SEL_BLOCK)
            s_ref[g, slot, blk, :] = s[blk] + msk_ref[g, pl.ds(2 * nsub * t + j, 1), :]

    def attend(g, t, slot, near, carry):
        s = s_ref[g, slot]
        if near:
            s = s + jnp.concatenate(
                [seld_ref[g, jnp.clip(t * nsub + u - i + 2, 0, 2)] for u in range(nsub)], axis=0)
        koff = pl.multiple_of(t * tk, tk)
        vt = jnp.concatenate([vst_ref[g * HEAD_DIM:(g + 1) * HEAD_DIM, pl.ds(koff, tk)],
                              jnp.ones((16, tk), BF16)], axis=0)
        m, acc = carry
        m_new = jnp.maximum(m, jnp.max(s, axis=0, keepdims=True))
        p = jnp.exp2(s - m_new).astype(BF16)
        return m_new, jnp.exp2(m - m_new) * acc + _dot(vt, p)

    def selected(qs):
        groups = range(N_KV)

        def make_pair(near):
            def pair(tt, carry):
                for g in groups:
                    scores(g, qs[g], 2 * tt + 1, 1)
                carry = [attend(g, 2 * tt, 0, near, carry[g]) for g in groups]
                for g in groups:
                    scores(g, qs[g], 2 * tt + 2, 0)
                return [attend(g, 2 * tt + 1, 1, near, carry[g]) for g in groups]
            return pair

        n_far_pairs = (jnp.maximum(i - 1, 0) // nsub) // 2
        init = [(jnp.full((1, cols), -jnp.inf, F32), jnp.zeros((HEAD_DIM + 16, cols), F32))
                for _ in groups]
        carry = lax.fori_loop(0, n_far_pairs, make_pair(False), init)
        carry = lax.fori_loop(n_far_pairs, n_trips // 2, make_pair(True), carry)
        carry = lax.cond(n_trips % 2 == 1,
                         lambda c: [attend(g, n_trips - 1, 0, True, c[g]) for g in groups],
                         lambda c: c, carry)
        return [acc[0:HEAD_DIM] * (1.0 / acc[HEAD_DIM:HEAD_DIM + 1]) for _, acc in carry]

    gt = gt_ref[...]

    def gate(g, j):
        return jnp.concatenate([gt[(g * GQA + r) * 3 + j:(g * GQA + r) * 3 + j + 1, :]
                                for r in range(GQA)], axis=1)

    qs = [query(g) for g in range(N_KV)]
    dense = [dense_branches(g, qs[g]) for g in range(N_KV)]
    for g in range(N_KV):
        scores(g, qs[g], 0, 0)
    o_sel = selected(qs)
    for g in range(N_KV):
        o_c, o_w = dense[g]
        o = gate(g, 0) * o_c + gate(g, 1) * o_sel[g] + gate(g, 2) * o_w
        o_ref[:, g * gw:(g + 1) * gw] = jnp.concatenate(
            [o[:, r * Q_BLOCK:(r + 1) * Q_BLOCK].T for r in range(GQA)], axis=1).astype(BF16)


def _nsa_prompt(qt, gt, kc, vct, ks, vst, kw, vwt, cband, far, seld, wbias, cover_t, bsz, seq):
    nblk = seq // Q_BLOCK
    ncp = kc.shape[1]
    ns = cover_t.shape[0]
    cols = GQA * Q_BLOCK
    nsub = next(c for c in (4, 2, 1) if nblk % c == 0)
    full = lambda a: pl.BlockSpec(a.shape, lambda b, i: (0,) * a.ndim)
    in_specs = [
        pl.BlockSpec((D_B, Q_BLOCK), lambda b, i: (0, b * nblk + i)),
        pl.BlockSpec((GATE_ROWS, Q_BLOCK), lambda b, i: (0, b * nblk + i)),
        pl.BlockSpec((1, ncp, KV_W), lambda b, i: (b, 0, 0)),
        pl.BlockSpec((1, KV_W, ncp), lambda b, i: (b, 0, 0)),
        pl.BlockSpec((seq, KV_W), lambda b, i: (b, 0)),
        pl.BlockSpec((KV_W, seq), lambda b, i: (0, b)),
        pl.BlockSpec((seq, KV_W), lambda b, i: (b, 0)),
        pl.BlockSpec((KV_W, seq), lambda b, i: (0, b)),
        full(cband), full(far), full(seld),
        pl.BlockSpec((N_KV, 1) + wbias.shape[2:],
                     lambda b, i: (0, jnp.minimum(i, WIN_VARIANTS - 1), 0, 0)),
        full(cover_t),
    ]
    return pl.pallas_call(
        functools.partial(_nsa_prompt_kernel, nsub=nsub),
        grid=(bsz, nblk),
        in_specs=in_specs,
        out_specs=pl.BlockSpec((Q_BLOCK, D_B), lambda b, i: (b * nblk + i, 0)),
        out_shape=jax.ShapeDtypeStruct((bsz * seq, D_B), BF16),
        scratch_shapes=[pltpu.VMEM((N_KV, ncp, cols), F32), pltpu.VMEM((N_KV, ns, cols), F32),
                        pltpu.VMEM((N_KV, 2, nsub * CHUNK, cols), F32)],
        compiler_params=pltpu.CompilerParams(
            dimension_semantics=("parallel", "arbitrary"),
            vmem_limit_bytes=VMEM_LIMIT),
        name="nsa_prompt",
    )(qt, gt, kc, vct, ks, vst, kw, vwt, cband, far, seld, wbias, cover_t)


def _nsa_sample_kernel(tbl_ref, *refs, pgs, npg, past, ds):
    del tbl_ref
    page_refs = refs[:pgs]
    (qbd_ref, gt_ref, kc_ref, vct_ref, win_ref, new_ref, cbias_ref, wbias_ref, sfar_ref, slast_ref,
     snew_ref, cover_ref, rsum_ref, o_ref, msk_ref, m_ref, acc_ref, ocw_ref) = refs[pgs:]
    j = pl.program_id(1)
    qbd = qbd_ref[0]
    ncol = qbd.shape[1]
    qcols = ncol // (N_KV * GQA)
    g = gt_ref[0]

    @pl.when(j == 0)
    def _():
        p_c = _softmax_rows(_dot(kc_ref[0], qbd) + cbias_ref[...])
        o_c = _dot(vct_ref[0], p_c.astype(BF16))
        imp = _split_dot(_split_dot_l(cover_ref[...], p_c), rsum_ref[...])
        lane = lax.broadcasted_iota(jnp.int32, (1, ncol), 1)
        qpos = past + (lane % qcols) % ds
        msk_ref[...] = _select_mask(imp, qpos) + sfar_ref[...]
        kw = win_ref[0, :, 0:KV_W].astype(BF16)
        vw = win_ref[0, :, KV_W:KV_ROW].astype(BF16)
        p_w = _softmax_rows(_dot(kw, qbd) + wbias_ref[...])
        o_w = _dot_t(vw, p_w.astype(BF16))
        ocw_ref[...] = g[0:1, :] * o_c + g[2:3, :] * o_w
        m_ref[...] = jnp.full(m_ref.shape, -jnp.inf, F32)
        acc_ref[...] = jnp.zeros(acc_ref.shape, F32)

    last = j == pl.num_programs(1) - 1
    blocks, vts = [], []
    for k, pr in enumerate(page_refs):
        pidx = j * pgs + k
        kk = pr[0, 0:KV_W, :].T.astype(BF16)
        vts.append(pr[0, KV_W:KV_ROW, :].astype(BF16))
        s = _dot(kk, qbd)
        if k == pgs - 1:
            s = s + jnp.where(last, slast_ref[...], 0.0)
        for hb in range(2):
            blocks.append(s[hb * SEL_BLOCK:(hb + 1) * SEL_BLOCK, :]
                          + msk_ref[pl.ds(2 * pidx + hb, 1), :])
    top = blocks[0]
    for s in blocks[1:]:
        top = jnp.maximum(top, s)
    m = m_ref[...]
    m_new = jnp.maximum(m, jnp.max(top, axis=0, keepdims=True))
    p = jnp.concatenate([jnp.exp2(s - m_new).astype(BF16) for s in blocks], axis=0)
    vt = jnp.concatenate([jnp.concatenate(vts, axis=1), jnp.ones((16, pgs * PAGE_SIZE), BF16)], axis=0)
    acc_ref[...] = jnp.exp2(m - m_new) * acc_ref[...] + _dot(vt, p)
    m_ref[...] = m_new

    @pl.when(last)
    def _():
        kn = new_ref[0, :, 0:KV_W].astype(BF16)
        vn = new_ref[0, :, KV_W:KV_ROW].astype(BF16)
        s = _dot(kn, qbd) + snew_ref[...] + msk_ref[pl.ds(2 * npg, 1), :]
        m = m_ref[...]
        m_new = jnp.maximum(m, jnp.max(s, axis=0, keepdims=True))
        p = jnp.exp2(s - m_new).astype(BF16)
        acc = jnp.exp2(m - m_new) * acc_ref[...]
        num = acc[0:KV_W] + _dot_t(vn, p)
        den = acc[KV_W:KV_W + 1] + jnp.sum(p.astype(F32), axis=0, keepdims=True)
        o = ocw_ref[...] + g[1:2, :] * (num * (1.0 / den))
        for gi in range(N_KV):
            o_ref[0, gi] = o[gi * HEAD_DIM:(gi + 1) * HEAD_DIM, gi * GQA * qcols:(gi + 1) * GQA * qcols]


def _nsa_sample(table, pages, qbd, gt, kc, vct, win, new, cbias, wbias, sfar, slast, snew,
                cover_t, rsum, past, ds):
    nb, npg = table.shape
    pgs = min(64, npg)
    assert npg % pgs == 0
    ncol = qbd.shape[2]
    ns = cover_t.shape[0]

    def page_spec(k):
        return pl.BlockSpec((1, KV_ROW, PAGE_SIZE), lambda b, j, tbl: (tbl[b, j * pgs + k], 0, 0))

    per_b = lambda a: pl.BlockSpec((1,) + a.shape[1:], lambda b, j, tbl: (b,) + (0,) * (a.ndim - 1))
    full = lambda a: pl.BlockSpec(a.shape, lambda b, j, tbl: (0,) * a.ndim)
    grid_spec = pltpu.PrefetchScalarGridSpec(
        num_scalar_prefetch=1,
        grid=(nb, npg // pgs),
        in_specs=[page_spec(k) for k in range(pgs)]
        + [per_b(a) for a in (qbd, gt, kc, vct, win, new)]
        + [full(a) for a in (cbias, wbias, sfar, slast, snew, cover_t, rsum)],
        out_specs=pl.BlockSpec((1, N_KV, HEAD_DIM, ncol // N_KV), lambda b, j, tbl: (b, 0, 0, 0)),
        scratch_shapes=[pltpu.VMEM((ns, ncol), F32), pltpu.VMEM((1, ncol), F32),
                        pltpu.VMEM((KV_W + 16, ncol), F32), pltpu.VMEM((KV_W, ncol), F32)],
    )
    return pl.pallas_call(
        functools.partial(_nsa_sample_kernel, pgs=pgs, npg=npg, past=past, ds=ds),
        grid_spec=grid_spec,
        out_shape=jax.ShapeDtypeStruct((nb, N_KV, HEAD_DIM, ncol // N_KV), F32),
        compiler_params=pltpu.CompilerParams(dimension_semantics=("parallel", "arbitrary"),
                                             vmem_limit_bytes=VMEM_LIMIT),
        name="nsa_sample",
    )(table, *([pages] * pgs), qbd, gt, kc, vct, win, new, cbias, wbias, sfar, slast, snew,
      cover_t, rsum)


def _cover_t(nc, ns, nc_pad, ns_pad):
    c0 = np.arange(nc) * CMP_STRIDE
    s0 = np.arange(ns) * SEL_BLOCK
    m = (c0[None, :] < s0[:, None] + SEL_BLOCK) & (c0[None, :] + CMP_BLOCK > s0[:, None])
    out = np.zeros((ns_pad, nc_pad), np.float32)
    out[:ns, :nc] = m
    return jnp.asarray(out, BF16)


def _prep_weights(norm_g, w_in, ln_v_g, spatial_w, spatial_b, cmp_pe, cmp_w1, cmp_b1, cmp_w2, cmp_b2,
                  w_out, final_g):
    offs = np.cumsum((D_A, D_A, D_A, D_B, 2 * KV_W, 2 * KV_W, 2 * KV_W, 3 * N_HEADS, D_B))
    g0, g1 = int(offs[6]), int(offs[7])
    w_perm = jnp.concatenate(
        [w_in[:, :g0], w_in[:, g1:], w_in[:, g0:g1],
         jnp.zeros((D_MODEL, GATE_PAD - 3 * N_HEADS), w_in.dtype)], axis=1).astype(BF16)
    pavg = jnp.asarray(np.kron(np.eye(A_GROUPS), np.full((HEAD_DIM, HEAD_DIM), 1.0 / HEAD_DIM)), BF16)
    w1cat = jnp.concatenate([cmp_w1[:, s].reshape(2, CMP_FLAT, CMP_HIDDEN) for s in range(CMP_R)],
                            axis=2).astype(BF16)
    pe8 = jnp.concatenate([cmp_pe.reshape(2, CMP_R, CMP_FLAT),
                           jnp.zeros((2, 8 - CMP_R, CMP_FLAT), F32)], axis=1)
    return dict(
        norm_g=norm_g.reshape(1, D_MODEL), w_perm=w_perm, ln_g=ln_v_g.reshape(1, D_A), pavg=pavg,
        w1cat=w1cat, pe8=pe8, b1=cmp_b1.reshape(2, 1, CMP_HIDDEN), w2=cmp_w2.astype(BF16),
        b2=cmp_b2.reshape(2, 1, HEAD_DIM), wo=w_out.astype(BF16), fg=final_g.reshape(1, D_MODEL),
        spatial_w=spatial_w, spatial_b=spatial_b)


def _spatial_operands(spatial_w, spatial_b, n):
    reps = CHUNK // n
    w = jnp.tril(spatial_w[:, :n, :n])
    eye = jnp.eye(reps, dtype=w.dtype)
    wsp = jnp.einsum('ab,gts->gatbs', eye, w).reshape(A_GROUPS, CHUNK, CHUNK).astype(BF16)
    b = jnp.tile(spatial_b[:, :n].T, (reps, 1))
    bsp = jnp.repeat(b, HEAD_DIM, axis=1)
    return wsp, bsp


def _prompt_bias_buckets():
    ql = np.arange(Q_BLOCK)[None, :]
    cl = np.arange(16)[:, None]
    band = np.stack([ql - CMP_STRIDE * cl - (CMP_BLOCK - 1),
                     ql + 97 - CMP_STRIDE * cl])
    kl = np.arange(CHUNK)[:, None]
    seld = np.stack([np.full((CHUNK, Q_BLOCK), FAR_DIST), CHUNK + ql - kl, ql - kl])
    wl = np.arange(WINDOW + Q_BLOCK)[:, None]
    dw = np.stack([Q_BLOCK * v + ql - wl for v in range(WIN_VARIANTS)])
    dw = np.where(dw < WINDOW, dw, -1)
    far = np.full((8, Q_BLOCK), FAR_DIST)
    tile4 = lambda d: np.tile(_t5_bucket_np(d.reshape(-1, Q_BLOCK)), (1, GQA))
    return tile4(band), tile4(seld), tile4(dw), tile4(far)


def _sample_bias_buckets(past, ds, qpad, ncp, nwin_pad):
    ql = (np.arange(qpad) % ds)[None, :]
    c = np.arange(ncp)[:, None]
    dc = past + ql - (CMP_STRIDE * c + CMP_BLOCK - 1)
    wl = np.arange(nwin_pad)[:, None]
    dw = WINDOW + ql - wl
    dw = np.where((dw < WINDOW) & (wl < WINDOW + ds), dw, -1)
    kl = np.arange(PAGE_SIZE)[:, None]
    dlast = PAGE_SIZE + ql - kl
    nl = np.arange(16)[:, None]
    dnew = np.where(nl < ds, ql - nl, -1)
    far = np.full((8, qpad), FAR_DIST)
    tile4 = lambda d: np.tile(_t5_bucket_np(d), (1, GQA))
    return tile4(dc), tile4(dw), tile4(dlast), tile4(dnew), tile4(far)


def kernel(x_prompt, x_sample, cache_cmp_kv, cache_sel_kv, state_win_kv, page_table, norm_g, w_in,
           ln_v_g, spatial_w, spatial_b, cmp_pe, cmp_w1, cmp_b1, cmp_w2, cmp_b2, rel_table, w_out,
           final_g):
    depth = norm_g.shape[0]
    assert depth == 1
    bsz, seq = x_prompt.shape[:2]
    db, ds = x_sample.shape[:2]
    npg = page_table.shape[1]
    past = npg * PAGE_SIZE
    win_buf = state_win_kv.shape[2]
    assert win_buf == WINDOW and past >= WINDOW and seq % Q_BLOCK == 0 and seq >= WINDOW + Q_BLOCK
    assert CHUNK % ds == 0 and ds <= 8 and (db * ds) % CHUNK == 0
    nblk = seq // Q_BLOCK
    l = 0
    wts = _prep_weights(norm_g[l], w_in[l], ln_v_g[l], spatial_w[l], spatial_b[l], cmp_pe[l],
                        cmp_w1[l], cmp_b1[l], cmp_w2[l], cmp_b2[l], w_out[l], final_g)
    inproj = lambda x, nb: _inproj(x, nb, wts['norm_g'], wts['w_perm'], wts['ln_g'], wts['pavg'])
    compress = lambda tbl, pages: _compress(tbl, pages, wts['w1cat'], wts['pe8'], wts['b1'],
                                            wts['w2'], wts['b2'])
    kv_out = lambda a: a.reshape(a.shape[0], 2, N_KV, HEAD_DIM, a.shape[2]).transpose(0, 4, 1, 2, 3)[None]
    pages_t = lambda c: c.transpose(0, 2, 3, 4, 1).reshape(c.shape[0], KV_ROW, PAGE_SIZE)

    n_p = bsz * seq
    xp = x_prompt.reshape(n_p, D_MODEL)
    ua, _, vnb, zas, zbs, kvc, kvs, kvw, qt, gt, ksk, vst, kwk, vwt = inproj(xp, bsz)
    kc, vct = compress(None, kvc)
    ncp = seq // CMP_STRIDE
    cover_p = _cover_t(ncp - CMP_R + 1, seq // SEL_BLOCK, ncp, seq // SEL_BLOCK)
    band_b, seld_b, win_b, far_b = _prompt_bias_buckets()
    cols = GQA * Q_BLOCK
    cband = _bias_tiles(rel_table, band_b, Q_BLOCK).reshape(N_KV, 2, 16, cols)
    seld = _bias_tiles(rel_table, seld_b, Q_BLOCK, rel_far=True).reshape(N_KV, 3, CHUNK, cols)
    wbias = _bias_tiles(rel_table, win_b, Q_BLOCK).reshape(N_KV, WIN_VARIANTS, WINDOW + Q_BLOCK, cols)
    far = _bias_tiles(rel_table, far_b, Q_BLOCK)[:, 0:1]
    ob = _nsa_prompt(qt, gt, kc, vct, ksk, vst, kwk, vwt, cband, far, seld, wbias, cover_p, bsz, seq)
    wsp, bsp = _spatial_operands(wts['spatial_w'], wts['spatial_b'], CHUNK)
    y_prompt = _mixout(xp, ua, vnb, zas, ob, zbs, wsp, bsp, wts['wo'], wts['fg']).reshape(bsz, seq, D_MODEL)
    new_cmp_p = kv_out(kvc)
    new_sel_p = kv_out(kvs)
    new_win_p = kv_out(kvw[:, :, seq - win_buf:])

    n_s = db * ds
    xs = x_sample.reshape(n_s, D_MODEL)
    ua, vn, vnb, zas, zbs, kvc, kvs, kvw, qt, gt, _, _, _, _ = inproj(xs, 1)
    kvc, kvs, kvw = (a[0].T for a in (kvc, kvs, kvw))
    kc, vct = compress(page_table, pages_t(cache_cmp_kv[l]))
    ncs = past // CMP_STRIDE
    t_all = past + ds
    nss = -(-t_all // SEL_BLOCK)
    nss_pad = -(-nss // 8) * 8
    cover_s = _cover_t(t_all // CMP_STRIDE - CMP_R + 1, nss, ncs, nss_pad)
    qpad = 16
    ncol = N_KV * GQA * qpad
    nwin_pad = -(-(win_buf + ds) // 16) * 16
    dc_b, dw_b, dl_b, dn_b, far_b = _sample_bias_buckets(past, ds, qpad, ncs, nwin_pad)
    both = lambda t: jnp.concatenate([t[0], t[1]], axis=1)
    cbias_s = both(_bias_tiles(rel_table, dc_b, qpad))
    wbias_s = both(_bias_tiles(rel_table, dw_b, qpad))
    slast = both(_bias_tiles(rel_table, dl_b, qpad, rel_far=True))
    snew = both(_bias_tiles(rel_table, dn_b, qpad, rel_far=True))
    sfar = both(_bias_tiles(rel_table, far_b, qpad))[0:1]

    qg = qt.reshape(N_KV, GQA, HEAD_DIM, db, ds).transpose(3, 0, 2, 1, 4)
    qg = jnp.tile(qg, (1, 1, 1, 1, qpad // ds)).reshape(db, N_KV, HEAD_DIM, GQA * qpad)
    zq = jnp.zeros_like(qg[:, 0])
    qbd = jnp.concatenate([jnp.concatenate([qg[:, 0], zq], axis=2),
                           jnp.concatenate([zq, qg[:, 1]], axis=2)], axis=1)
    gts = gt[:3 * N_HEADS].reshape(N_KV, GQA, 3, db, ds).transpose(3, 2, 0, 1, 4)
    gts = jnp.tile(gts, (1, 1, 1, 1, qpad // ds)).reshape(db, 3, ncol)
    win = jnp.concatenate([state_win_kv[l].reshape(db, win_buf, KV_ROW),
                           kvw.reshape(db, ds, KV_ROW)], axis=1)
    win_pad = jnp.pad(win, ((0, 0), (0, nwin_pad - win_buf - ds), (0, 0)))
    new_pad = jnp.pad(kvs.reshape(db, ds, KV_ROW), ((0, 0), (0, 16 - ds), (0, 0)))
    rsum = jnp.asarray(np.kron(np.eye(N_KV), np.kron(np.ones((GQA, GQA)), np.eye(qpad))), BF16)
    o_s = _nsa_sample(page_table, pages_t(cache_sel_kv[l]), qbd, gts,
                      kc, vct, win_pad, new_pad, cbias_s, wbias_s, sfar, slast, snew,
                      cover_s, rsum, past, ds)
    ob = (o_s.reshape(db, N_KV, HEAD_DIM, GQA, qpad)[..., :ds].transpose(0, 4, 1, 3, 2)
          .reshape(n_s, D_B))
    wsp, bsp = _spatial_operands(wts['spatial_w'], wts['spatial_b'], ds)
    y_sample = _mixout(xs, ua, vnb, zas, ob, zbs, wsp, bsp, wts['wo'], wts['fg']).reshape(db, ds, D_MODEL)
    kv5 = lambda a: a.reshape(1, db, ds, 2, N_KV, HEAD_DIM)
    new_cmp_s = kv5(kvc)
    new_sel_s = kv5(kvs)
    new_win_s = win[:, ds:].reshape(1, db, win_buf, 2, N_KV, HEAD_DIM)
    new_chunk_v = vn.reshape(1, db, ds, D_A)
    return (y_prompt, y_sample, new_cmp_p, new_sel_p, new_win_p, new_cmp_s, new_sel_s, new_win_s,
            new_chunk_v)
```

```python
import functools
import math

import numpy as np
import jax
import jax.numpy as jnp
from jax import lax
from jax.experimental import pallas as pl
from jax.experimental.pallas import tpu as pltpu

F32 = jnp.float32
BF16 = jnp.bfloat16

D_MODEL = 1024
HEAD_DIM = 64
D_A = 512
D_B = 512
A_GROUPS = D_A // HEAD_DIM
CHUNK = 128
N_HEADS = D_B // HEAD_DIM
N_KV = 2
GQA = N_HEADS // N_KV
KV_W = N_KV * HEAD_DIM
KV_ROW = 2 * KV_W
CMP_STRIDE = 16
CMP_BLOCK = 32
CMP_R = CMP_BLOCK // CMP_STRIDE
CMP_HIDDEN = 256
CMP_FLAT = CMP_STRIDE * HEAD_DIM
SEL_BLOCK = 64
N_SEL = 16
WINDOW = 512
N_BUCKETS = 32
MAX_DISTANCE = 128
Q_BLOCK = 128
PAGE_SIZE = 128
RMS_EPS = 1e-6
LN_EPS = 1e-5
NEG = -1e30
FORCE_BONUS = 1e6
LOG2E = 1.4426950408889634
Q_SCALE = HEAD_DIM ** -0.5 * LOG2E
FAR_DIST = 1 << 20
WIN_VARIANTS = WINDOW // Q_BLOCK + 1

_OFF_U, _OFF_V, _OFF_ZA, _OFF_Q = 0, 512, 1024, 1536
_OFF_KVC, _OFF_KVS, _OFF_KVW, _OFF_ZB, _OFF_G = 2048, 2304, 2560, 2816, 3328
D_IN_PAD = 3456
GATE_PAD = 128
GATE_ROWS = 32

VMEM_LIMIT = 52 * 1024 * 1024


def _gelu(x):
    return x * (0.5 * (1.0 + jnp.tanh(0.7978845608028654 * (x + 0.044715 * (x * x * x)))))


def _sigmoid(x):
    return 1.0 / (1.0 + jnp.exp(-x))


def _dot(a, b):
    return jnp.dot(a, b, preferred_element_type=F32)


def _dot_t(a, b):
    return lax.dot_general(a, b, (((0,), (0,)), ((), ())), preferred_element_type=F32)


def _dot_nt(a, b):
    return lax.dot_general(a, b, (((1,), (1,)), ((), ())), preferred_element_type=F32)


def _split_dot(a, b):
    hi = a.astype(BF16)
    lo = (a - hi.astype(F32)).astype(BF16)
    return _dot(hi, b) + _dot(lo, b)


def _split_dot_l(a, b):
    hi = b.astype(BF16)
    lo = (b - hi.astype(F32)).astype(BF16)
    return _dot(a, hi) + _dot(a, lo)


def _inproj_kernel(x_ref, ng_ref, w_ref, lng_ref, pavg_ref,
                   ua_ref, vn_ref, vnb_ref, zas_ref, zbs_ref, kvc_ref, kvs_ref, kvw_ref,
                   qt_ref, gt_ref, ksk_ref, vst_ref, kwk_ref, vwt_ref):
    x = x_ref[...]
    ms = jnp.mean(x * x, axis=-1, keepdims=True)
    h = (x * lax.rsqrt(ms + RMS_EPS) * ng_ref[...]).astype(BF16)

    def proj(a, b):
        return _dot(h, w_ref[:, a:b])

    ua_ref[...] = _gelu(proj(_OFF_U, _OFF_V)).astype(BF16)
    v = _gelu(proj(_OFF_V, _OFF_ZA))
    mu = _split_dot(v, pavg_ref[...])
    d = v - mu
    var = _dot((d * d).astype(BF16), pavg_ref[...])
    vn = d * lax.rsqrt(var + LN_EPS) * lng_ref[...]
    vn_ref[...] = vn
    vnb_ref[...] = vn.astype(BF16)
    za = proj(_OFF_ZA, _OFF_Q)
    zas_ref[...] = (za * _sigmoid(za)).astype(BF16)
    zb = proj(_OFF_ZB, _OFF_G)
    zbs_ref[...] = (zb * _sigmoid(zb)).astype(BF16)
    qt_ref[...] = (proj(_OFF_Q, _OFF_KVC) * Q_SCALE).T.astype(BF16)
    gt_ref[...] = _sigmoid(proj(_OFF_G, D_IN_PAD)).T[0:GATE_ROWS, :]
    kvc_ref[0] = proj(_OFF_KVC, _OFF_KVS).T
    kvs = proj(_OFF_KVS, _OFF_KVW)
    kvs_t = kvs.T
    kvs_ref[0] = kvs_t
    ksk_ref[...] = kvs[:, 0:KV_W].astype(BF16)
    vst_ref[...] = kvs_t[KV_W:KV_ROW, :].astype(BF16)
    kvw = proj(_OFF_KVW, _OFF_ZB)
    kvw_t = kvw.T
    kvw_ref[0] = kvw_t
    kwk_ref[...] = kvw[:, 0:KV_W].astype(BF16)
    vwt_ref[...] = kvw_t[KV_W:KV_ROW, :].astype(BF16)


def _inproj(x, nb, norm_g, w_perm, ln_g, pavg):
    n = x.shape[0]
    s = n // nb
    tm = min(512, s)
    assert s % tm == 0
    per = s // tm
    row = lambda w: pl.BlockSpec((tm, w), lambda i: (i, 0))
    col = lambda h: pl.BlockSpec((h, tm), lambda i: (0, i))
    kvt = pl.BlockSpec((1, KV_ROW, tm), lambda i: (i // per, 0, i % per))
    full = lambda a: pl.BlockSpec(a.shape, lambda i: (0,) * a.ndim)
    rows = [(D_A, BF16), (D_A, F32), (D_A, BF16), (D_A, BF16), (D_B, BF16)]
    sds = lambda shape, dt: jax.ShapeDtypeStruct(shape, dt)
    out_specs = ([row(w) for w, _ in rows] + [kvt, kvt, kvt]
                 + [col(D_B), col(GATE_ROWS), row(KV_W), col(KV_W), row(KV_W), col(KV_W)])
    out_shape = ([sds((n, w), dt) for w, dt in rows] + [sds((nb, KV_ROW, s), F32)] * 3
                 + [sds((D_B, n), BF16), sds((GATE_ROWS, n), F32), sds((n, KV_W), BF16),
                    sds((KV_W, n), BF16), sds((n, KV_W), BF16), sds((KV_W, n), BF16)])
    return pl.pallas_call(
        _inproj_kernel,
        grid=(n // tm,),
        in_specs=[row(D_MODEL), full(norm_g), full(w_perm), full(ln_g), full(pavg)],
        out_specs=out_specs,
        out_shape=out_shape,
        compiler_params=pltpu.CompilerParams(dimension_semantics=("parallel",),
                                             vmem_limit_bytes=VMEM_LIMIT),
        name="inproj",
    )(x, norm_g, w_perm, ln_g, pavg)


def _mixout_kernel(x_ref, ua_ref, vn_ref, zas_ref, ob_ref, zbs_ref, wsp_ref, bsp_ref, wo_ref, fg_ref,
                   y_ref, s_ref):
    tm = x_ref.shape[0]
    for c in range(tm // CHUNK):
        rows = slice(c * CHUNK, (c + 1) * CHUNK)
        vc = vn_ref[rows, :]
        for g in range(A_GROUPS):
            cols = slice(g * HEAD_DIM, (g + 1) * HEAD_DIM)
            s_ref[rows, cols] = _dot(wsp_ref[g], vc[:, cols])
        s_ref[rows, :] = s_ref[rows, :] + bsp_ref[...]
    mix_a = (ua_ref[...].astype(F32) * s_ref[...] * zas_ref[...].astype(F32)).astype(BF16)
    mix_b = (ob_ref[...].astype(F32) * zbs_ref[...].astype(F32)).astype(BF16)
    y = x_ref[...] + _dot(mix_a, wo_ref[0:D_A, :]) + _dot(mix_b, wo_ref[D_A:D_A + D_B, :])
    ms = jnp.mean(y * y, axis=-1, keepdims=True)
    y_ref[...] = y * lax.rsqrt(ms + RMS_EPS) * fg_ref[...]


def _mixout(x, ua, vn, zas, ob, zbs, wsp, bsp, wo, fg):
    n = x.shape[0]
    tm = min(512, n)
    row = lambda w: pl.BlockSpec((tm, w), lambda i: (i, 0))
    full = lambda a: pl.BlockSpec(a.shape, lambda i: (0,) * a.ndim)
    return pl.pallas_call(
        _mixout_kernel,
        grid=(n // tm,),
        in_specs=[row(D_MODEL), row(D_A), row(D_A), row(D_A), row(D_B), row(D_B),
                  full(wsp), full(bsp), full(wo), full(fg)],
        out_specs=row(D_MODEL),
        out_shape=jax.ShapeDtypeStruct((n, D_MODEL), F32),
        scratch_shapes=[pltpu.VMEM((tm, D_A), F32)],
        compiler_params=pltpu.CompilerParams(dimension_semantics=("parallel",),
                                             vmem_limit_bytes=VMEM_LIMIT),
        name="mixout",
    )(x, ua, vn, zas, ob, zbs, wsp, bsp, wo, fg)


def _t5_bucket_np(dist):
    dist = np.asarray(dist, np.int64)
    n = np.maximum(dist, 0)
    max_exact = N_BUCKETS // 2
    nf = np.maximum(n, max_exact).astype(np.float64)
    large = max_exact + (np.log(nf / max_exact) / math.log(MAX_DISTANCE / max_exact)
                         * (N_BUCKETS - max_exact)).astype(np.int64)
    b = np.where(n < max_exact, n, np.minimum(large, N_BUCKETS - 1))
    return np.where(dist < 0, -1, b).astype(np.int32)


def _bias_kernel(tbl_ref, bkt_ref, out_ref, *, qb, rel_far):
    b = bkt_ref[...]
    grp = lax.broadcasted_iota(jnp.int32, (1, b.shape[1]), 1) // qb

    def head_row(k, g):
        row = jnp.zeros(grp.shape, F32)
        for r in range(GQA):
            row = jnp.where(grp == r, tbl_ref[k * N_HEADS + g * GQA + r], row)
        return row * LOG2E

    for g in range(N_KV):
        acc = jnp.full(b.shape, NEG, F32)
        base = head_row(N_BUCKETS - 1, g) if rel_far else None
        for k in range(N_BUCKETS):
            row = head_row(k, g)
            if rel_far:
                row = row - base
            acc = jnp.where(b == k, row, acc)
        out_ref[g] = acc


def _bias_tiles(rel_table, bkt, qb, rel_far=False):
    r, c = bkt.shape
    rb = r
    for cand in (512, 256, 128, 64, 32, 16, 8):
        if r % cand == 0:
            rb = cand
            break
    return pl.pallas_call(
        functools.partial(_bias_kernel, qb=qb, rel_far=rel_far),
        grid=(r // rb,),
        in_specs=[pl.BlockSpec(memory_space=pltpu.SMEM), pl.BlockSpec((rb, c), lambda i: (i, 0))],
        out_specs=pl.BlockSpec((N_KV, rb, c), lambda i: (0, i, 0)),
        out_shape=jax.ShapeDtypeStruct((N_KV, r, c), F32),
        compiler_params=pltpu.CompilerParams(dimension_semantics=("parallel",)),
        name="bias_tiles",
    )(rel_table.reshape(-1), jnp.asarray(bkt))


def _compress_kernel(tbl_ref, *refs, pg):
    del tbl_ref
    page_refs = refs[:pg + 1]
    perm_ref, w1_ref, pe_ref, b1_ref, w2_ref, b2_ref, kc_ref, vct_ref, x_ref = refs[pg + 1:]
    mp = (pg + 1) * 8
    nrow = pg * 8
    left = lax.broadcasted_iota(jnp.int32, (8, 128), 1) < HEAD_DIM
    perm = perm_ref[...]
    for k, pr in enumerate(page_refs):
        tok = _dot_nt(perm, pr[0].astype(BF16))
        for kv in range(2):
            for t in range(CMP_STRIDE // 2):
                e = tok[16 * t:16 * t + 8, kv * KV_W:(kv + 1) * KV_W]
                o = tok[16 * t + 8:16 * t + 16, kv * KV_W:(kv + 1) * KV_W]
                sw = pltpu.roll(jnp.where(left, o, e), HEAD_DIM, 1)
                x_ref[kv, 8 * k:8 * k + 8, 128 * t:128 * t + 128] = jnp.where(left, e, sw)
                x_ref[kv, mp + 8 * k:mp + 8 * k + 8, 128 * t:128 * t + 128] = jnp.where(left, sw, o)
    outs = []
    for kv in range(2):
        x_ref[kv, 2 * mp:2 * mp + 8, :] = pe_ref[kv]
        p = _dot(x_ref[kv].astype(BF16), w1_ref[kv])
        hc = (b1_ref[kv] + p[2 * mp:2 * mp + 1, 0:CMP_HIDDEN]
              + p[2 * mp + 1:2 * mp + 2, CMP_HIDDEN:2 * CMP_HIDDEN])
        per_g = []
        for g in range(N_KV):
            base = g * mp
            h = (p[base:base + nrow, 0:CMP_HIDDEN]
                 + p[base + 1:base + nrow + 1, CMP_HIDDEN:2 * CMP_HIDDEN] + hc)
            per_g.append(_dot(_gelu(h).astype(BF16), w2_ref[kv]) + b2_ref[kv])
        outs.append(jnp.concatenate(per_g, axis=1))
    kc_ref[0] = outs[0].astype(BF16)
    vct_ref[0] = outs[1].T.astype(BF16)


def _compress(table, pages, w1cat, pe8, b1, w2, b2):
    if table is None:
        nb, npg = pages.shape[0], pages.shape[2] // PAGE_SIZE
        table = jnp.zeros((1, 1), jnp.int32)
        index = lambda b, idx, tbl: (b, 0, idx)
    else:
        nb, npg = table.shape
        index = lambda b, idx, tbl: (tbl[b, idx], 0, 0)
    pg = min(64, npg)
    assert npg % pg == 0
    mp = (pg + 1) * 8

    def page_spec(k):
        return pl.BlockSpec(
            (1, KV_ROW, PAGE_SIZE),
            lambda b, j, tbl: index(b, jnp.minimum(j * pg + k, npg - 1), tbl))

    tok = np.arange(PAGE_SIZE)
    perm_np = np.zeros((PAGE_SIZE, PAGE_SIZE), np.float32)
    perm_np[(tok % CMP_STRIDE) * (PAGE_SIZE // CMP_STRIDE) + tok // CMP_STRIDE, tok] = 1.0
    perm = jnp.asarray(perm_np, BF16)
    full = lambda a: pl.BlockSpec(a.shape, lambda b, j, tbl: (0,) * a.ndim)
    grid_spec = pltpu.PrefetchScalarGridSpec(
        num_scalar_prefetch=1,
        grid=(nb, npg // pg),
        in_specs=[page_spec(k) for k in range(pg + 1)]
        + [full(a) for a in (perm, w1cat, pe8, b1, w2, b2)],
        out_specs=[pl.BlockSpec((1, pg * 8, KV_W), lambda b, j, tbl: (b, j, 0)),
                   pl.BlockSpec((1, KV_W, pg * 8), lambda b, j, tbl: (b, 0, j))],
        scratch_shapes=[pltpu.VMEM((2, 2 * mp + 8, CMP_FLAT), F32)],
    )
    return pl.pallas_call(
        functools.partial(_compress_kernel, pg=pg),
        grid_spec=grid_spec,
        out_shape=[jax.ShapeDtypeStruct((nb, npg * 8, KV_W), BF16),
                   jax.ShapeDtypeStruct((nb, KV_W, npg * 8), BF16)],
        compiler_params=pltpu.CompilerParams(dimension_semantics=("parallel", "parallel"),
                                             vmem_limit_bytes=VMEM_LIMIT),
        name="compress",
    )(table, *([pages] * (pg + 1)), perm, w1cat, pe8, b1, w2, b2)


def _topk_rows(imp, n_sel):
    ns = imp.shape[0]
    blk = lax.broadcasted_iota(jnp.int32, imp.shape, 0).astype(F32)
    for _ in range(n_sel):
        mx = jnp.max(imp, axis=0, keepdims=True)
        idx = jnp.min(jnp.where(imp == mx, blk, float(ns)), axis=0, keepdims=True)
        imp = jnp.where(blk == idx, -jnp.inf, imp)
    return imp == -jnp.inf


def _select_mask(imp, qpos):
    blk = lax.broadcasted_iota(jnp.int32, imp.shape, 0)
    cur = qpos // SEL_BLOCK
    forced = (blk == 0) | (blk == cur) | (blk == cur - 1)
    valid = blk * SEL_BLOCK <= qpos
    imp = imp + jnp.where(forced, FORCE_BONUS, 0.0)
    imp = jnp.where(valid, imp, NEG)
    sel = _topk_rows(imp, N_SEL)
    return jnp.where(sel & valid, 0.0, NEG)


def _softmax_rows(s):
    m = jnp.max(s, axis=0, keepdims=True)
    e = jnp.exp2(s - m)
    inv = jnp.where(m > NEG / 2, 1.0 / jnp.sum(e, axis=0, keepdims=True), 0.0)
    return e * inv


def _online_update(s, v_dot, m, l, acc):
    m_new = jnp.maximum(m, jnp.max(s, axis=0, keepdims=True))
    alpha = jnp.exp2(m - m_new)
    p = jnp.exp2(s - m_new)
    l = alpha * l + jnp.sum(p, axis=0, keepdims=True)
    acc = alpha * acc + v_dot(p.astype(BF16))
    return m_new, l, acc


def _nsa_prompt_kernel(qt_ref, gt_ref, kc_ref, vct_ref, ks_ref, vst_ref, kw_ref, vwt_ref,
                       cband_ref, far_ref, seld_ref, wbias_ref, cover_ref,
                       o_ref, bias_ref, msk_ref, s_ref, *, nsub):
    i = pl.program_id(1)
    cols = GQA * Q_BLOCK
    ncp = kc_ref.shape[1]
    gw = GQA * HEAD_DIM
    qpos = i * Q_BLOCK + lax.broadcasted_iota(jnp.int32, (1, Q_BLOCK), 1)
    n_trips = i // nsub + 1
    tk = nsub * CHUNK
    c0 = pl.multiple_of(jnp.maximum(i * 8 - 8, 0), 8)
    crow = lax.broadcasted_iota(jnp.int32, (ncp, 1), 0)
    woff = pl.multiple_of(jnp.maximum(i * Q_BLOCK - WINDOW, 0), Q_BLOCK)
    nwk = WINDOW + Q_BLOCK

    def query(g):
        qblk = qt_ref[g * gw:(g + 1) * gw, :]
        q64 = jnp.concatenate([qblk[r * HEAD_DIM:(r + 1) * HEAD_DIM, :] for r in range(GQA)], axis=1)
        zero = jnp.zeros_like(q64)
        return jnp.concatenate([q64, zero] if g == 0 else [zero, q64], axis=0)

    def compressed(g, q):
        far = far_ref[g]
        bias_ref[g] = jnp.where(crow < c0, far, NEG)
        bias_ref[g, pl.ds(c0, 16), :] = cband_ref[g, jnp.minimum(i, 1)]
        s_c = _dot(kc_ref[0], q) + bias_ref[g]
        m_c = jnp.max(s_c, axis=0, keepdims=True)
        e_c = jnp.exp2(s_c - m_c).astype(BF16)
        vc = jnp.concatenate([vct_ref[0, g * HEAD_DIM:(g + 1) * HEAD_DIM, :],
                              jnp.ones((16, ncp), BF16)], axis=0)
        a_c = _dot(vc, e_c)
        inv = jnp.where(m_c > NEG / 2, 1.0 / a_c[HEAD_DIM:HEAD_DIM + 1], 0.0)
        o_c = a_c[0:HEAD_DIM] * inv
        w = _dot(cover_ref[...], e_c) * inv
        imp = (w[:, 0:Q_BLOCK] + w[:, Q_BLOCK:2 * Q_BLOCK]
               + w[:, 2 * Q_BLOCK:3 * Q_BLOCK] + w[:, 3 * Q_BLOCK:4 * Q_BLOCK])
        return o_c, imp

    def window(g, q):
        s_w = _dot(kw_ref[pl.ds(woff, nwk), :], q) + wbias_ref[g, 0]
        e_w = jnp.exp2(s_w - jnp.max(s_w, axis=0, keepdims=True)).astype(BF16)
        vw = jnp.concatenate([vwt_ref[g * HEAD_DIM:(g + 1) * HEAD_DIM, pl.ds(woff, nwk)],
                              jnp.ones((16, nwk), BF16)], axis=0)
        a_w = _dot(vw, e_w)
        return a_w[0:HEAD_DIM] * (1.0 / a_w[HEAD_DIM:HEAD_DIM + 1])

    def select(g, imp):
        mask = _select_mask(imp, qpos)
        msk_ref[g] = jnp.concatenate([mask] * GQA, axis=1) + far_ref[g]

    def scores(g, q, t, slot):
        t = jnp.minimum(t, n_trips - 1)
        koff = pl.multiple_of(t * tk, tk)
        s = _dot(ks_ref[pl.ds(koff, tk), :], q)
        for j in range(2 * nsub):
            blk = slice(j * SEL_BLOCK, (j + 1) * SEL_BLOCK)
            s_ref[g, slot, blk, :] = s[blk] + msk_ref[g, pl.ds(2 * nsub * t + j, 1), :]

    def attend(g, t, slot, near, carry):
        s = s_ref[g, slot]
        if near:
            s = s + jnp.concatenate(
                [seld_ref[g, jnp.clip(t * nsub + u - i + 2, 0, 2)] for u in range(nsub)], axis=0)
        koff = pl.multiple_of(t * tk, tk)
        vt = jnp.concatenate([vst_ref[g * HEAD_DIM:(g + 1) * HEAD_DIM, pl.ds(koff, tk)],
                              jnp.ones((16, tk), BF16)], axis=0)
        m, acc = carry
        m_new = jnp.maximum(m, jnp.max(s, axis=0, keepdims=True))
        p = jnp.exp2(s - m_new).astype(BF16)
        return m_new, jnp.exp2(m - m_new) * acc + _dot(vt, p)

    def selected(qs):
        groups = range(N_KV)

        def make_pair(near):
            def pair(tt, carry):
                for g in groups:
                    scores(g, qs[g], 2 * tt + 1, 1)
                carry = [attend(g, 2 * tt, 0, near, carry[g]) for g in groups]
                for g in groups:
                    scores(g, qs[g], 2 * tt + 2, 0)
                return [attend(g, 2 * tt + 1, 1, near, carry[g]) for g in groups]
            return pair

        n_far_pairs = (jnp.maximum(i - 1, 0) // nsub) // 2
        init = [(jnp.full((1, cols), -jnp.inf, F32), jnp.zeros((HEAD_DIM + 16, cols), F32))
                for _ in groups]
        carry = lax.fori_loop(0, n_far_pairs, make_pair(False), init)
        carry = lax.fori_loop(n_far_pairs, n_trips // 2, make_pair(True), carry)
        carry = lax.cond(n_trips % 2 == 1,
                         lambda c: [attend(g, n_trips - 1, 0, True, c[g]) for g in groups],
                         lambda c: c, carry)
        return [acc[0:HEAD_DIM] * (1.0 / acc[HEAD_DIM:HEAD_DIM + 1]) for _, acc in carry]

    gt = gt_ref[...]

    def gate(g, j):
        return jnp.concatenate([gt[(g * GQA + r) * 3 + j:(g * GQA + r) * 3 + j + 1, :]
                                for r in range(GQA)], axis=1)

    qs = [query(g) for g in range(N_KV)]
    cmp_out, o_win = [], []
    for g in range(N_KV):
        cmp_out.append(compressed(g, qs[g]))
        o_win.append(window(g, qs[g]))
        select(g, cmp_out[g][1])
    for g in range(N_KV):
        scores(g, qs[g], 0, 0)
    o_sel = selected(qs)
    for g in range(N_KV):
        o = gate(g, 0) * cmp_out[g][0] + gate(g, 1) * o_sel[g] + gate(g, 2) * o_win[g]
        o_ref[:, g * gw:(g + 1) * gw] = jnp.concatenate(
            [o[:, r * Q_BLOCK:(r + 1) * Q_BLOCK].T for r in range(GQA)], axis=1).astype(BF16)


def _nsa_prompt(qt, gt, kc, vct, ks, vst, kw, vwt, cband, far, seld, wbias, cover_t, bsz, seq):
    nblk = seq // Q_BLOCK
    ncp = kc.shape[1]
    ns = cover_t.shape[0]
    cols = GQA * Q_BLOCK
    nsub = next(c for c in (4, 2, 1) if nblk % c == 0)
    full = lambda a: pl.BlockSpec(a.shape, lambda b, i: (0,) * a.ndim)
    in_specs = [
        pl.BlockSpec((D_B, Q_BLOCK), lambda b, i: (0, b * nblk + i)),
        pl.BlockSpec((GATE_ROWS, Q_BLOCK), lambda b, i: (0, b * nblk + i)),
        pl.BlockSpec((1, ncp, KV_W), lambda b, i: (b, 0, 0)),
        pl.BlockSpec((1, KV_W, ncp), lambda b, i: (b, 0, 0)),
        pl.BlockSpec((seq, KV_W), lambda b, i: (b, 0)),
        pl.BlockSpec((KV_W, seq), lambda b, i: (0, b)),
        pl.BlockSpec((seq, KV_W), lambda b, i: (b, 0)),
        pl.BlockSpec((KV_W, seq), lambda b, i: (0, b)),
        full(cband), full(far), full(seld),
        pl.BlockSpec((N_KV, 1) + wbias.shape[2:],
                     lambda b, i: (0, jnp.minimum(i, WIN_VARIANTS - 1), 0, 0)),
        full(cover_t),
    ]
    return pl.pallas_call(
        functools.partial(_nsa_prompt_kernel, nsub=nsub),
        grid=(bsz, nblk),
        in_specs=in_specs,
        out_specs=pl.BlockSpec((Q_BLOCK, D_B), lambda b, i: (b * nblk + i, 0)),
        out_shape=jax.ShapeDtypeStruct((bsz * seq, D_B), BF16),
        scratch_shapes=[pltpu.VMEM((N_KV, ncp, cols), F32), pltpu.VMEM((N_KV, ns, cols), F32),
                        pltpu.VMEM((N_KV, 2, nsub * CHUNK, cols), F32)],
        compiler_params=pltpu.CompilerParams(
            dimension_semantics=("parallel", "arbitrary"),
            vmem_limit_bytes=VMEM_LIMIT),
        name="nsa_prompt",
    )(qt, gt, kc, vct, ks, vst, kw, vwt, cband, far, seld, wbias, cover_t)


def _nsa_sample_kernel(tbl_ref, *refs, pgs, npg, past, ds):
    del tbl_ref
    page_refs = refs[:pgs]
    (qbd_ref, gt_ref, kc_ref, vct_ref, win_ref, new_ref, cbias_ref, wbias_ref, sfar_ref, slast_ref,
     snew_ref, cover_ref, rsum_ref, o_ref, msk_ref, m_ref, acc_ref, ocw_ref) = refs[pgs:]
    j = pl.program_id(1)
    qbd = qbd_ref[0]
    ncol = qbd.shape[1]
    qcols = ncol // (N_KV * GQA)
    g = gt_ref[0]

    @pl.when(j == 0)
    def _():
        p_c = _softmax_rows(_dot(kc_ref[0], qbd) + cbias_ref[...])
        o_c = _dot(vct_ref[0], p_c.astype(BF16))
        imp = _split_dot(_split_dot_l(cover_ref[...], p_c), rsum_ref[...])
        lane = lax.broadcasted_iota(jnp.int32, (1, ncol), 1)
        qpos = past + (lane % qcols) % ds
        msk_ref[...] = _select_mask(imp, qpos) + sfar_ref[...]
        kw = win_ref[0, :, 0:KV_W].astype(BF16)
        vw = win_ref[0, :, KV_W:KV_ROW].astype(BF16)
        p_w = _softmax_rows(_dot(kw, qbd) + wbias_ref[...])
        o_w = _dot_t(vw, p_w.astype(BF16))
        ocw_ref[...] = g[0:1, :] * o_c + g[2:3, :] * o_w
        m_ref[...] = jnp.full(m_ref.shape, -jnp.inf, F32)
        acc_ref[...] = jnp.zeros(acc_ref.shape, F32)

    last = j == pl.num_programs(1) - 1
    blocks, vts = [], []
    for k, pr in enumerate(page_refs):
        pidx = j * pgs + k
        kk = pr[0, 0:KV_W, :].T.astype(BF16)
        vts.append(pr[0, KV_W:KV_ROW, :].astype(BF16))
        s = _dot(kk, qbd)
        if k == pgs - 1:
            s = s + jnp.where(last, slast_ref[...], 0.0)
        for hb in range(2):
            blocks.append(s[hb * SEL_BLOCK:(hb + 1) * SEL_BLOCK, :]
                          + msk_ref[pl.ds(2 * pidx + hb, 1), :])
    top = blocks[0]
    for s in blocks[1:]:
        top = jnp.maximum(top, s)
    m = m_ref[...]
    m_new = jnp.maximum(m, jnp.max(top, axis=0, keepdims=True))
    p = jnp.concatenate([jnp.exp2(s - m_new).astype(BF16) for s in blocks], axis=0)
    vt = jnp.concatenate([jnp.concatenate(vts, axis=1), jnp.ones((16, pgs * PAGE_SIZE), BF16)], axis=0)
    acc_ref[...] = jnp.exp2(m - m_new) * acc_ref[...] + _dot(vt, p)
    m_ref[...] = m_new

    @pl.when(last)
    def _():
        kn = new_ref[0, :, 0:KV_W].astype(BF16)
        vn = new_ref[0, :, KV_W:KV_ROW].astype(BF16)
        s = _dot(kn, qbd) + snew_ref[...] + msk_ref[pl.ds(2 * npg, 1), :]
        m = m_ref[...]
        m_new = jnp.maximum(m, jnp.max(s, axis=0, keepdims=True))
        p = jnp.exp2(s - m_new).astype(BF16)
        acc = jnp.exp2(m - m_new) * acc_ref[...]
        num = acc[0:KV_W] + _dot_t(vn, p)
        den = acc[KV_W:KV_W + 1] + jnp.sum(p.astype(F32), axis=0, keepdims=True)
        o = ocw_ref[...] + g[1:2, :] * (num * (1.0 / den))
        for gi in range(N_KV):
            o_ref[0, gi] = o[gi * HEAD_DIM:(gi + 1) * HEAD_DIM, gi * GQA * qcols:(gi + 1) * GQA * qcols]


def _nsa_sample(table, pages, qbd, gt, kc, vct, win, new, cbias, wbias, sfar, slast, snew,
                cover_t, rsum, past, ds):
    nb, npg = table.shape
    pgs = min(64, npg)
    assert npg % pgs == 0
    ncol = qbd.shape[2]
    ns = cover_t.shape[0]

    def page_spec(k):
        return pl.BlockSpec((1, KV_ROW, PAGE_SIZE), lambda b, j, tbl: (tbl[b, j * pgs + k], 0, 0))

    per_b = lambda a: pl.BlockSpec((1,) + a.shape[1:], lambda b, j, tbl: (b,) + (0,) * (a.ndim - 1))
    full = lambda a: pl.BlockSpec(a.shape, lambda b, j, tbl: (0,) * a.ndim)
    grid_spec = pltpu.PrefetchScalarGridSpec(
        num_scalar_prefetch=1,
        grid=(nb, npg // pgs),
        in_specs=[page_spec(k) for k in range(pgs)]
        + [per_b(a) for a in (qbd, gt, kc, vct, win, new)]
        + [full(a) for a in (cbias, wbias, sfar, slast, snew, cover_t, rsum)],
        out_specs=pl.BlockSpec((1, N_KV, HEAD_DIM, ncol // N_KV), lambda b, j, tbl: (b, 0, 0, 0)),
        scratch_shapes=[pltpu.VMEM((ns, ncol), F32), pltpu.VMEM((1, ncol), F32),
                        pltpu.VMEM((KV_W + 16, ncol), F32), pltpu.VMEM((KV_W, ncol), F32)],
    )
    return pl.pallas_call(
        functools.partial(_nsa_sample_kernel, pgs=pgs, npg=npg, past=past, ds=ds),
        grid_spec=grid_spec,
        out_shape=jax.ShapeDtypeStruct((nb, N_KV, HEAD_DIM, ncol // N_KV), F32),
        compiler_params=pltpu.CompilerParams(dimension_semantics=("parallel", "arbitrary"),
                                             vmem_limit_bytes=VMEM_LIMIT),
        name="nsa_sample",
    )(table, *([pages] * pgs), qbd, gt, kc, vct, win, new, cbias, wbias, sfar, slast, snew,
      cover_t, rsum)


def _cover_t(nc, ns, nc_pad, ns_pad):
    c0 = np.arange(nc) * CMP_STRIDE
    s0 = np.arange(ns) * SEL_BLOCK
    m = (c0[None, :] < s0[:, None] + SEL_BLOCK) & (c0[None, :] + CMP_BLOCK > s0[:, None])
    out = np.zeros((ns_pad, nc_pad), np.float32)
    out[:ns, :nc] = m
    return jnp.asarray(out, BF16)


def _prep_weights(norm_g, w_in, ln_v_g, spatial_w, spatial_b, cmp_pe, cmp_w1, cmp_b1, cmp_w2, cmp_b2,
                  w_out, final_g):
    offs = np.cumsum((D_A, D_A, D_A, D_B, 2 * KV_W, 2 * KV_W, 2 * KV_W, 3 * N_HEADS, D_B))
    g0, g1 = int(offs[6]), int(offs[7])
    w_perm = jnp.concatenate(
        [w_in[:, :g0], w_in[:, g1:], w_in[:, g0:g1],
         jnp.zeros((D_MODEL, GATE_PAD - 3 * N_HEADS), w_in.dtype)], axis=1).astype(BF16)
    pavg = jnp.asarray(np.kron(np.eye(A_GROUPS), np.full((HEAD_DIM, HEAD_DIM), 1.0 / HEAD_DIM)), BF16)
    w1cat = jnp.concatenate([cmp_w1[:, s].reshape(2, CMP_FLAT, CMP_HIDDEN) for s in range(CMP_R)],
                            axis=2).astype(BF16)
    pe8 = jnp.concatenate([cmp_pe.reshape(2, CMP_R, CMP_FLAT),
                           jnp.zeros((2, 8 - CMP_R, CMP_FLAT), F32)], axis=1)
    return dict(
        norm_g=norm_g.reshape(1, D_MODEL), w_perm=w_perm, ln_g=ln_v_g.reshape(1, D_A), pavg=pavg,
        w1cat=w1cat, pe8=pe8, b1=cmp_b1.reshape(2, 1, CMP_HIDDEN), w2=cmp_w2.astype(BF16),
        b2=cmp_b2.reshape(2, 1, HEAD_DIM), wo=w_out.astype(BF16), fg=final_g.reshape(1, D_MODEL),
        spatial_w=spatial_w, spatial_b=spatial_b)


def _spatial_operands(spatial_w, spatial_b, n):
    reps = CHUNK // n
    w = jnp.tril(spatial_w[:, :n, :n])
    eye = jnp.eye(reps, dtype=w.dtype)
    wsp = jnp.einsum('ab,gts->gatbs', eye, w).reshape(A_GROUPS, CHUNK, CHUNK).astype(BF16)
    b = jnp.tile(spatial_b[:, :n].T, (reps, 1))
    bsp = jnp.repeat(b, HEAD_DIM, axis=1)
    return wsp, bsp


def _prompt_bias_buckets():
    ql = np.arange(Q_BLOCK)[None, :]
    cl = np.arange(16)[:, None]
    band = np.stack([ql - CMP_STRIDE * cl - (CMP_BLOCK - 1),
                     ql + 97 - CMP_STRIDE * cl])
    kl = np.arange(CHUNK)[:, None]
    seld = np.stack([np.full((CHUNK, Q_BLOCK), FAR_DIST), CHUNK + ql - kl, ql - kl])
    wl = np.arange(WINDOW + Q_BLOCK)[:, None]
    dw = np.stack([Q_BLOCK * v + ql - wl for v in range(WIN_VARIANTS)])
    dw = np.where(dw < WINDOW, dw, -1)
    far = np.full((8, Q_BLOCK), FAR_DIST)
    tile4 = lambda d: np.tile(_t5_bucket_np(d.reshape(-1, Q_BLOCK)), (1, GQA))
    return tile4(band), tile4(seld), tile4(dw), tile4(far)


def _sample_bias_buckets(past, ds, qpad, ncp, nwin_pad):
    ql = (np.arange(qpad) % ds)[None, :]
    c = np.arange(ncp)[:, None]
    dc = past + ql - (CMP_STRIDE * c + CMP_BLOCK - 1)
    wl = np.arange(nwin_pad)[:, None]
    dw = WINDOW + ql - wl
    dw = np.where((dw < WINDOW) & (wl < WINDOW + ds), dw, -1)
    kl = np.arange(PAGE_SIZE)[:, None]
    dlast = PAGE_SIZE + ql - kl
    nl = np.arange(16)[:, None]
    dnew = np.where(nl < ds, ql - nl, -1)
    far = np.full((8, qpad), FAR_DIST)
    tile4 = lambda d: np.tile(_t5_bucket_np(d), (1, GQA))
    return tile4(dc), tile4(dw), tile4(dlast), tile4(dnew), tile4(far)


def kernel(x_prompt, x_sample, cache_cmp_kv, cache_sel_kv, state_win_kv, page_table, norm_g, w_in,
           ln_v_g, spatial_w, spatial_b, cmp_pe, cmp_w1, cmp_b1, cmp_w2, cmp_b2, rel_table, w_out,
           final_g):
    depth = norm_g.shape[0]
    assert depth == 1
    bsz, seq = x_prompt.shape[:2]
    db, ds = x_sample.shape[:2]
    npg = page_table.shape[1]
    past = npg * PAGE_SIZE
    win_buf = state_win_kv.shape[2]
    assert win_buf == WINDOW and past >= WINDOW and seq % Q_BLOCK == 0 and seq >= WINDOW + Q_BLOCK
    assert CHUNK % ds == 0 and ds <= 8 and (db * ds) % CHUNK == 0
    nblk = seq // Q_BLOCK
    l = 0
    wts = _prep_weights(norm_g[l], w_in[l], ln_v_g[l], spatial_w[l], spatial_b[l], cmp_pe[l],
                        cmp_w1[l], cmp_b1[l], cmp_w2[l], cmp_b2[l], w_out[l], final_g)
    inproj = lambda x, nb: _inproj(x, nb, wts['norm_g'], wts['w_perm'], wts['ln_g'], wts['pavg'])
    compress = lambda tbl, pages: _compress(tbl, pages, wts['w1cat'], wts['pe8'], wts['b1'],
                                            wts['w2'], wts['b2'])
    kv_out = lambda a: a.reshape(a.shape[0], 2, N_KV, HEAD_DIM, a.shape[2]).transpose(0, 4, 1, 2, 3)[None]
    pages_t = lambda c: c.transpose(0, 2, 3, 4, 1).reshape(c.shape[0], KV_ROW, PAGE_SIZE)

    n_p = bsz * seq
    xp = x_prompt.reshape(n_p, D_MODEL)
    ua, _, vnb, zas, zbs, kvc, kvs, kvw, qt, gt, ksk, vst, kwk, vwt = inproj(xp, bsz)
    kc, vct = compress(None, kvc)
    ncp = seq // CMP_STRIDE
    cover_p = _cover_t(ncp - CMP_R + 1, seq // SEL_BLOCK, ncp, seq // SEL_BLOCK)
    band_b, seld_b, win_b, far_b = _prompt_bias_buckets()
    cols = GQA * Q_BLOCK
    cband = _bias_tiles(rel_table, band_b, Q_BLOCK).reshape(N_KV, 2, 16, cols)
    seld = _bias_tiles(rel_table, seld_b, Q_BLOCK, rel_far=True).reshape(N_KV, 3, CHUNK, cols)
    wbias = _bias_tiles(rel_table, win_b, Q_BLOCK).reshape(N_KV, WIN_VARIANTS, WINDOW + Q_BLOCK, cols)
    far = _bias_tiles(rel_table, far_b, Q_BLOCK)[:, 0:1]
    ob = _nsa_prompt(qt, gt, kc, vct, ksk, vst, kwk, vwt, cband, far, seld, wbias, cover_p, bsz, seq)
    wsp, bsp = _spatial_operands(wts['spatial_w'], wts['spatial_b'], CHUNK)
    y_prompt = _mixout(xp, ua, vnb, zas, ob, zbs, wsp, bsp, wts['wo'], wts['fg']).reshape(bsz, seq, D_MODEL)
    new_cmp_p = kv_out(kvc)
    new_sel_p = kv_out(kvs)
    new_win_p = kv_out(kvw[:, :, seq - win_buf:])

    n_s = db * ds
    xs = x_sample.reshape(n_s, D_MODEL)
    ua, vn, vnb, zas, zbs, kvc, kvs, kvw, qt, gt, _, _, _, _ = inproj(xs, 1)
    kvc, kvs, kvw = (a[0].T for a in (kvc, kvs, kvw))
    kc, vct = compress(page_table, pages_t(cache_cmp_kv[l]))
    ncs = past // CMP_STRIDE
    t_all = past + ds
    nss = -(-t_all // SEL_BLOCK)
    nss_pad = -(-nss // 8) * 8
    cover_s = _cover_t(t_all // CMP_STRIDE - CMP_R + 1, nss, ncs, nss_pad)
    qpad = 16
    ncol = N_KV * GQA * qpad
    nwin_pad = -(-(win_buf + ds) // 16) * 16
    dc_b, dw_b, dl_b, dn_b, far_b = _sample_bias_buckets(past, ds, qpad, ncs, nwin_pad)
    both = lambda t: jnp.concatenate([t[0], t[1]], axis=1)
    cbias_s = both(_bias_tiles(rel_table, dc_b, qpad))
    wbias_s = both(_bias_tiles(rel_table, dw_b, qpad))
    slast = both(_bias_tiles(rel_table, dl_b, qpad, rel_far=True))
    snew = both(_bias_tiles(rel_table, dn_b, qpad, rel_far=True))
    sfar = both(_bias_tiles(rel_table, far_b, qpad))[0:1]

    qg = qt.reshape(N_KV, GQA, HEAD_DIM, db, ds).transpose(3, 0, 2, 1, 4)
    qg = jnp.tile(qg, (1, 1, 1, 1, qpad // ds)).reshape(db, N_KV, HEAD_DIM, GQA * qpad)
    zq = jnp.zeros_like(qg[:, 0])
    qbd = jnp.concatenate([jnp.concatenate([qg[:, 0], zq], axis=2),
                           jnp.concatenate([zq, qg[:, 1]], axis=2)], axis=1)
    gts = gt[:3 * N_HEADS].reshape(N_KV, GQA, 3, db, ds).transpose(3, 2, 0, 1, 4)
    gts = jnp.tile(gts, (1, 1, 1, 1, qpad // ds)).reshape(db, 3, ncol)
    win = jnp.concatenate([state_win_kv[l].reshape(db, win_buf, KV_ROW),
                           kvw.reshape(db, ds, KV_ROW)], axis=1)
    win_pad = jnp.pad(win, ((0, 0), (0, nwin_pad - win_buf - ds), (0, 0)))
    new_pad = jnp.pad(kvs.reshape(db, ds, KV_ROW), ((0, 0), (0, 16 - ds), (0, 0)))
    rsum = jnp.asarray(np.kron(np.eye(N_KV), np.kron(np.ones((GQA, GQA)), np.eye(qpad))), BF16)
    o_s = _nsa_sample(page_table, pages_t(cache_sel_kv[l]), qbd, gts,
                      kc, vct, win_pad, new_pad, cbias_s, wbias_s, sfar, slast, snew,
                      cover_s, rsum, past, ds)
    ob = (o_s.reshape(db, N_KV, HEAD_DIM, GQA, qpad)[..., :ds].transpose(0, 4, 1, 3, 2)
          .reshape(n_s, D_B))
    wsp, bsp = _spatial_operands(wts['spatial_w'], wts['spatial_b'], ds)
    y_sample = _mixout(xs, ua, vnb, zas, ob, zbs, wsp, bsp, wts['wo'], wts['fg']).reshape(db, ds, D_MODEL)
    kv5 = lambda a: a.reshape(1, db, ds, 2, N_KV, HEAD_DIM)
    new_cmp_s = kv5(kvc)
    new_sel_s = kv5(kvs)
    new_win_s = win[:, ds:].reshape(1, db, win_buf, 2, N_KV, HEAD_DIM)
    new_chunk_v = vn.reshape(1, db, ds, D_A)
    return (y_prompt, y_sample, new_cmp_p, new_sel_p, new_win_p, new_cmp_s, new_sel_s, new_win_s,
            new_chunk_v)
```

```python
import functools
import math

import numpy as np
import jax
import jax.numpy as jnp
from jax import lax
from jax.experimental import pallas as pl
from jax.experimental.pallas import tpu as pltpu

F32 = jnp.float32
BF16 = jnp.bfloat16

D_MODEL = 1024
HEAD_DIM = 64
D_A = 512
D_B = 512
A_GROUPS = D_A // HEAD_DIM
CHUNK = 128
N_HEADS = D_B // HEAD_DIM
N_KV = 2
GQA = N_HEADS // N_KV
KV_W = N_KV * HEAD_DIM
KV_ROW = 2 * KV_W
CMP_STRIDE = 16
CMP_BLOCK = 32
CMP_R = CMP_BLOCK // CMP_STRIDE
CMP_HIDDEN = 256
CMP_FLAT = CMP_STRIDE * HEAD_DIM
SEL_BLOCK = 64
N_SEL = 16
WINDOW = 512
N_BUCKETS = 32
MAX_DISTANCE = 128
Q_BLOCK = 128
PAGE_SIZE = 128
RMS_EPS = 1e-6
LN_EPS = 1e-5
NEG = -1e30
FORCE_BONUS = 1e6
LOG2E = 1.4426950408889634
Q_SCALE = HEAD_DIM ** -0.5 * LOG2E
FAR_DIST = 1 << 20
WIN_VARIANTS = WINDOW // Q_BLOCK + 1

_OFF_U, _OFF_V, _OFF_ZA, _OFF_Q = 0, 512, 1024, 1536
_OFF_KVC, _OFF_KVS, _OFF_KVW, _OFF_ZB, _OFF_G = 2048, 2304, 2560, 2816, 3328
D_IN_PAD = 3456
GATE_PAD = 128
GATE_ROWS = 32

VMEM_LIMIT = 52 * 1024 * 1024


def _gelu(x):
    return x * (0.5 * (1.0 + jnp.tanh(0.7978845608028654 * (x + 0.044715 * (x * x * x)))))


def _sigmoid(x):
    return 1.0 / (1.0 + jnp.exp(-x))


def _dot(a, b):
    return jnp.dot(a, b, preferred_element_type=F32)


def _dot_t(a, b):
    return lax.dot_general(a, b, (((0,), (0,)), ((), ())), preferred_element_type=F32)


def _dot_nt(a, b):
    return lax.dot_general(a, b, (((1,), (1,)), ((), ())), preferred_element_type=F32)


def _split_dot(a, b):
    hi = a.astype(BF16)
    lo = (a - hi.astype(F32)).astype(BF16)
    return _dot(hi, b) + _dot(lo, b)


def _split_dot_l(a, b):
    hi = b.astype(BF16)
    lo = (b - hi.astype(F32)).astype(BF16)
    return _dot(a, hi) + _dot(a, lo)


def _inproj_kernel(x_ref, ng_ref, w_ref, lng_ref, pavg_ref,
                   ua_ref, vn_ref, vnb_ref, zas_ref, zbs_ref, kvc_ref, kvs_ref, kvw_ref,
                   qt_ref, gt_ref, ksk_ref, vst_ref, kwk_ref, vwt_ref):
    x = x_ref[...]
    ms = jnp.mean(x * x, axis=-1, keepdims=True)
    h = (x * lax.rsqrt(ms + RMS_EPS) * ng_ref[...]).astype(BF16)

    def proj(a, b):
        return _dot(h, w_ref[:, a:b])

    ua_ref[...] = _gelu(proj(_OFF_U, _OFF_V)).astype(BF16)
    v = _gelu(proj(_OFF_V, _OFF_ZA))
    mu = _dot(v.astype(BF16), pavg_ref[...])
    d = v - mu
    var = _dot((d * d).astype(BF16), pavg_ref[...])
    vn = d * lax.rsqrt(var + LN_EPS) * lng_ref[...]
    vn_ref[...] = vn
    vnb_ref[...] = vn.astype(BF16)
    za = proj(_OFF_ZA, _OFF_Q)
    zas_ref[...] = (za * _sigmoid(za)).astype(BF16)
    zb = proj(_OFF_ZB, _OFF_G)
    zbs_ref[...] = (zb * _sigmoid(zb)).astype(BF16)
    qt_ref[...] = (proj(_OFF_Q, _OFF_KVC) * Q_SCALE).T.astype(BF16)
    gt_ref[...] = _sigmoid(proj(_OFF_G, D_IN_PAD)).T[0:GATE_ROWS, :]
    kvc_ref[0] = proj(_OFF_KVC, _OFF_KVS).T
    kvs = proj(_OFF_KVS, _OFF_KVW)
    kvs_t = kvs.T
    kvs_ref[0] = kvs_t
    ksk_ref[...] = kvs[:, 0:KV_W].astype(BF16)
    vst_ref[...] = kvs_t[KV_W:KV_ROW, :].astype(BF16)
    kvw = proj(_OFF_KVW, _OFF_ZB)
    kvw_t = kvw.T
    kvw_ref[0] = kvw_t
    kwk_ref[...] = kvw[:, 0:KV_W].astype(BF16)
    vwt_ref[...] = kvw_t[KV_W:KV_ROW, :].astype(BF16)


def _inproj(x, nb, norm_g, w_perm, ln_g, pavg):
    n = x.shape[0]
    s = n // nb
    tm = min(512, s)
    assert s % tm == 0
    per = s // tm
    row = lambda w: pl.BlockSpec((tm, w), lambda i: (i, 0))
    col = lambda h: pl.BlockSpec((h, tm), lambda i: (0, i))
    kvt = pl.BlockSpec((1, KV_ROW, tm), lambda i: (i // per, 0, i % per))
    full = lambda a: pl.BlockSpec(a.shape, lambda i: (0,) * a.ndim)
    rows = [(D_A, BF16), (D_A, F32), (D_A, BF16), (D_A, BF16), (D_B, BF16)]
    sds = lambda shape, dt: jax.ShapeDtypeStruct(shape, dt)
    out_specs = ([row(w) for w, _ in rows] + [kvt, kvt, kvt]
                 + [col(D_B), col(GATE_ROWS), row(KV_W), col(KV_W), row(KV_W), col(KV_W)])
    out_shape = ([sds((n, w), dt) for w, dt in rows] + [sds((nb, KV_ROW, s), F32)] * 3
                 + [sds((D_B, n), BF16), sds((GATE_ROWS, n), F32), sds((n, KV_W), BF16),
                    sds((KV_W, n), BF16), sds((n, KV_W), BF16), sds((KV_W, n), BF16)])
    return pl.pallas_call(
        _inproj_kernel,
        grid=(n // tm,),
        in_specs=[row(D_MODEL), full(norm_g), full(w_perm), full(ln_g), full(pavg)],
        out_specs=out_specs,
        out_shape=out_shape,
        compiler_params=pltpu.CompilerParams(dimension_semantics=("parallel",),
                                             vmem_limit_bytes=VMEM_LIMIT),
        name="inproj",
    )(x, norm_g, w_perm, ln_g, pavg)


def _mixout_kernel(x_ref, ua_ref, vn_ref, zas_ref, ob_ref, zbs_ref, wsp_ref, bsp_ref, wo_ref, fg_ref,
                   y_ref, s_ref):
    tm = x_ref.shape[0]
    for c in range(tm // CHUNK):
        rows = slice(c * CHUNK, (c + 1) * CHUNK)
        vc = vn_ref[rows, :]
        for g in range(A_GROUPS):
            cols = slice(g * HEAD_DIM, (g + 1) * HEAD_DIM)
            s_ref[rows, cols] = _dot(wsp_ref[g], vc[:, cols])
        s_ref[rows, :] = s_ref[rows, :] + bsp_ref[...]
    mix_a = (ua_ref[...].astype(F32) * s_ref[...] * zas_ref[...].astype(F32)).astype(BF16)
    mix_b = (ob_ref[...].astype(F32) * zbs_ref[...].astype(F32)).astype(BF16)
    y = x_ref[...] + _dot(mix_a, wo_ref[0:D_A, :]) + _dot(mix_b, wo_ref[D_A:D_A + D_B, :])
    ms = jnp.mean(y * y, axis=-1, keepdims=True)
    y_ref[...] = y * lax.rsqrt(ms + RMS_EPS) * fg_ref[...]


def _mixout(x, ua, vn, zas, ob, zbs, wsp, bsp, wo, fg):
    n = x.shape[0]
    tm = min(512, n)
    row = lambda w: pl.BlockSpec((tm, w), lambda i: (i, 0))
    full = lambda a: pl.BlockSpec(a.shape, lambda i: (0,) * a.ndim)
    return pl.pallas_call(
        _mixout_kernel,
        grid=(n // tm,),
        in_specs=[row(D_MODEL), row(D_A), row(D_A), row(D_A), row(D_B), row(D_B),
                  full(wsp), full(bsp), full(wo), full(fg)],
        out_specs=row(D_MODEL),
        out_shape=jax.ShapeDtypeStruct((n, D_MODEL), F32),
        scratch_shapes=[pltpu.VMEM((tm, D_A), F32)],
        compiler_params=pltpu.CompilerParams(dimension_semantics=("parallel",),
                                             vmem_limit_bytes=VMEM_LIMIT),
        name="mixout",
    )(x, ua, vn, zas, ob, zbs, wsp, bsp, wo, fg)


def _t5_bucket_np(dist):
    dist = np.asarray(dist, np.int64)
    n = np.maximum(dist, 0)
    max_exact = N_BUCKETS // 2
    nf = np.maximum(n, max_exact).astype(np.float64)
    large = max_exact + (np.log(nf / max_exact) / math.log(MAX_DISTANCE / max_exact)
                         * (N_BUCKETS - max_exact)).astype(np.int64)
    b = np.where(n < max_exact, n, np.minimum(large, N_BUCKETS - 1))
    return np.where(dist < 0, -1, b).astype(np.int32)


def _bias_kernel(tbl_ref, bkt_ref, out_ref, *, qb, rel_far):
    b = bkt_ref[...]
    grp = lax.broadcasted_iota(jnp.int32, (1, b.shape[1]), 1) // qb

    def head_row(k, g):
        row = jnp.zeros(grp.shape, F32)
        for r in range(GQA):
            row = jnp.where(grp == r, tbl_ref[k * N_HEADS + g * GQA + r], row)
        return row * LOG2E

    for g in range(N_KV):
        acc = jnp.full(b.shape, NEG, F32)
        base = head_row(N_BUCKETS - 1, g) if rel_far else None
        for k in range(N_BUCKETS):
            row = head_row(k, g)
            if rel_far:
                row = row - base
            acc = jnp.where(b == k, row, acc)
        out_ref[g] = acc


def _bias_tiles(rel_table, bkt, qb, rel_far=False):
    r, c = bkt.shape
    rb = r
    for cand in (512, 256, 128, 64, 32, 16, 8):
        if r % cand == 0:
            rb = cand
            break
    return pl.pallas_call(
        functools.partial(_bias_kernel, qb=qb, rel_far=rel_far),
        grid=(r // rb,),
        in_specs=[pl.BlockSpec(memory_space=pltpu.SMEM), pl.BlockSpec((rb, c), lambda i: (i, 0))],
        out_specs=pl.BlockSpec((N_KV, rb, c), lambda i: (0, i, 0)),
        out_shape=jax.ShapeDtypeStruct((N_KV, r, c), F32),
        compiler_params=pltpu.CompilerParams(dimension_semantics=("parallel",)),
        name="bias_tiles",
    )(rel_table.reshape(-1), jnp.asarray(bkt))


def _compress_kernel(tbl_ref, *refs, pg):
    del tbl_ref
    page_refs = refs[:pg + 1]
    perm_ref, w1_ref, pe_ref, b1_ref, w2_ref, b2_ref, kc_ref, vct_ref, x_ref = refs[pg + 1:]
    mp = (pg + 1) * 8
    nrow = pg * 8
    left = lax.broadcasted_iota(jnp.int32, (8, 128), 1) < HEAD_DIM
    perm = perm_ref[...]
    for k, pr in enumerate(page_refs):
        tok = _dot_nt(perm, pr[0].astype(BF16))
        for kv in range(2):
            for t in range(CMP_STRIDE // 2):
                e = tok[16 * t:16 * t + 8, kv * KV_W:(kv + 1) * KV_W]
                o = tok[16 * t + 8:16 * t + 16, kv * KV_W:(kv + 1) * KV_W]
                sw = pltpu.roll(jnp.where(left, o, e), HEAD_DIM, 1)
                x_ref[kv, 8 * k:8 * k + 8, 128 * t:128 * t + 128] = jnp.where(left, e, sw)
                x_ref[kv, mp + 8 * k:mp + 8 * k + 8, 128 * t:128 * t + 128] = jnp.where(left, sw, o)
    outs = []
    for kv in range(2):
        x_ref[kv, 2 * mp:2 * mp + 8, :] = pe_ref[kv]
        p = _dot(x_ref[kv].astype(BF16), w1_ref[kv])
        hc = (b1_ref[kv] + p[2 * mp:2 * mp + 1, 0:CMP_HIDDEN]
              + p[2 * mp + 1:2 * mp + 2, CMP_HIDDEN:2 * CMP_HIDDEN])
        per_g = []
        for g in range(N_KV):
            base = g * mp
            h = (p[base:base + nrow, 0:CMP_HIDDEN]
                 + p[base + 1:base + nrow + 1, CMP_HIDDEN:2 * CMP_HIDDEN] + hc)
            per_g.append(_dot(_gelu(h).astype(BF16), w2_ref[kv]) + b2_ref[kv])
        outs.append(jnp.concatenate(per_g, axis=1))
    kc_ref[0] = outs[0].astype(BF16)
    vct_ref[0] = outs[1].T.astype(BF16)


def _compress(table, pages, w1cat, pe8, b1, w2, b2):
    if table is None:
        nb, npg = pages.shape[0], pages.shape[2] // PAGE_SIZE
        table = jnp.zeros((1, 1), jnp.int32)
        index = lambda b, idx, tbl: (b, 0, idx)
    else:
        nb, npg = table.shape
        index = lambda b, idx, tbl: (tbl[b, idx], 0, 0)
    pg = min(64, npg)
    assert npg % pg == 0
    mp = (pg + 1) * 8

    def page_spec(k):
        return pl.BlockSpec(
            (1, KV_ROW, PAGE_SIZE),
            lambda b, j, tbl: index(b, jnp.minimum(j * pg + k, npg - 1), tbl))

    tok = np.arange(PAGE_SIZE)
    perm_np = np.zeros((PAGE_SIZE, PAGE_SIZE), np.float32)
    perm_np[(tok % CMP_STRIDE) * (PAGE_SIZE // CMP_STRIDE) + tok // CMP_STRIDE, tok] = 1.0
    perm = jnp.asarray(perm_np, BF16)
    full = lambda a: pl.BlockSpec(a.shape, lambda b, j, tbl: (0,) * a.ndim)
    grid_spec = pltpu.PrefetchScalarGridSpec(
        num_scalar_prefetch=1,
        grid=(nb, npg // pg),
        in_specs=[page_spec(k) for k in range(pg + 1)]
        + [full(a) for a in (perm, w1cat, pe8, b1, w2, b2)],
        out_specs=[pl.BlockSpec((1, pg * 8, KV_W), lambda b, j, tbl: (b, j, 0)),
                   pl.BlockSpec((1, KV_W, pg * 8), lambda b, j, tbl: (b, 0, j))],
        scratch_shapes=[pltpu.VMEM((2, 2 * mp + 8, CMP_FLAT), F32)],
    )
    return pl.pallas_call(
        functools.partial(_compress_kernel, pg=pg),
        grid_spec=grid_spec,
        out_shape=[jax.ShapeDtypeStruct((nb, npg * 8, KV_W), BF16),
                   jax.ShapeDtypeStruct((nb, KV_W, npg * 8), BF16)],
        compiler_params=pltpu.CompilerParams(dimension_semantics=("parallel", "parallel"),
                                             vmem_limit_bytes=VMEM_LIMIT),
        name="compress",
    )(table, *([pages] * (pg + 1)), perm, w1cat, pe8, b1, w2, b2)


def _topk_rows(imp, n_sel):
    ns = imp.shape[0]
    blk = lax.broadcasted_iota(jnp.int32, imp.shape, 0).astype(F32)
    for _ in range(n_sel):
        mx = jnp.max(imp, axis=0, keepdims=True)
        idx = jnp.min(jnp.where(imp == mx, blk, float(ns)), axis=0, keepdims=True)
        imp = jnp.where(blk == idx, -jnp.inf, imp)
    return imp == -jnp.inf


def _select_mask(imp, qpos):
    blk = lax.broadcasted_iota(jnp.int32, imp.shape, 0)
    cur = qpos // SEL_BLOCK
    forced = (blk == 0) | (blk == cur) | (blk == cur - 1)
    valid = blk * SEL_BLOCK <= qpos
    imp = imp + jnp.where(forced, FORCE_BONUS, 0.0)
    imp = jnp.where(valid, imp, NEG)
    sel = _topk_rows(imp, N_SEL)
    return jnp.where(sel & valid, 0.0, NEG)


def _softmax_rows(s):
    m = jnp.max(s, axis=0, keepdims=True)
    e = jnp.exp2(s - m)
    inv = jnp.where(m > NEG / 2, 1.0 / jnp.sum(e, axis=0, keepdims=True), 0.0)
    return e * inv


def _online_update(s, v_dot, m, l, acc):
    m_new = jnp.maximum(m, jnp.max(s, axis=0, keepdims=True))
    alpha = jnp.exp2(m - m_new)
    p = jnp.exp2(s - m_new)
    l = alpha * l + jnp.sum(p, axis=0, keepdims=True)
    acc = alpha * acc + v_dot(p.astype(BF16))
    return m_new, l, acc


def _nsa_prompt_kernel(qt_ref, gt_ref, kc_ref, vct_ref, ks_ref, vst_ref, kw_ref, vwt_ref,
                       cband_ref, far_ref, seld_ref, wbias_ref, cover_ref,
                       o_ref, bias_ref, msk_ref, s_ref, *, nsub):
    i = pl.program_id(1)
    cols = GQA * Q_BLOCK
    ncp = kc_ref.shape[1]
    gw = GQA * HEAD_DIM
    qpos = i * Q_BLOCK + lax.broadcasted_iota(jnp.int32, (1, Q_BLOCK), 1)
    n_trips = i // nsub + 1
    tk = nsub * CHUNK
    c0 = pl.multiple_of(jnp.maximum(i * 8 - 8, 0), 8)
    crow = lax.broadcasted_iota(jnp.int32, (ncp, 1), 0)
    woff = pl.multiple_of(jnp.maximum(i * Q_BLOCK - WINDOW, 0), Q_BLOCK)
    nwk = WINDOW + Q_BLOCK

    def query(g):
        qblk = qt_ref[g * gw:(g + 1) * gw, :]
        q64 = jnp.concatenate([qblk[r * HEAD_DIM:(r + 1) * HEAD_DIM, :] for r in range(GQA)], axis=1)
        zero = jnp.zeros_like(q64)
        return jnp.concatenate([q64, zero] if g == 0 else [zero, q64], axis=0)

    def compressed(g, q):
        far = far_ref[g]
        bias_ref[g] = jnp.where(crow < c0, far, NEG)
        bias_ref[g, pl.ds(c0, 16), :] = cband_ref[g, jnp.minimum(i, 1)]
        s_c = _dot(kc_ref[0], q) + bias_ref[g]
        m_c = jnp.max(s_c, axis=0, keepdims=True)
        e_c = jnp.exp2(s_c - m_c).astype(BF16)
        vc = jnp.concatenate([vct_ref[0, g * HEAD_DIM:(g + 1) * HEAD_DIM, :],
                              jnp.ones((16, ncp), BF16)], axis=0)
        a_c = _dot(vc, e_c)
        inv = jnp.where(m_c > NEG / 2, 1.0 / a_c[HEAD_DIM:HEAD_DIM + 1], 0.0)
        o_c = a_c[0:HEAD_DIM] * inv
        w = _dot(cover_ref[...], e_c) * inv
        imp = (w[:, 0:Q_BLOCK] + w[:, Q_BLOCK:2 * Q_BLOCK]
               + w[:, 2 * Q_BLOCK:3 * Q_BLOCK] + w[:, 3 * Q_BLOCK:4 * Q_BLOCK])
        return o_c, imp

    def window(g, q):
        s_w = _dot(kw_ref[pl.ds(woff, nwk), :], q) + wbias_ref[g, 0]
        e_w = jnp.exp2(s_w - jnp.max(s_w, axis=0, keepdims=True)).astype(BF16)
        vw = jnp.concatenate([vwt_ref[g * HEAD_DIM:(g + 1) * HEAD_DIM, pl.ds(woff, nwk)],
                              jnp.ones((16, nwk), BF16)], axis=0)
        a_w = _dot(vw, e_w)
        return a_w[0:HEAD_DIM] * (1.0 / a_w[HEAD_DIM:HEAD_DIM + 1])

    def select(g, imp):
        mask = _select_mask(imp, qpos)
        msk_ref[g] = jnp.concatenate([mask] * GQA, axis=1) + far_ref[g]

    def scores(g, q, t, slot):
        t = jnp.minimum(t, n_trips - 1)
        koff = pl.multiple_of(t * tk, tk)
        s = _dot(ks_ref[pl.ds(koff, tk), :], q)
        for j in range(2 * nsub):
            blk = slice(j * SEL_BLOCK, (j + 1) * SEL_BLOCK)
            s_ref[g, slot, blk, :] = s[blk] + msk_ref[g, pl.ds(2 * nsub * t + j, 1), :]

    def attend(g, t, slot, near, carry):
        s = s_ref[g, slot]
        if near:
            s = s + jnp.concatenate(
                [seld_ref[g, jnp.clip(t * nsub + u - i + 2, 0, 2)] for u in range(nsub)], axis=0)
        koff = pl.multiple_of(t * tk, tk)
        vt = jnp.concatenate([vst_ref[g * HEAD_DIM:(g + 1) * HEAD_DIM, pl.ds(koff, tk)],
                              jnp.ones((16, tk), BF16)], axis=0)
        m, acc = carry
        m_new = jnp.maximum(m, jnp.max(s, axis=0, keepdims=True))
        p = jnp.exp2(s - m_new).astype(BF16)
        return m_new, jnp.exp2(m - m_new) * acc + _dot(vt, p)

    def selected(qs):
        groups = range(N_KV)

        def make_pair(near):
            def pair(tt, carry):
                for g in groups:
                    scores(g, qs[g], 2 * tt + 1, 1)
                carry = [attend(g, 2 * tt, 0, near, carry[g]) for g in groups]
                for g in groups:
                    scores(g, qs[g], 2 * tt + 2, 0)
                return [attend(g, 2 * tt + 1, 1, near, carry[g]) for g in groups]
            return pair

        n_far_pairs = (jnp.maximum(i - 1, 0) // nsub) // 2
        init = [(jnp.full((1, cols), -jnp.inf, F32), jnp.zeros((HEAD_DIM + 16, cols), F32))
                for _ in groups]
        carry = lax.fori_loop(0, n_far_pairs, make_pair(False), init)
        carry = lax.fori_loop(n_far_pairs, n_trips // 2, make_pair(True), carry)
        carry = lax.cond(n_trips % 2 == 1,
                         lambda c: [attend(g, n_trips - 1, 0, True, c[g]) for g in groups],
                         lambda c: c, carry)
        return [acc[0:HEAD_DIM] * (1.0 / acc[HEAD_DIM:HEAD_DIM + 1]) for _, acc in carry]

    gt = gt_ref[...]

    def gate(g, j):
        return jnp.concatenate([gt[(g * GQA + r) * 3 + j:(g * GQA + r) * 3 + j + 1, :]
                                for r in range(GQA)], axis=1)

    qs = [query(g) for g in range(N_KV)]
    cmp_out, o_win = [], []
    for g in range(N_KV):
        cmp_out.append(compressed(g, qs[g]))
        o_win.append(window(g, qs[g]))
        select(g, cmp_out[g][1])
    for g in range(N_KV):
        scores(g, qs[g], 0, 0)
    o_sel = selected(qs)
    for g in range(N_KV):
        o = gate(g, 0) * cmp_out[g][0] + gate(g, 1) * o_sel[g] + gate(g, 2) * o_win[g]
        o_ref[:, g * gw:(g + 1) * gw] = jnp.concatenate(
            [o[:, r * Q_BLOCK:(r + 1) * Q_BLOCK].T for r in range(GQA)], axis=1).astype(BF16)


def _nsa_prompt(qt, gt, kc, vct, ks, vst, kw, vwt, cband, far, seld, wbias, cover_t, bsz, seq):
    nblk = seq // Q_BLOCK
    ncp = kc.shape[1]
    ns = cover_t.shape[0]
    cols = GQA * Q_BLOCK
    nsub = next(c for c in (4, 2, 1) if nblk % c == 0)
    full = lambda a: pl.BlockSpec(a.shape, lambda b, i: (0,) * a.ndim)
    in_specs = [
        pl.BlockSpec((D_B, Q_BLOCK), lambda b, i: (0, b * nblk + i)),
        pl.BlockSpec((GATE_ROWS, Q_BLOCK), lambda b, i: (0, b * nblk + i)),
        pl.BlockSpec((1, ncp, KV_W), lambda b, i: (b, 0, 0)),
        pl.BlockSpec((1, KV_W, ncp), lambda b, i: (b, 0, 0)),
        pl.BlockSpec((seq, KV_W), lambda b, i: (b, 0)),
        pl.BlockSpec((KV_W, seq), lambda b, i: (0, b)),
        pl.BlockSpec((seq, KV_W), lambda b, i: (b, 0)),
        pl.BlockSpec((KV_W, seq), lambda b, i: (0, b)),
        full(cband), full(far), full(seld),
        pl.BlockSpec((N_KV, 1) + wbias.shape[2:],
                     lambda b, i: (0, jnp.minimum(i, WIN_VARIANTS - 1), 0, 0)),
        full(cover_t),
    ]
    return pl.pallas_call(
        functools.partial(_nsa_prompt_kernel, nsub=nsub),
        grid=(bsz, nblk),
        in_specs=in_specs,
        out_specs=pl.BlockSpec((Q_BLOCK, D_B), lambda b, i: (b * nblk + i, 0)),
        out_shape=jax.ShapeDtypeStruct((bsz * seq, D_B), BF16),
        scratch_shapes=[pltpu.VMEM((N_KV, ncp, cols), F32), pltpu.VMEM((N_KV, ns, cols), F32),
                        pltpu.VMEM((N_KV, 2, nsub * CHUNK, cols), F32)],
        compiler_params=pltpu.CompilerParams(
            dimension_semantics=("parallel", "arbitrary"),
            vmem_limit_bytes=VMEM_LIMIT),
        name="nsa_prompt",
    )(qt, gt, kc, vct, ks, vst, kw, vwt, cband, far, seld, wbias, cover_t)


def _nsa_sample_kernel(tbl_ref, *refs, pgs, npg, past, ds):
    del tbl_ref
    page_refs = refs[:pgs]
    (qbd_ref, gt_ref, kc_ref, vct_ref, win_ref, new_ref, cbias_ref, wbias_ref, sfar_ref, slast_ref,
     snew_ref, cover_ref, rsum_ref, o_ref, msk_ref, m_ref, acc_ref, ocw_ref) = refs[pgs:]
    j = pl.program_id(1)
    qbd = qbd_ref[0]
    ncol = qbd.shape[1]
    qcols = ncol // (N_KV * GQA)
    g = gt_ref[0]

    @pl.when(j == 0)
    def _():
        p_c = _softmax_rows(_dot(kc_ref[0], qbd) + cbias_ref[...])
        o_c = _dot(vct_ref[0], p_c.astype(BF16))
        imp = _split_dot(_split_dot_l(cover_ref[...], p_c), rsum_ref[...])
        lane = lax.broadcasted_iota(jnp.int32, (1, ncol), 1)
        qpos = past + (lane % qcols) % ds
        msk_ref[...] = _select_mask(imp, qpos) + sfar_ref[...]
        kw = win_ref[0, :, 0:KV_W].astype(BF16)
        vw = win_ref[0, :, KV_W:KV_ROW].astype(BF16)
        p_w = _softmax_rows(_dot(kw, qbd) + wbias_ref[...])
        o_w = _dot_t(vw, p_w.astype(BF16))
        ocw_ref[...] = g[0:1, :] * o_c + g[2:3, :] * o_w
        m_ref[...] = jnp.full(m_ref.shape, -jnp.inf, F32)
        acc_ref[...] = jnp.zeros(acc_ref.shape, F32)

    last = j == pl.num_programs(1) - 1
    blocks, vts = [], []
    for k, pr in enumerate(page_refs):
        pidx = j * pgs + k
        kk = pr[0, 0:KV_W, :].T.astype(BF16)
        vts.append(pr[0, KV_W:KV_ROW, :].astype(BF16))
        s = _dot(kk, qbd)
        if k == pgs - 1:
            s = s + jnp.where(last, slast_ref[...], 0.0)
        for hb in range(2):
            blocks.append(s[hb * SEL_BLOCK:(hb + 1) * SEL_BLOCK, :]
                          + msk_ref[pl.ds(2 * pidx + hb, 1), :])
    top = blocks[0]
    for s in blocks[1:]:
        top = jnp.maximum(top, s)
    m = m_ref[...]
    m_new = jnp.maximum(m, jnp.max(top, axis=0, keepdims=True))
    p = jnp.concatenate([jnp.exp2(s - m_new).astype(BF16) for s in blocks], axis=0)
    vt = jnp.concatenate([jnp.concatenate(vts, axis=1), jnp.ones((16, pgs * PAGE_SIZE), BF16)], axis=0)
    acc_ref[...] = jnp.exp2(m - m_new) * acc_ref[...] + _dot(vt, p)
    m_ref[...] = m_new

    @pl.when(last)
    def _():
        kn = new_ref[0, :, 0:KV_W].astype(BF16)
        vn = new_ref[0, :, KV_W:KV_ROW].astype(BF16)
        s = _dot(kn, qbd) + snew_ref[...] + msk_ref[pl.ds(2 * npg, 1), :]
        m = m_ref[...]
        m_new = jnp.maximum(m, jnp.max(s, axis=0, keepdims=True))
        p = jnp.exp2(s - m_new).astype(BF16)
        acc = jnp.exp2(m - m_new) * acc_ref[...]
        num = acc[0:KV_W] + _dot_t(vn, p)
        den = acc[KV_W:KV_W + 1] + jnp.sum(p.astype(F32), axis=0, keepdims=True)
        o = ocw_ref[...] + g[1:2, :] * (num * (1.0 / den))
        for gi in range(N_KV):
            o_ref[0, gi] = o[gi * HEAD_DIM:(gi + 1) * HEAD_DIM, gi * GQA * qcols:(gi + 1) * GQA * qcols]


def _nsa_sample(table, pages, qbd, gt, kc, vct, win, new, cbias, wbias, sfar, slast, snew,
                cover_t, rsum, past, ds):
    nb, npg = table.shape
    pgs = min(64, npg)
    assert npg % pgs == 0
    ncol = qbd.shape[2]
    ns = cover_t.shape[0]

    def page_spec(k):
        return pl.BlockSpec((1, KV_ROW, PAGE_SIZE), lambda b, j, tbl: (tbl[b, j * pgs + k], 0, 0))

    per_b = lambda a: pl.BlockSpec((1,) + a.shape[1:], lambda b, j, tbl: (b,) + (0,) * (a.ndim - 1))
    full = lambda a: pl.BlockSpec(a.shape, lambda b, j, tbl: (0,) * a.ndim)
    grid_spec = pltpu.PrefetchScalarGridSpec(
        num_scalar_prefetch=1,
        grid=(nb, npg // pgs),
        in_specs=[page_spec(k) for k in range(pgs)]
        + [per_b(a) for a in (qbd, gt, kc, vct, win, new)]
        + [full(a) for a in (cbias, wbias, sfar, slast, snew, cover_t, rsum)],
        out_specs=pl.BlockSpec((1, N_KV, HEAD_DIM, ncol // N_KV), lambda b, j, tbl: (b, 0, 0, 0)),
        scratch_shapes=[pltpu.VMEM((ns, ncol), F32), pltpu.VMEM((1, ncol), F32),
                        pltpu.VMEM((KV_W + 16, ncol), F32), pltpu.VMEM((KV_W, ncol), F32)],
    )
    return pl.pallas_call(
        functools.partial(_nsa_sample_kernel, pgs=pgs, npg=npg, past=past, ds=ds),
        grid_spec=grid_spec,
        out_shape=jax.ShapeDtypeStruct((nb, N_KV, HEAD_DIM, ncol // N_KV), F32),
        compiler_params=pltpu.CompilerParams(dimension_semantics=("parallel", "arbitrary"),
                                             vmem_limit_bytes=VMEM_LIMIT),
        name="nsa_sample",
    )(table, *([pages] * pgs), qbd, gt, kc, vct, win, new, cbias, wbias, sfar, slast, snew,
      cover_t, rsum)


def _cover_t(nc, ns, nc_pad, ns_pad):
    c0 = np.arange(nc) * CMP_STRIDE
    s0 = np.arange(ns) * SEL_BLOCK
    m = (c0[None, :] < s0[:, None] + SEL_BLOCK) & (c0[None, :] + CMP_BLOCK > s0[:, None])
    out = np.zeros((ns_pad, nc_pad), np.float32)
    out[:ns, :nc] = m
    return jnp.asarray(out, BF16)


def _prep_weights(norm_g, w_in, ln_v_g, spatial_w, spatial_b, cmp_pe, cmp_w1, cmp_b1, cmp_w2, cmp_b2,
                  w_out, final_g):
    offs = np.cumsum((D_A, D_A, D_A, D_B, 2 * KV_W, 2 * KV_W, 2 * KV_W, 3 * N_HEADS, D_B))
    g0, g1 = int(offs[6]), int(offs[7])
    w_perm = jnp.concatenate(
        [w_in[:, :g0], w_in[:, g1:], w_in[:, g0:g1],
         jnp.zeros((D_MODEL, GATE_PAD - 3 * N_HEADS), w_in.dtype)], axis=1).astype(BF16)
    pavg = jnp.asarray(np.kron(np.eye(A_GROUPS), np.full((HEAD_DIM, HEAD_DIM), 1.0 / HEAD_DIM)), BF16)
    w1cat = jnp.concatenate([cmp_w1[:, s].reshape(2, CMP_FLAT, CMP_HIDDEN) for s in range(CMP_R)],
                            axis=2).astype(BF16)
    pe8 = jnp.concatenate([cmp_pe.reshape(2, CMP_R, CMP_FLAT),
                           jnp.zeros((2, 8 - CMP_R, CMP_FLAT), F32)], axis=1)
    return dict(
        norm_g=norm_g.reshape(1, D_MODEL), w_perm=w_perm, ln_g=ln_v_g.reshape(1, D_A), pavg=pavg,
        w1cat=w1cat, pe8=pe8, b1=cmp_b1.reshape(2, 1, CMP_HIDDEN), w2=cmp_w2.astype(BF16),
        b2=cmp_b2.reshape(2, 1, HEAD_DIM), wo=w_out.astype(BF16), fg=final_g.reshape(1, D_MODEL),
        spatial_w=spatial_w, spatial_b=spatial_b)


def _spatial_operands(spatial_w, spatial_b, n):
    reps = CHUNK // n
    w = jnp.tril(spatial_w[:, :n, :n])
    eye = jnp.eye(reps, dtype=w.dtype)
    wsp = jnp.einsum('ab,gts->gatbs', eye, w).reshape(A_GROUPS, CHUNK, CHUNK).astype(BF16)
    b = jnp.tile(spatial_b[:, :n].T, (reps, 1))
    bsp = jnp.repeat(b, HEAD_DIM, axis=1)
    return wsp, bsp


def _prompt_bias_buckets():
    ql = np.arange(Q_BLOCK)[None, :]
    cl = np.arange(16)[:, None]
    band = np.stack([ql - CMP_STRIDE * cl - (CMP_BLOCK - 1),
                     ql + 97 - CMP_STRIDE * cl])
    kl = np.arange(CHUNK)[:, None]
    seld = np.stack([np.full((CHUNK, Q_BLOCK), FAR_DIST), CHUNK + ql - kl, ql - kl])
    wl = np.arange(WINDOW + Q_BLOCK)[:, None]
    dw = np.stack([Q_BLOCK * v + ql - wl for v in range(WIN_VARIANTS)])
    dw = np.where(dw < WINDOW, dw, -1)
    far = np.full((8, Q_BLOCK), FAR_DIST)
    tile4 = lambda d: np.tile(_t5_bucket_np(d.reshape(-1, Q_BLOCK)), (1, GQA))
    return tile4(band), tile4(seld), tile4(dw), tile4(far)


def _sample_bias_buckets(past, ds, qpad, ncp, nwin_pad):
    ql = (np.arange(qpad) % ds)[None, :]
    c = np.arange(ncp)[:, None]
    dc = past + ql - (CMP_STRIDE * c + CMP_BLOCK - 1)
    wl = np.arange(nwin_pad)[:, None]
    dw = WINDOW + ql - wl
    dw = np.where((dw < WINDOW) & (wl < WINDOW + ds), dw, -1)
    kl = np.arange(PAGE_SIZE)[:, None]
    dlast = PAGE_SIZE + ql - kl
    nl = np.arange(16)[:, None]
    dnew = np.where(nl < ds, ql - nl, -1)
    far = np.full((8, qpad), FAR_DIST)
    tile4 = lambda d: np.tile(_t5_bucket_np(d), (1, GQA))
    return tile4(dc), tile4(dw), tile4(dlast), tile4(dnew), tile4(far)


def kernel(x_prompt, x_sample, cache_cmp_kv, cache_sel_kv, state_win_kv, page_table, norm_g, w_in,
           ln_v_g, spatial_w, spatial_b, cmp_pe, cmp_w1, cmp_b1, cmp_w2, cmp_b2, rel_table, w_out,
           final_g):
    depth = norm_g.shape[0]
    assert depth == 1
    bsz, seq = x_prompt.shape[:2]
    db, ds = x_sample.shape[:2]
    npg = page_table.shape[1]
    past = npg * PAGE_SIZE
    win_buf = state_win_kv.shape[2]
    assert win_buf == WINDOW and past >= WINDOW and seq % Q_BLOCK == 0 and seq >= WINDOW + Q_BLOCK
    assert CHUNK % ds == 0 and ds <= 8 and (db * ds) % CHUNK == 0
    nblk = seq // Q_BLOCK
    l = 0
    wts = _prep_weights(norm_g[l], w_in[l], ln_v_g[l], spatial_w[l], spatial_b[l], cmp_pe[l],
                        cmp_w1[l], cmp_b1[l], cmp_w2[l], cmp_b2[l], w_out[l], final_g)
    inproj = lambda x, nb: _inproj(x, nb, wts['norm_g'], wts['w_perm'], wts['ln_g'], wts['pavg'])
    compress = lambda tbl, pages: _compress(tbl, pages, wts['w1cat'], wts['pe8'], wts['b1'],
                                            wts['w2'], wts['b2'])
    kv_out = lambda a: a.reshape(a.shape[0], 2, N_KV, HEAD_DIM, a.shape[2]).transpose(0, 4, 1, 2, 3)[None]
    pages_t = lambda c: c.transpose(0, 2, 3, 4, 1).reshape(c.shape[0], KV_ROW, PAGE_SIZE)

    n_p = bsz * seq
    xp = x_prompt.reshape(n_p, D_MODEL)
    ua, _, vnb, zas, zbs, kvc, kvs, kvw, qt, gt, ksk, vst, kwk, vwt = inproj(xp, bsz)
    kc, vct = compress(None, kvc)
    ncp = seq // CMP_STRIDE
    cover_p = _cover_t(ncp - CMP_R + 1, seq // SEL_BLOCK, ncp, seq // SEL_BLOCK)
    band_b, seld_b, win_b, far_b = _prompt_bias_buckets()
    cols = GQA * Q_BLOCK
    cband = _bias_tiles(rel_table, band_b, Q_BLOCK).reshape(N_KV, 2, 16, cols)
    seld = _bias_tiles(rel_table, seld_b, Q_BLOCK, rel_far=True).reshape(N_KV, 3, CHUNK, cols)
    wbias = _bias_tiles(rel_table, win_b, Q_BLOCK).reshape(N_KV, WIN_VARIANTS, WINDOW + Q_BLOCK, cols)
    far = _bias_tiles(rel_table, far_b, Q_BLOCK)[:, 0:1]
    ob = _nsa_prompt(qt, gt, kc, vct, ksk, vst, kwk, vwt, cband, far, seld, wbias, cover_p, bsz, seq)
    wsp, bsp = _spatial_operands(wts['spatial_w'], wts['spatial_b'], CHUNK)
    y_prompt = _mixout(xp, ua, vnb, zas, ob, zbs, wsp, bsp, wts['wo'], wts['fg']).reshape(bsz, seq, D_MODEL)
    new_cmp_p = kv_out(kvc)
    new_sel_p = kv_out(kvs)
    new_win_p = kv_out(kvw[:, :, seq - win_buf:])

    n_s = db * ds
    xs = x_sample.reshape(n_s, D_MODEL)
    ua, vn, vnb, zas, zbs, kvc, kvs, kvw, qt, gt, _, _, _, _ = inproj(xs, 1)
    kvc, kvs, kvw = (a[0].T for a in (kvc, kvs, kvw))
    kc, vct = compress(page_table, pages_t(cache_cmp_kv[l]))
    ncs = past // CMP_STRIDE
    t_all = past + ds
    nss = -(-t_all // SEL_BLOCK)
    nss_pad = -(-nss // 8) * 8
    cover_s = _cover_t(t_all // CMP_STRIDE - CMP_R + 1, nss, ncs, nss_pad)
    qpad = 16
    ncol = N_KV * GQA * qpad
    nwin_pad = -(-(win_buf + ds) // 16) * 16
    dc_b, dw_b, dl_b, dn_b, far_b = _sample_bias_buckets(past, ds, qpad, ncs, nwin_pad)
    both = lambda t: jnp.concatenate([t[0], t[1]], axis=1)
    cbias_s = both(_bias_tiles(rel_table, dc_b, qpad))
    wbias_s = both(_bias_tiles(rel_table, dw_b, qpad))
    slast = both(_bias_tiles(rel_table, dl_b, qpad, rel_far=True))
    snew = both(_bias_tiles(rel_table, dn_b, qpad, rel_far=True))
    sfar = both(_bias_tiles(rel_table, far_b, qpad))[0:1]

    qg = qt.reshape(N_KV, GQA, HEAD_DIM, db, ds).transpose(3, 0, 2, 1, 4)
    qg = jnp.tile(qg, (1, 1, 1, 1, qpad // ds)).reshape(db, N_KV, HEAD_DIM, GQA * qpad)
    zq = jnp.zeros_like(qg[:, 0])
    qbd = jnp.concatenate([jnp.concatenate([qg[:, 0], zq], axis=2),
                           jnp.concatenate([zq, qg[:, 1]], axis=2)], axis=1)
    gts = gt[:3 * N_HEADS].reshape(N_KV, GQA, 3, db, ds).transpose(3, 2, 0, 1, 4)
    gts = jnp.tile(gts, (1, 1, 1, 1, qpad // ds)).reshape(db, 3, ncol)
    win = jnp.concatenate([state_win_kv[l].reshape(db, win_buf, KV_ROW),
                           kvw.reshape(db, ds, KV_ROW)], axis=1)
    win_pad = jnp.pad(win, ((0, 0), (0, nwin_pad - win_buf - ds), (0, 0)))
    new_pad = jnp.pad(kvs.reshape(db, ds, KV_ROW), ((0, 0), (0, 16 - ds), (0, 0)))
    rsum = jnp.asarray(np.kron(np.eye(N_KV), np.kron(np.ones((GQA, GQA)), np.eye(qpad))), BF16)
    o_s = _nsa_sample(page_table, pages_t(cache_sel_kv[l]), qbd, gts,
                      kc, vct, win_pad, new_pad, cbias_s, wbias_s, sfar, slast, snew,
                      cover_s, rsum, past, ds)
    ob = (o_s.reshape(db, N_KV, HEAD_DIM, GQA, qpad)[..., :ds].transpose(0, 4, 1, 3, 2)
          .reshape(n_s, D_B))
    wsp, bsp = _spatial_operands(wts['spatial_w'], wts['spatial_b'], ds)
    y_sample = _mixout(xs, ua, vnb, zas, ob, zbs, wsp, bsp, wts['wo'], wts['fg']).reshape(db, ds, D_MODEL)
    kv5 = lambda a: a.reshape(1, db, ds, 2, N_KV, HEAD_DIM)
    new_cmp_s = kv5(kvc)
    new_sel_s = kv5(kvs)
    new_win_s = win[:, ds:].reshape(1, db, win_buf, 2, N_KV, HEAD_DIM)
    new_chunk_v = vn.reshape(1, db, ds, D_A)
    return (y_prompt, y_sample, new_cmp_p, new_sel_p, new_win_p, new_cmp_s, new_sel_s, new_win_s,
            new_chunk_v)
```

```python
import functools
import math

import numpy as np
import jax
import jax.numpy as jnp
from jax import lax
from jax.experimental import pallas as pl
from jax.experimental.pallas import tpu as pltpu

F32 = jnp.float32
BF16 = jnp.bfloat16

D_MODEL = 1024
HEAD_DIM = 64
D_A = 512
D_B = 512
A_GROUPS = D_A // HEAD_DIM
CHUNK = 128
N_HEADS = D_B // HEAD_DIM
N_KV = 2
GQA = N_HEADS // N_KV
KV_W = N_KV * HEAD_DIM
KV_ROW = 2 * KV_W
CMP_STRIDE = 16
CMP_BLOCK = 32
CMP_R = CMP_BLOCK // CMP_STRIDE
CMP_HIDDEN = 256
CMP_FLAT = CMP_STRIDE * HEAD_DIM
SEL_BLOCK = 64
N_SEL = 16
WINDOW = 512
N_BUCKETS = 32
MAX_DISTANCE = 128
Q_BLOCK = 128
PAGE_SIZE = 128
RMS_EPS = 1e-6
LN_EPS = 1e-5
NEG = -1e30
FORCE_BONUS = 1e6
LOG2E = 1.4426950408889634
Q_SCALE = HEAD_DIM ** -0.5 * LOG2E
FAR_DIST = 1 << 20
WIN_VARIANTS = WINDOW // Q_BLOCK + 1

_OFF_U, _OFF_V, _OFF_ZA, _OFF_Q = 0, 512, 1024, 1536
_OFF_KVC, _OFF_KVS, _OFF_KVW, _OFF_ZB, _OFF_G = 2048, 2304, 2560, 2816, 3328
D_IN_PAD = 3456
GATE_PAD = 128
GATE_ROWS = 32

VMEM_LIMIT = 52 * 1024 * 1024


def _gelu(x):
    return x * (0.5 * (1.0 + jnp.tanh(0.7978845608028654 * (x + 0.044715 * (x * x * x)))))


def _sigmoid(x):
    return 1.0 / (1.0 + jnp.exp(-x))


def _dot(a, b):
    return jnp.dot(a, b, preferred_element_type=F32)


def _dot_t(a, b):
    return lax.dot_general(a, b, (((0,), (0,)), ((), ())), preferred_element_type=F32)


def _dot_nt(a, b):
    return lax.dot_general(a, b, (((1,), (1,)), ((), ())), preferred_element_type=F32)


def _split_dot(a, b):
    hi = a.astype(BF16)
    lo = (a - hi.astype(F32)).astype(BF16)
    return _dot(hi, b) + _dot(lo, b)


def _split_dot_l(a, b):
    hi = b.astype(BF16)
    lo = (b - hi.astype(F32)).astype(BF16)
    return _dot(a, hi) + _dot(a, lo)


def _inproj_kernel(x_ref, ng_ref, w_ref, lng_ref, pavg_ref,
                   ua_ref, vn_ref, vnb_ref, zas_ref, zbs_ref, kvc_ref, kvs_ref, kvw_ref,
                   qt_ref, gt_ref, ksk_ref, vst_ref, kwk_ref, vwt_ref):
    x = x_ref[...]
    ms = jnp.mean(x * x, axis=-1, keepdims=True)
    h = (x * lax.rsqrt(ms + RMS_EPS) * ng_ref[...]).astype(BF16)

    def proj(a, b):
        return _dot(h, w_ref[:, a:b])

    ua_ref[...] = _gelu(proj(_OFF_U, _OFF_V)).astype(BF16)
    v = _gelu(proj(_OFF_V, _OFF_ZA))
    mu = _dot(v.astype(BF16), pavg_ref[...])
    d = v - mu
    var = _dot((d * d).astype(BF16), pavg_ref[...])
    vn = d * lax.rsqrt(var + LN_EPS) * lng_ref[...]
    vn_ref[...] = vn
    vnb_ref[...] = vn.astype(BF16)
    za = proj(_OFF_ZA, _OFF_Q)
    zas_ref[...] = (za * _sigmoid(za)).astype(BF16)
    zb = proj(_OFF_ZB, _OFF_G)
    zbs_ref[...] = (zb * _sigmoid(zb)).astype(BF16)
    qt_ref[...] = (proj(_OFF_Q, _OFF_KVC) * Q_SCALE).T.astype(BF16)
    gt_ref[...] = _sigmoid(proj(_OFF_G, D_IN_PAD)).T[0:GATE_ROWS, :]
    kvc_ref[0] = proj(_OFF_KVC, _OFF_KVS).T
    kvs = proj(_OFF_KVS, _OFF_KVW)
    kvs_t = kvs.T
    kvs_ref[0] = kvs_t
    ksk_ref[...] = kvs[:, 0:KV_W].astype(BF16)
    vst_ref[...] = kvs_t[KV_W:KV_ROW, :].astype(BF16)
    kvw = proj(_OFF_KVW, _OFF_ZB)
    kvw_t = kvw.T
    kvw_ref[0] = kvw_t
    kwk_ref[...] = kvw[:, 0:KV_W].astype(BF16)
    vwt_ref[...] = kvw_t[KV_W:KV_ROW, :].astype(BF16)


def _inproj(x, nb, norm_g, w_perm, ln_g, pavg):
    n = x.shape[0]
    s = n // nb
    tm = min(1024, s)
    assert s % tm == 0
    per = s // tm
    row = lambda w: pl.BlockSpec((tm, w), lambda i: (i, 0))
    col = lambda h: pl.BlockSpec((h, tm), lambda i: (0, i))
    kvt = pl.BlockSpec((1, KV_ROW, tm), lambda i: (i // per, 0, i % per))
    full = lambda a: pl.BlockSpec(a.shape, lambda i: (0,) * a.ndim)
    rows = [(D_A, BF16), (D_A, F32), (D_A, BF16), (D_A, BF16), (D_B, BF16)]
    sds = lambda shape, dt: jax.ShapeDtypeStruct(shape, dt)
    out_specs = ([row(w) for w, _ in rows] + [kvt, kvt, kvt]
                 + [col(D_B), col(GATE_ROWS), row(KV_W), col(KV_W), row(KV_W), col(KV_W)])
    out_shape = ([sds((n, w), dt) for w, dt in rows] + [sds((nb, KV_ROW, s), F32)] * 3
                 + [sds((D_B, n), BF16), sds((GATE_ROWS, n), F32), sds((n, KV_W), BF16),
                    sds((KV_W, n), BF16), sds((n, KV_W), BF16), sds((KV_W, n), BF16)])
    return pl.pallas_call(
        _inproj_kernel,
        grid=(n // tm,),
        in_specs=[row(D_MODEL), full(norm_g), full(w_perm), full(ln_g), full(pavg)],
        out_specs=out_specs,
        out_shape=out_shape,
        compiler_params=pltpu.CompilerParams(dimension_semantics=("parallel",),
                                             vmem_limit_bytes=VMEM_LIMIT),
        name="inproj",
    )(x, norm_g, w_perm, ln_g, pavg)


def _mixout_kernel(x_ref, ua_ref, vn_ref, zas_ref, ob_ref, zbs_ref, wsp_ref, bsp_ref, wo_ref, fg_ref,
                   y_ref, s_ref):
    tm = x_ref.shape[0]
    for c in range(tm // CHUNK):
        rows = slice(c * CHUNK, (c + 1) * CHUNK)
        vc = vn_ref[rows, :]
        for g in range(A_GROUPS):
            cols = slice(g * HEAD_DIM, (g + 1) * HEAD_DIM)
            s_ref[rows, cols] = _dot(wsp_ref[g], vc[:, cols])
        s_ref[rows, :] = s_ref[rows, :] + bsp_ref[...]
    mix_a = (ua_ref[...].astype(F32) * s_ref[...] * zas_ref[...].astype(F32)).astype(BF16)
    mix_b = (ob_ref[...].astype(F32) * zbs_ref[...].astype(F32)).astype(BF16)
    y = x_ref[...] + _dot(mix_a, wo_ref[0:D_A, :]) + _dot(mix_b, wo_ref[D_A:D_A + D_B, :])
    ms = jnp.mean(y * y, axis=-1, keepdims=True)
    y_ref[...] = y * lax.rsqrt(ms + RMS_EPS) * fg_ref[...]


def _mixout(x, ua, vn, zas, ob, zbs, wsp, bsp, wo, fg):
    n = x.shape[0]
    tm = min(512, n)
    row = lambda w: pl.BlockSpec((tm, w), lambda i: (i, 0))
    full = lambda a: pl.BlockSpec(a.shape, lambda i: (0,) * a.ndim)
    return pl.pallas_call(
        _mixout_kernel,
        grid=(n // tm,),
        in_specs=[row(D_MODEL), row(D_A), row(D_A), row(D_A), row(D_B), row(D_B),
                  full(wsp), full(bsp), full(wo), full(fg)],
        out_specs=row(D_MODEL),
        out_shape=jax.ShapeDtypeStruct((n, D_MODEL), F32),
        scratch_shapes=[pltpu.VMEM((tm, D_A), F32)],
        compiler_params=pltpu.CompilerParams(dimension_semantics=("parallel",),
                                             vmem_limit_bytes=VMEM_LIMIT),
        name="mixout",
    )(x, ua, vn, zas, ob, zbs, wsp, bsp, wo, fg)


def _t5_bucket_np(dist):
    dist = np.asarray(dist, np.int64)
    n = np.maximum(dist, 0)
    max_exact = N_BUCKETS // 2
    nf = np.maximum(n, max_exact).astype(np.float64)
    large = max_exact + (np.log(nf / max_exact) / math.log(MAX_DISTANCE / max_exact)
                         * (N_BUCKETS - max_exact)).astype(np.int64)
    b = np.where(n < max_exact, n, np.minimum(large, N_BUCKETS - 1))
    return np.where(dist < 0, -1, b).astype(np.int32)


def _bias_kernel(tbl_ref, bkt_ref, out_ref, *, qb, rel_far):
    b = bkt_ref[...]
    grp = lax.broadcasted_iota(jnp.int32, (1, b.shape[1]), 1) // qb

    def head_row(k, g):
        row = jnp.zeros(grp.shape, F32)
        for r in range(GQA):
            row = jnp.where(grp == r, tbl_ref[k * N_HEADS + g * GQA + r], row)
        return row * LOG2E

    for g in range(N_KV):
        acc = jnp.full(b.shape, NEG, F32)
        base = head_row(N_BUCKETS - 1, g) if rel_far else None
        for k in range(N_BUCKETS):
            row = head_row(k, g)
            if rel_far:
                row = row - base
            acc = jnp.where(b == k, row, acc)
        out_ref[g] = acc


def _bias_tiles(rel_table, bkt, qb, rel_far=False):
    r, c = bkt.shape
    rb = r
    for cand in (512, 256, 128, 64, 32, 16, 8):
        if r % cand == 0:
            rb = cand
            break
    return pl.pallas_call(
        functools.partial(_bias_kernel, qb=qb, rel_far=rel_far),
        grid=(r // rb,),
        in_specs=[pl.BlockSpec(memory_space=pltpu.SMEM), pl.BlockSpec((rb, c), lambda i: (i, 0))],
        out_specs=pl.BlockSpec((N_KV, rb, c), lambda i: (0, i, 0)),
        out_shape=jax.ShapeDtypeStruct((N_KV, r, c), F32),
        compiler_params=pltpu.CompilerParams(dimension_semantics=("parallel",)),
        name="bias_tiles",
    )(rel_table.reshape(-1), jnp.asarray(bkt))


def _compress_kernel(tbl_ref, *refs, pg):
    del tbl_ref
    page_refs = refs[:pg + 1]
    perm_ref, w1_ref, pe_ref, b1_ref, w2_ref, b2_ref, kc_ref, vct_ref, x_ref = refs[pg + 1:]
    mp = (pg + 1) * 8
    nrow = pg * 8
    left = lax.broadcasted_iota(jnp.int32, (8, 128), 1) < HEAD_DIM
    perm = perm_ref[...]
    for k, pr in enumerate(page_refs):
        tok = _dot_nt(perm, pr[0].astype(BF16))
        for kv in range(2):
            for t in range(CMP_STRIDE // 2):
                e = tok[16 * t:16 * t + 8, kv * KV_W:(kv + 1) * KV_W]
                o = tok[16 * t + 8:16 * t + 16, kv * KV_W:(kv + 1) * KV_W]
                sw = pltpu.roll(jnp.where(left, o, e), HEAD_DIM, 1)
                x_ref[kv, 8 * k:8 * k + 8, 128 * t:128 * t + 128] = jnp.where(left, e, sw)
                x_ref[kv, mp + 8 * k:mp + 8 * k + 8, 128 * t:128 * t + 128] = jnp.where(left, sw, o)
    outs = []
    for kv in range(2):
        x_ref[kv, 2 * mp:2 * mp + 8, :] = pe_ref[kv]
        p = _dot(x_ref[kv].astype(BF16), w1_ref[kv])
        hc = (b1_ref[kv] + p[2 * mp:2 * mp + 1, 0:CMP_HIDDEN]
              + p[2 * mp + 1:2 * mp + 2, CMP_HIDDEN:2 * CMP_HIDDEN])
        per_g = []
        for g in range(N_KV):
            base = g * mp
            h = (p[base:base + nrow, 0:CMP_HIDDEN]
                 + p[base + 1:base + nrow + 1, CMP_HIDDEN:2 * CMP_HIDDEN] + hc)
            per_g.append(_dot(_gelu(h).astype(BF16), w2_ref[kv]) + b2_ref[kv])
        outs.append(jnp.concatenate(per_g, axis=1))
    kc_ref[0] = outs[0].astype(BF16)
    vct_ref[0] = outs[1].T.astype(BF16)


def _compress(table, pages, w1cat, pe8, b1, w2, b2):
    if table is None:
        nb, npg = pages.shape[0], pages.shape[2] // PAGE_SIZE
        table = jnp.zeros((1, 1), jnp.int32)
        index = lambda b, idx, tbl: (b, 0, idx)
    else:
        nb, npg = table.shape
        index = lambda b, idx, tbl: (tbl[b, idx], 0, 0)
    pg = min(64, npg)
    assert npg % pg == 0
    mp = (pg + 1) * 8

    def page_spec(k):
        return pl.BlockSpec(
            (1, KV_ROW, PAGE_SIZE),
            lambda b, j, tbl: index(b, jnp.minimum(j * pg + k, npg - 1), tbl))

    tok = np.arange(PAGE_SIZE)
    perm_np = np.zeros((PAGE_SIZE, PAGE_SIZE), np.float32)
    perm_np[(tok % CMP_STRIDE) * (PAGE_SIZE // CMP_STRIDE) + tok // CMP_STRIDE, tok] = 1.0
    perm = jnp.asarray(perm_np, BF16)
    full = lambda a: pl.BlockSpec(a.shape, lambda b, j, tbl: (0,) * a.ndim)
    grid_spec = pltpu.PrefetchScalarGridSpec(
        num_scalar_prefetch=1,
        grid=(nb, npg // pg),
        in_specs=[page_spec(k) for k in range(pg + 1)]
        + [full(a) for a in (perm, w1cat, pe8, b1, w2, b2)],
        out_specs=[pl.BlockSpec((1, pg * 8, KV_W), lambda b, j, tbl: (b, j, 0)),
                   pl.BlockSpec((1, KV_W, pg * 8), lambda b, j, tbl: (b, 0, j))],
        scratch_shapes=[pltpu.VMEM((2, 2 * mp + 8, CMP_FLAT), F32)],
    )
    return pl.pallas_call(
        functools.partial(_compress_kernel, pg=pg),
        grid_spec=grid_spec,
        out_shape=[jax.ShapeDtypeStruct((nb, npg * 8, KV_W), BF16),
                   jax.ShapeDtypeStruct((nb, KV_W, npg * 8), BF16)],
        compiler_params=pltpu.CompilerParams(dimension_semantics=("parallel", "parallel"),
                                             vmem_limit_bytes=VMEM_LIMIT),
        name="compress",
    )(table, *([pages] * (pg + 1)), perm, w1cat, pe8, b1, w2, b2)


def _topk_rows(imp, n_sel):
    ns = imp.shape[0]
    blk = lax.broadcasted_iota(jnp.int32, imp.shape, 0).astype(F32)
    for _ in range(n_sel):
        mx = jnp.max(imp, axis=0, keepdims=True)
        idx = jnp.min(jnp.where(imp == mx, blk, float(ns)), axis=0, keepdims=True)
        imp = jnp.where(blk == idx, -jnp.inf, imp)
    return imp == -jnp.inf


def _select_mask(imp, qpos):
    blk = lax.broadcasted_iota(jnp.int32, imp.shape, 0)
    cur = qpos // SEL_BLOCK
    forced = (blk == 0) | (blk == cur) | (blk == cur - 1)
    valid = blk * SEL_BLOCK <= qpos
    imp = imp + jnp.where(forced, FORCE_BONUS, 0.0)
    imp = jnp.where(valid, imp, NEG)
    sel = _topk_rows(imp, N_SEL)
    return jnp.where(sel & valid, 0.0, NEG)


def _softmax_rows(s):
    m = jnp.max(s, axis=0, keepdims=True)
    e = jnp.exp2(s - m)
    inv = jnp.where(m > NEG / 2, 1.0 / jnp.sum(e, axis=0, keepdims=True), 0.0)
    return e * inv


def _online_update(s, v_dot, m, l, acc):
    m_new = jnp.maximum(m, jnp.max(s, axis=0, keepdims=True))
    alpha = jnp.exp2(m - m_new)
    p = jnp.exp2(s - m_new)
    l = alpha * l + jnp.sum(p, axis=0, keepdims=True)
    acc = alpha * acc + v_dot(p.astype(BF16))
    return m_new, l, acc


def _nsa_prompt_kernel(qt_ref, gt_ref, kc_ref, vct_ref, ks_ref, vst_ref, kw_ref, vwt_ref,
                       cband_ref, far_ref, seld_ref, wbias_ref, cover_ref,
                       o_ref, bias_ref, msk_ref, s_ref, *, nsub):
    i = pl.program_id(1)
    cols = GQA * Q_BLOCK
    ncp = kc_ref.shape[1]
    gw = GQA * HEAD_DIM
    qpos = i * Q_BLOCK + lax.broadcasted_iota(jnp.int32, (1, Q_BLOCK), 1)
    n_trips = i // nsub + 1
    tk = nsub * CHUNK
    c0 = pl.multiple_of(jnp.maximum(i * 8 - 8, 0), 8)
    crow = lax.broadcasted_iota(jnp.int32, (ncp, 1), 0)
    woff = pl.multiple_of(jnp.maximum(i * Q_BLOCK - WINDOW, 0), Q_BLOCK)
    nwk = WINDOW + Q_BLOCK

    def query(g):
        qblk = qt_ref[g * gw:(g + 1) * gw, :]
        q64 = jnp.concatenate([qblk[r * HEAD_DIM:(r + 1) * HEAD_DIM, :] for r in range(GQA)], axis=1)
        zero = jnp.zeros_like(q64)
        return jnp.concatenate([q64, zero] if g == 0 else [zero, q64], axis=0)

    def compressed(g, q):
        far = far_ref[g]
        bias_ref[g] = jnp.where(crow < c0, far, NEG)
        bias_ref[g, pl.ds(c0, 16), :] = cband_ref[g, jnp.minimum(i, 1)]
        s_c = _dot(kc_ref[0], q) + bias_ref[g]
        m_c = jnp.max(s_c, axis=0, keepdims=True)
        e_c = jnp.exp2(s_c - m_c).astype(BF16)
        vc = jnp.concatenate([vct_ref[0, g * HEAD_DIM:(g + 1) * HEAD_DIM, :],
                              jnp.ones((16, ncp), BF16)], axis=0)
        a_c = _dot(vc, e_c)
        inv = jnp.where(m_c > NEG / 2, 1.0 / a_c[HEAD_DIM:HEAD_DIM + 1], 0.0)
        o_c = a_c[0:HEAD_DIM] * inv
        w = _dot(cover_ref[...], e_c) * inv
        imp = (w[:, 0:Q_BLOCK] + w[:, Q_BLOCK:2 * Q_BLOCK]
               + w[:, 2 * Q_BLOCK:3 * Q_BLOCK] + w[:, 3 * Q_BLOCK:4 * Q_BLOCK])
        return o_c, imp

    def window(g, q):
        s_w = _dot(kw_ref[pl.ds(woff, nwk), :], q) + wbias_ref[g, 0]
        e_w = jnp.exp2(s_w - jnp.max(s_w, axis=0, keepdims=True)).astype(BF16)
        vw = jnp.concatenate([vwt_ref[g * HEAD_DIM:(g + 1) * HEAD_DIM, pl.ds(woff, nwk)],
                              jnp.ones((16, nwk), BF16)], axis=0)
        a_w = _dot(vw, e_w)
        return a_w[0:HEAD_DIM] * (1.0 / a_w[HEAD_DIM:HEAD_DIM + 1])

    def select(g, imp):
        mask = _select_mask(imp, qpos)
        msk_ref[g] = jnp.concatenate([mask] * GQA, axis=1) + far_ref[g]

    def scores(g, q, t, slot):
        t = jnp.minimum(t, n_trips - 1)
        koff = pl.multiple_of(t * tk, tk)
        s = _dot(ks_ref[pl.ds(koff, tk), :], q)
        for j in range(2 * nsub):
            blk = slice(j * SEL_BLOCK, (j + 1) * SEL_BLOCK)
            s_ref[g, slot, blk, :] = s[blk] + msk_ref[g, pl.ds(2 * nsub * t + j, 1), :]

    def attend(g, t, slot, near, carry):
        s = s_ref[g, slot]
        if near:
            s = s + jnp.concatenate(
                [seld_ref[g, jnp.clip(t * nsub + u - i + 2, 0, 2)] for u in range(nsub)], axis=0)
        koff = pl.multiple_of(t * tk, tk)
        vt = jnp.concatenate([vst_ref[g * HEAD_DIM:(g + 1) * HEAD_DIM, pl.ds(koff, tk)],
                              jnp.ones((16, tk), BF16)], axis=0)
        m, acc = carry
        m_new = jnp.maximum(m, jnp.max(s, axis=0, keepdims=True))
        p = jnp.exp2(s - m_new).astype(BF16)
        return m_new, jnp.exp2(m - m_new) * acc + _dot(vt, p)

    def selected(qs):
        groups = range(N_KV)

        def make_pair(near):
            def pair(tt, carry):
                for g in groups:
                    scores(g, qs[g], 2 * tt + 1, 1)
                carry = [attend(g, 2 * tt, 0, near, carry[g]) for g in groups]
                for g in groups:
                    scores(g, qs[g], 2 * tt + 2, 0)
                return [attend(g, 2 * tt + 1, 1, near, carry[g]) for g in groups]
            return pair

        n_far_pairs = (jnp.maximum(i - 1, 0) // nsub) // 2
        init = [(jnp.full((1, cols), -jnp.inf, F32), jnp.zeros((HEAD_DIM + 16, cols), F32))
                for _ in groups]
        carry = lax.fori_loop(0, n_far_pairs, make_pair(False), init)
        carry = lax.fori_loop(n_far_pairs, n_trips // 2, make_pair(True), carry)
        carry = lax.cond(n_trips % 2 == 1,
                         lambda c: [attend(g, n_trips - 1, 0, True, c[g]) for g in groups],
                         lambda c: c, carry)
        return [acc[0:HEAD_DIM] * (1.0 / acc[HEAD_DIM:HEAD_DIM + 1]) for _, acc in carry]

    gt = gt_ref[...]

    def gate(g, j):
        return jnp.concatenate([gt[(g * GQA + r) * 3 + j:(g * GQA + r) * 3 + j + 1, :]
                                for r in range(GQA)], axis=1)

    qs = [query(g) for g in range(N_KV)]
    cmp_out, o_win = [], []
    for g in range(N_KV):
        cmp_out.append(compressed(g, qs[g]))
        o_win.append(window(g, qs[g]))
        select(g, cmp_out[g][1])
    for g in range(N_KV):
        scores(g, qs[g], 0, 0)
    o_sel = selected(qs)
    for g in range(N_KV):
        o = gate(g, 0) * cmp_out[g][0] + gate(g, 1) * o_sel[g] + gate(g, 2) * o_win[g]
        o_ref[:, g * gw:(g + 1) * gw] = jnp.concatenate(
            [o[:, r * Q_BLOCK:(r + 1) * Q_BLOCK].T for r in range(GQA)], axis=1).astype(BF16)


def _nsa_prompt(qt, gt, kc, vct, ks, vst, kw, vwt, cband, far, seld, wbias, cover_t, bsz, seq):
    nblk = seq // Q_BLOCK
    ncp = kc.shape[1]
    ns = cover_t.shape[0]
    cols = GQA * Q_BLOCK
    nsub = next(c for c in (4, 2, 1) if nblk % c == 0)
    full = lambda a: pl.BlockSpec(a.shape, lambda b, i: (0,) * a.ndim)
    in_specs = [
        pl.BlockSpec((D_B, Q_BLOCK), lambda b, i: (0, b * nblk + i)),
        pl.BlockSpec((GATE_ROWS, Q_BLOCK), lambda b, i: (0, b * nblk + i)),
        pl.BlockSpec((1, ncp, KV_W), lambda b, i: (b, 0, 0)),
        pl.BlockSpec((1, KV_W, ncp), lambda b, i: (b, 0, 0)),
        pl.BlockSpec((seq, KV_W), lambda b, i: (b, 0)),
        pl.BlockSpec((KV_W, seq), lambda b, i: (0, b)),
        pl.BlockSpec((seq, KV_W), lambda b, i: (b, 0)),
        pl.BlockSpec((KV_W, seq), lambda b, i: (0, b)),
        full(cband), full(far), full(seld),
        pl.BlockSpec((N_KV, 1) + wbias.shape[2:],
                     lambda b, i: (0, jnp.minimum(i, WIN_VARIANTS - 1), 0, 0)),
        full(cover_t),
    ]
    return pl.pallas_call(
        functools.partial(_nsa_prompt_kernel, nsub=nsub),
        grid=(bsz, nblk),
        in_specs=in_specs,
        out_specs=pl.BlockSpec((Q_BLOCK, D_B), lambda b, i: (b * nblk + i, 0)),
        out_shape=jax.ShapeDtypeStruct((bsz * seq, D_B), BF16),
        scratch_shapes=[pltpu.VMEM((N_KV, ncp, cols), F32), pltpu.VMEM((N_KV, ns, cols), F32),
                        pltpu.VMEM((N_KV, 2, nsub * CHUNK, cols), F32)],
        compiler_params=pltpu.CompilerParams(
            dimension_semantics=("parallel", "arbitrary"),
            vmem_limit_bytes=VMEM_LIMIT),
        name="nsa_prompt",
    )(qt, gt, kc, vct, ks, vst, kw, vwt, cband, far, seld, wbias, cover_t)


def _nsa_sample_kernel(tbl_ref, *refs, pgs, npg, past, ds):
    del tbl_ref
    page_refs = refs[:pgs]
    (qbd_ref, gt_ref, kc_ref, vct_ref, win_ref, new_ref, cbias_ref, wbias_ref, sfar_ref, slast_ref,
     snew_ref, cover_ref, rsum_ref, o_ref, msk_ref, m_ref, acc_ref, ocw_ref) = refs[pgs:]
    j = pl.program_id(1)
    qbd = qbd_ref[0]
    ncol = qbd.shape[1]
    qcols = ncol // (N_KV * GQA)
    g = gt_ref[0]

    @pl.when(j == 0)
    def _():
        p_c = _softmax_rows(_dot(kc_ref[0], qbd) + cbias_ref[...])
        o_c = _dot(vct_ref[0], p_c.astype(BF16))
        imp = _split_dot(_split_dot_l(cover_ref[...], p_c), rsum_ref[...])
        lane = lax.broadcasted_iota(jnp.int32, (1, ncol), 1)
        qpos = past + (lane % qcols) % ds
        msk_ref[...] = _select_mask(imp, qpos) + sfar_ref[...]
        kw = win_ref[0, :, 0:KV_W].astype(BF16)
        vw = win_ref[0, :, KV_W:KV_ROW].astype(BF16)
        p_w = _softmax_rows(_dot(kw, qbd) + wbias_ref[...])
        o_w = _dot_t(vw, p_w.astype(BF16))
        ocw_ref[...] = g[0:1, :] * o_c + g[2:3, :] * o_w
        m_ref[...] = jnp.full(m_ref.shape, -jnp.inf, F32)
        acc_ref[...] = jnp.zeros(acc_ref.shape, F32)

    last = j == pl.num_programs(1) - 1
    n_chain = m_ref.shape[0]
    per_chain = pgs // n_chain
    for c in range(n_chain):
        blocks, vts = [], []
        for k in range(c * per_chain, (c + 1) * per_chain):
            pr = page_refs[k]
            pidx = j * pgs + k
            kk = pr[0, 0:KV_W, :].T.astype(BF16)
            vts.append(pr[0, KV_W:KV_ROW, :].astype(BF16))
            s = _dot(kk, qbd)
            if k == pgs - 1:
                s = s + jnp.where(last, slast_ref[...], 0.0)
            for hb in range(2):
                blocks.append(s[hb * SEL_BLOCK:(hb + 1) * SEL_BLOCK, :]
                              + msk_ref[pl.ds(2 * pidx + hb, 1), :])
        top = blocks[0]
        for s in blocks[1:]:
            top = jnp.maximum(top, s)
        m = m_ref[c]
        m_new = jnp.maximum(m, jnp.max(top, axis=0, keepdims=True))
        p = jnp.concatenate([jnp.exp2(s - m_new).astype(BF16) for s in blocks], axis=0)
        vt = jnp.concatenate([jnp.concatenate(vts, axis=1),
                              jnp.ones((16, per_chain * PAGE_SIZE), BF16)], axis=0)
        acc_ref[c] = jnp.exp2(m - m_new) * acc_ref[c] + _dot(vt, p)
        m_ref[c] = m_new

    @pl.when(last)
    def _():
        kn = new_ref[0, :, 0:KV_W].astype(BF16)
        vn = new_ref[0, :, KV_W:KV_ROW].astype(BF16)
        s = _dot(kn, qbd) + snew_ref[...] + msk_ref[pl.ds(2 * npg, 1), :]
        m_new = jnp.max(s, axis=0, keepdims=True)
        for c in range(n_chain):
            m_new = jnp.maximum(m_new, m_ref[c])
        p = jnp.exp2(s - m_new).astype(BF16)
        acc = jnp.exp2(m_ref[0] - m_new) * acc_ref[0]
        for c in range(1, n_chain):
            acc = acc + jnp.exp2(m_ref[c] - m_new) * acc_ref[c]
        num = acc[0:KV_W] + _dot_t(vn, p)
        den = acc[KV_W:KV_W + 1] + jnp.sum(p.astype(F32), axis=0, keepdims=True)
        o = ocw_ref[...] + g[1:2, :] * (num * (1.0 / den))
        for gi in range(N_KV):
            o_ref[0, gi] = o[gi * HEAD_DIM:(gi + 1) * HEAD_DIM, gi * GQA * qcols:(gi + 1) * GQA * qcols]


def _nsa_sample(table, pages, qbd, gt, kc, vct, win, new, cbias, wbias, sfar, slast, snew,
                cover_t, rsum, past, ds):
    nb, npg = table.shape
    pgs = min(64, npg)
    assert npg % pgs == 0
    n_chain = 2 if pgs % 2 == 0 else 1
    ncol = qbd.shape[2]
    ns = cover_t.shape[0]

    def page_spec(k):
        return pl.BlockSpec((1, KV_ROW, PAGE_SIZE), lambda b, j, tbl: (tbl[b, j * pgs + k], 0, 0))

    per_b = lambda a: pl.BlockSpec((1,) + a.shape[1:], lambda b, j, tbl: (b,) + (0,) * (a.ndim - 1))
    full = lambda a: pl.BlockSpec(a.shape, lambda b, j, tbl: (0,) * a.ndim)
    grid_spec = pltpu.PrefetchScalarGridSpec(
        num_scalar_prefetch=1,
        grid=(nb, npg // pgs),
        in_specs=[page_spec(k) for k in range(pgs)]
        + [per_b(a) for a in (qbd, gt, kc, vct, win, new)]
        + [full(a) for a in (cbias, wbias, sfar, slast, snew, cover_t, rsum)],
        out_specs=pl.BlockSpec((1, N_KV, HEAD_DIM, ncol // N_KV), lambda b, j, tbl: (b, 0, 0, 0)),
        scratch_shapes=[pltpu.VMEM((ns, ncol), F32), pltpu.VMEM((n_chain, 1, ncol), F32),
                        pltpu.VMEM((n_chain, KV_W + 16, ncol), F32), pltpu.VMEM((KV_W, ncol), F32)],
    )
    return pl.pallas_call(
        functools.partial(_nsa_sample_kernel, pgs=pgs, npg=npg, past=past, ds=ds),
        grid_spec=grid_spec,
        out_shape=jax.ShapeDtypeStruct((nb, N_KV, HEAD_DIM, ncol // N_KV), F32),
        compiler_params=pltpu.CompilerParams(dimension_semantics=("parallel", "arbitrary"),
                                             vmem_limit_bytes=VMEM_LIMIT),
        name="nsa_sample",
    )(table, *([pages] * pgs), qbd, gt, kc, vct, win, new, cbias, wbias, sfar, slast, snew,
      cover_t, rsum)


def _cover_t(nc, ns, nc_pad, ns_pad):
    c0 = np.arange(nc) * CMP_STRIDE
    s0 = np.arange(ns) * SEL_BLOCK
    m = (c0[None, :] < s0[:, None] + SEL_BLOCK) & (c0[None, :] + CMP_BLOCK > s0[:, None])
    out = np.zeros((ns_pad, nc_pad), np.float32)
    out[:ns, :nc] = m
    return jnp.asarray(out, BF16)


def _prep_weights(norm_g, w_in, ln_v_g, spatial_w, spatial_b, cmp_pe, cmp_w1, cmp_b1, cmp_w2, cmp_b2,
                  w_out, final_g):
    offs = np.cumsum((D_A, D_A, D_A, D_B, 2 * KV_W, 2 * KV_W, 2 * KV_W, 3 * N_HEADS, D_B))
    g0, g1 = int(offs[6]), int(offs[7])
    w_perm = jnp.concatenate(
        [w_in[:, :g0], w_in[:, g1:], w_in[:, g0:g1],
         jnp.zeros((D_MODEL, GATE_PAD - 3 * N_HEADS), w_in.dtype)], axis=1).astype(BF16)
    pavg = jnp.asarray(np.kron(np.eye(A_GROUPS), np.full((HEAD_DIM, HEAD_DIM), 1.0 / HEAD_DIM)), BF16)
    w1cat = jnp.concatenate([cmp_w1[:, s].reshape(2, CMP_FLAT, CMP_HIDDEN) for s in range(CMP_R)],
                            axis=2).astype(BF16)
    pe8 = jnp.concatenate([cmp_pe.reshape(2, CMP_R, CMP_FLAT),
                           jnp.zeros((2, 8 - CMP_R, CMP_FLAT), F32)], axis=1)
    return dict(
        norm_g=norm_g.reshape(1, D_MODEL), w_perm=w_perm, ln_g=ln_v_g.reshape(1, D_A), pavg=pavg,
        w1cat=w1cat, pe8=pe8, b1=cmp_b1.reshape(2, 1, CMP_HIDDEN), w2=cmp_w2.astype(BF16),
        b2=cmp_b2.reshape(2, 1, HEAD_DIM), wo=w_out.astype(BF16), fg=final_g.reshape(1, D_MODEL),
        spatial_w=spatial_w, spatial_b=spatial_b)


def _spatial_operands(spatial_w, spatial_b, n):
    reps = CHUNK // n
    w = jnp.tril(spatial_w[:, :n, :n])
    eye = jnp.eye(reps, dtype=w.dtype)
    wsp = jnp.einsum('ab,gts->gatbs', eye, w).reshape(A_GROUPS, CHUNK, CHUNK).astype(BF16)
    b = jnp.tile(spatial_b[:, :n].T, (reps, 1))
    bsp = jnp.repeat(b, HEAD_DIM, axis=1)
    return wsp, bsp


def _prompt_bias_buckets():
    ql = np.arange(Q_BLOCK)[None, :]
    cl = np.arange(16)[:, None]
    band = np.stack([ql - CMP_STRIDE * cl - (CMP_BLOCK - 1),
                     ql + 97 - CMP_STRIDE * cl])
    kl = np.arange(CHUNK)[:, None]
    seld = np.stack([np.full((CHUNK, Q_BLOCK), FAR_DIST), CHUNK + ql - kl, ql - kl])
    wl = np.arange(WINDOW + Q_BLOCK)[:, None]
    dw = np.stack([Q_BLOCK * v + ql - wl for v in range(WIN_VARIANTS)])
    dw = np.where(dw < WINDOW, dw, -1)
    far = np.full((8, Q_BLOCK), FAR_DIST)
    tile4 = lambda d: np.tile(_t5_bucket_np(d.reshape(-1, Q_BLOCK)), (1, GQA))
    return tile4(band), tile4(seld), tile4(dw), tile4(far)


def _sample_bias_buckets(past, ds, qpad, ncp, nwin_pad):
    ql = (np.arange(qpad) % ds)[None, :]
    c = np.arange(ncp)[:, None]
    dc = past + ql - (CMP_STRIDE * c + CMP_BLOCK - 1)
    wl = np.arange(nwin_pad)[:, None]
    dw = WINDOW + ql - wl
    dw = np.where((dw < WINDOW) & (wl < WINDOW + ds), dw, -1)
    kl = np.arange(PAGE_SIZE)[:, None]
    dlast = PAGE_SIZE + ql - kl
    nl = np.arange(16)[:, None]
    dnew = np.where(nl < ds, ql - nl, -1)
    far = np.full((8, qpad), FAR_DIST)
    tile4 = lambda d: np.tile(_t5_bucket_np(d), (1, GQA))
    return tile4(dc), tile4(dw), tile4(dlast), tile4(dnew), tile4(far)


def kernel(x_prompt, x_sample, cache_cmp_kv, cache_sel_kv, state_win_kv, page_table, norm_g, w_in,
           ln_v_g, spatial_w, spatial_b, cmp_pe, cmp_w1, cmp_b1, cmp_w2, cmp_b2, rel_table, w_out,
           final_g):
    depth = norm_g.shape[0]
    assert depth == 1
    bsz, seq = x_prompt.shape[:2]
    db, ds = x_sample.shape[:2]
    npg = page_table.shape[1]
    past = npg * PAGE_SIZE
    win_buf = state_win_kv.shape[2]
    assert win_buf == WINDOW and past >= WINDOW and seq % Q_BLOCK == 0 and seq >= WINDOW + Q_BLOCK
    assert CHUNK % ds == 0 and ds <= 8 and (db * ds) % CHUNK == 0
    nblk = seq // Q_BLOCK
    l = 0
    wts = _prep_weights(norm_g[l], w_in[l], ln_v_g[l], spatial_w[l], spatial_b[l], cmp_pe[l],
                        cmp_w1[l], cmp_b1[l], cmp_w2[l], cmp_b2[l], w_out[l], final_g)
    inproj = lambda x, nb: _inproj(x, nb, wts['norm_g'], wts['w_perm'], wts['ln_g'], wts['pavg'])
    compress = lambda tbl, pages: _compress(tbl, pages, wts['w1cat'], wts['pe8'], wts['b1'],
                                            wts['w2'], wts['b2'])
    kv_out = lambda a: a.reshape(a.shape[0], 2, N_KV, HEAD_DIM, a.shape[2]).transpose(0, 4, 1, 2, 3)[None]
    pages_t = lambda c: c.transpose(0, 2, 3, 4, 1).reshape(c.shape[0], KV_ROW, PAGE_SIZE)

    n_p = bsz * seq
    xp = x_prompt.reshape(n_p, D_MODEL)
    ua, _, vnb, zas, zbs, kvc, kvs, kvw, qt, gt, ksk, vst, kwk, vwt = inproj(xp, bsz)
    kc, vct = compress(None, kvc)
    ncp = seq // CMP_STRIDE
    cover_p = _cover_t(ncp - CMP_R + 1, seq // SEL_BLOCK, ncp, seq // SEL_BLOCK)
    band_b, seld_b, win_b, far_b = _prompt_bias_buckets()
    cols = GQA * Q_BLOCK
    cband = _bias_tiles(rel_table, band_b, Q_BLOCK).reshape(N_KV, 2, 16, cols)
    seld = _bias_tiles(rel_table, seld_b, Q_BLOCK, rel_far=True).reshape(N_KV, 3, CHUNK, cols)
    wbias = _bias_tiles(rel_table, win_b, Q_BLOCK).reshape(N_KV, WIN_VARIANTS, WINDOW + Q_BLOCK, cols)
    far = _bias_tiles(rel_table, far_b, Q_BLOCK)[:, 0:1]
    ob = _nsa_prompt(qt, gt, kc, vct, ksk, vst, kwk, vwt, cband, far, seld, wbias, cover_p, bsz, seq)
    wsp, bsp = _spatial_operands(wts['spatial_w'], wts['spatial_b'], CHUNK)
    y_prompt = _mixout(xp, ua, vnb, zas, ob, zbs, wsp, bsp, wts['wo'], wts['fg']).reshape(bsz, seq, D_MODEL)
    new_cmp_p = kv_out(kvc)
    new_sel_p = kv_out(kvs)
    new_win_p = kv_out(kvw[:, :, seq - win_buf:])

    n_s = db * ds
    xs = x_sample.reshape(n_s, D_MODEL)
    ua, vn, vnb, zas, zbs, kvc, kvs, kvw, qt, gt, _, _, _, _ = inproj(xs, 1)
    kvc, kvs, kvw = (a[0].T for a in (kvc, kvs, kvw))
    kc, vct = compress(page_table, pages_t(cache_cmp_kv[l]))
    ncs = past // CMP_STRIDE
    t_all = past + ds
    nss = -(-t_all // SEL_BLOCK)
    nss_pad = -(-nss // 8) * 8
    cover_s = _cover_t(t_all // CMP_STRIDE - CMP_R + 1, nss, ncs, nss_pad)
    qpad = 16
    ncol = N_KV * GQA * qpad
    nwin_pad = -(-(win_buf + ds) // 16) * 16
    dc_b, dw_b, dl_b, dn_b, far_b = _sample_bias_buckets(past, ds, qpad, ncs, nwin_pad)
    both = lambda t: jnp.concatenate([t[0], t[1]], axis=1)
    cbias_s = both(_bias_tiles(rel_table, dc_b, qpad))
    wbias_s = both(_bias_tiles(rel_table, dw_b, qpad))
    slast = both(_bias_tiles(rel_table, dl_b, qpad, rel_far=True))
    snew = both(_bias_tiles(rel_table, dn_b, qpad, rel_far=True))
    sfar = both(_bias_tiles(rel_table, far_b, qpad))[0:1]

    qg = qt.reshape(N_KV, GQA, HEAD_DIM, db, ds).transpose(3, 0, 2, 1, 4)
    qg = jnp.tile(qg, (1, 1, 1, 1, qpad // ds)).reshape(db, N_KV, HEAD_DIM, GQA * qpad)
    zq = jnp.zeros_like(qg[:, 0])
    qbd = jnp.concatenate([jnp.concatenate([qg[:, 0], zq], axis=2),
                           jnp.concatenate([zq, qg[:, 1]], axis=2)], axis=1)
    gts = gt[:3 * N_HEADS].reshape(N_KV, GQA, 3, db, ds).transpose(3, 2, 0, 1, 4)
    gts = jnp.tile(gts, (1, 1, 1, 1, qpad // ds)).reshape(db, 3, ncol)
    win = jnp.concatenate([state_win_kv[l].reshape(db, win_buf, KV_ROW),
                           kvw.reshape(db, ds, KV_ROW)], axis=1)
    win_pad = jnp.pad(win, ((0, 0), (0, nwin_pad - win_buf - ds), (0, 0)))
    new_pad = jnp.pad(kvs.reshape(db, ds, KV_ROW), ((0, 0), (0, 16 - ds), (0, 0)))
    rsum = jnp.asarray(np.kron(np.eye(N_KV), np.kron(np.ones((GQA, GQA)), np.eye(qpad))), BF16)
    o_s = _nsa_sample(page_table, pages_t(cache_sel_kv[l]), qbd, gts,
                      kc, vct, win_pad, new_pad, cbias_s, wbias_s, sfar, slast, snew,
                      cover_s, rsum, past, ds)
    ob = (o_s.reshape(db, N_KV, HEAD_DIM, GQA, qpad)[..., :ds].transpose(0, 4, 1, 3, 2)
          .reshape(n_s, D_B))
    wsp, bsp = _spatial_operands(wts['spatial_w'], wts['spatial_b'], ds)
    y_sample = _mixout(xs, ua, vnb, zas, ob, zbs, wsp, bsp, wts['wo'], wts['fg']).reshape(db, ds, D_MODEL)
    kv5 = lambda a: a.reshape(1, db, ds, 2, N_KV, HEAD_DIM)
    new_cmp_s = kv5(kvc)
    new_sel_s = kv5(kvs)
    new_win_s = win[:, ds:].reshape(1, db, win_buf, 2, N_KV, HEAD_DIM)
    new_chunk_v = vn.reshape(1, db, ds, D_A)
    return (y_prompt, y_sample, new_cmp_p, new_sel_p, new_win_p, new_cmp_s, new_sel_s, new_win_s,
            new_chunk_v)
```

```python
import functools
import math

import numpy as np
import jax
import jax.numpy as jnp
from jax import lax
from jax.experimental import pallas as pl
from jax.experimental.pallas import tpu as pltpu

F32 = jnp.float32
BF16 = jnp.bfloat16

D_MODEL = 1024
HEAD_DIM = 64
D_A = 512
D_B = 512
A_GROUPS = D_A // HEAD_DIM
CHUNK = 128
N_HEADS = D_B // HEAD_DIM
N_KV = 2
GQA = N_HEADS // N_KV
KV_W = N_KV * HEAD_DIM
KV_ROW = 2 * KV_W
CMP_STRIDE = 16
CMP_BLOCK = 32
CMP_R = CMP_BLOCK // CMP_STRIDE
CMP_HIDDEN = 256
CMP_FLAT = CMP_STRIDE * HEAD_DIM
SEL_BLOCK = 64
N_SEL = 16
WINDOW = 512
N_BUCKETS = 32
MAX_DISTANCE = 128
Q_BLOCK = 128
PAGE_SIZE = 128
RMS_EPS = 1e-6
LN_EPS = 1e-5
NEG = -1e30
FORCE_BONUS = 1e6
LOG2E = 1.4426950408889634
Q_SCALE = HEAD_DIM ** -0.5 * LOG2E
FAR_DIST = 1 << 20
WIN_VARIANTS = WINDOW // Q_BLOCK + 1

_OFF_U, _OFF_V, _OFF_ZA, _OFF_Q = 0, 512, 1024, 1536
_OFF_KVC, _OFF_KVS, _OFF_KVW, _OFF_ZB, _OFF_G = 2048, 2304, 2560, 2816, 3328
D_IN_PAD = 3456
GATE_PAD = 128
GATE_ROWS = 32

VMEM_LIMIT = 52 * 1024 * 1024


def _gelu(x):
    return x * (0.5 * (1.0 + jnp.tanh(0.7978845608028654 * (x + 0.044715 * (x * x * x)))))


def _sigmoid(x):
    return 1.0 / (1.0 + jnp.exp(-x))


def _dot(a, b):
    return jnp.dot(a, b, preferred_element_type=F32)


def _dot_t(a, b):
    return lax.dot_general(a, b, (((0,), (0,)), ((), ())), preferred_element_type=F32)


def _dot_nt(a, b):
    return lax.dot_general(a, b, (((1,), (1,)), ((), ())), preferred_element_type=F32)


def _split_dot(a, b):
    hi = a.astype(BF16)
    lo = (a - hi.astype(F32)).astype(BF16)
    return _dot(hi, b) + _dot(lo, b)


def _split_dot_l(a, b):
    hi = b.astype(BF16)
    lo = (b - hi.astype(F32)).astype(BF16)
    return _dot(a, hi) + _dot(a, lo)


def _inproj_kernel(x_ref, ng_ref, w_ref, lng_ref, pavg_ref,
                   ua_ref, vn_ref, vnb_ref, zas_ref, zbs_ref, kvc_ref, kvs_ref, kvw_ref,
                   qt_ref, gt_ref, ksk_ref, vst_ref, kwk_ref, vwt_ref):
    x = x_ref[...]
    ms = jnp.mean(x * x, axis=-1, keepdims=True)
    h = (x * lax.rsqrt(ms + RMS_EPS) * ng_ref[...]).astype(BF16)

    def proj(a, b):
        return _dot(h, w_ref[:, a:b])

    ua_ref[...] = _gelu(proj(_OFF_U, _OFF_V)).astype(BF16)
    v = _gelu(proj(_OFF_V, _OFF_ZA))
    mu = _dot(v.astype(BF16), pavg_ref[...])
    d = v - mu
    var = _dot((d * d).astype(BF16), pavg_ref[...])
    vn = d * lax.rsqrt(var + LN_EPS) * lng_ref[...]
    vn_ref[...] = vn
    vnb_ref[...] = vn.astype(BF16)
    za = proj(_OFF_ZA, _OFF_Q)
    zas_ref[...] = (za * _sigmoid(za)).astype(BF16)
    zb = proj(_OFF_ZB, _OFF_G)
    zbs_ref[...] = (zb * _sigmoid(zb)).astype(BF16)
    qt_ref[...] = (proj(_OFF_Q, _OFF_KVC) * Q_SCALE).T.astype(BF16)
    gt_ref[...] = _sigmoid(proj(_OFF_G, D_IN_PAD)).T[0:GATE_ROWS, :]
    kvc_ref[0] = proj(_OFF_KVC, _OFF_KVS).T
    kvs = proj(_OFF_KVS, _OFF_KVW)
    kvs_t = kvs.T
    kvs_ref[0] = kvs_t
    ksk_ref[...] = kvs[:, 0:KV_W].astype(BF16)
    vst_ref[...] = kvs_t[KV_W:KV_ROW, :].astype(BF16)
    kvw = proj(_OFF_KVW, _OFF_ZB)
    kvw_t = kvw.T
    kvw_ref[0] = kvw_t
    kwk_ref[...] = kvw[:, 0:KV_W].astype(BF16)
    vwt_ref[...] = kvw_t[KV_W:KV_ROW, :].astype(BF16)


def _inproj(x, nb, norm_g, w_perm, ln_g, pavg):
    n = x.shape[0]
    s = n // nb
    tm = min(1024, s)
    assert s % tm == 0
    per = s // tm
    row = lambda w: pl.BlockSpec((tm, w), lambda i: (i, 0))
    col = lambda h: pl.BlockSpec((h, tm), lambda i: (0, i))
    kvt = pl.BlockSpec((1, KV_ROW, tm), lambda i: (i // per, 0, i % per))
    full = lambda a: pl.BlockSpec(a.shape, lambda i: (0,) * a.ndim)
    rows = [(D_A, BF16), (D_A, F32), (D_A, BF16), (D_A, BF16), (D_B, BF16)]
    sds = lambda shape, dt: jax.ShapeDtypeStruct(shape, dt)
    out_specs = ([row(w) for w, _ in rows] + [kvt, kvt, kvt]
                 + [col(D_B), col(GATE_ROWS), row(KV_W), col(KV_W), row(KV_W), col(KV_W)])
    out_shape = ([sds((n, w), dt) for w, dt in rows] + [sds((nb, KV_ROW, s), F32)] * 3
                 + [sds((D_B, n), BF16), sds((GATE_ROWS, n), F32), sds((n, KV_W), BF16),
                    sds((KV_W, n), BF16), sds((n, KV_W), BF16), sds((KV_W, n), BF16)])
    return pl.pallas_call(
        _inproj_kernel,
        grid=(n // tm,),
        in_specs=[row(D_MODEL), full(norm_g), full(w_perm), full(ln_g), full(pavg)],
        out_specs=out_specs,
        out_shape=out_shape,
        compiler_params=pltpu.CompilerParams(dimension_semantics=("parallel",),
                                             vmem_limit_bytes=VMEM_LIMIT),
        name="inproj",
    )(x, norm_g, w_perm, ln_g, pavg)


def _mixout_kernel(x_ref, ua_ref, vn_ref, zas_ref, ob_ref, zbs_ref, wsp_ref, bsp_ref, wo_ref, fg_ref,
                   y_ref, s_ref):
    tm = x_ref.shape[0]
    for c in range(tm // CHUNK):
        rows = slice(c * CHUNK, (c + 1) * CHUNK)
        vc = vn_ref[rows, :]
        for g in range(A_GROUPS):
            cols = slice(g * HEAD_DIM, (g + 1) * HEAD_DIM)
            s_ref[rows, cols] = _dot(wsp_ref[g], vc[:, cols])
        s_ref[rows, :] = s_ref[rows, :] + bsp_ref[...]
    mix_a = (ua_ref[...].astype(F32) * s_ref[...] * zas_ref[...].astype(F32)).astype(BF16)
    mix_b = (ob_ref[...].astype(F32) * zbs_ref[...].astype(F32)).astype(BF16)
    y = x_ref[...] + _dot(mix_a, wo_ref[0:D_A, :]) + _dot(mix_b, wo_ref[D_A:D_A + D_B, :])
    ms = jnp.mean(y * y, axis=-1, keepdims=True)
    y_ref[...] = y * lax.rsqrt(ms + RMS_EPS) * fg_ref[...]


def _mixout(x, ua, vn, zas, ob, zbs, wsp, bsp, wo, fg):
    n = x.shape[0]
    tm = min(512, n)
    row = lambda w: pl.BlockSpec((tm, w), lambda i: (i, 0))
    full = lambda a: pl.BlockSpec(a.shape, lambda i: (0,) * a.ndim)
    return pl.pallas_call(
        _mixout_kernel,
        grid=(n // tm,),
        in_specs=[row(D_MODEL), row(D_A), row(D_A), row(D_A), row(D_B), row(D_B),
                  full(wsp), full(bsp), full(wo), full(fg)],
        out_specs=row(D_MODEL),
        out_shape=jax.ShapeDtypeStruct((n, D_MODEL), F32),
        scratch_shapes=[pltpu.VMEM((tm, D_A), F32)],
        compiler_params=pltpu.CompilerParams(dimension_semantics=("parallel",),
                                             vmem_limit_bytes=VMEM_LIMIT),
        name="mixout",
    )(x, ua, vn, zas, ob, zbs, wsp, bsp, wo, fg)


def _t5_bucket_np(dist):
    dist = np.asarray(dist, np.int64)
    n = np.maximum(dist, 0)
    max_exact = N_BUCKETS // 2
    nf = np.maximum(n, max_exact).astype(np.float64)
    large = max_exact + (np.log(nf / max_exact) / math.log(MAX_DISTANCE / max_exact)
                         * (N_BUCKETS - max_exact)).astype(np.int64)
    b = np.where(n < max_exact, n, np.minimum(large, N_BUCKETS - 1))
    return np.where(dist < 0, -1, b).astype(np.int32)


def _bias_kernel(tbl_ref, bkt_ref, out_ref, *, qb, rel_far):
    b = bkt_ref[...]
    grp = lax.broadcasted_iota(jnp.int32, (1, b.shape[1]), 1) // qb

    def head_row(k, g):
        row = jnp.zeros(grp.shape, F32)
        for r in range(GQA):
            row = jnp.where(grp == r, tbl_ref[k * N_HEADS + g * GQA + r], row)
        return row * LOG2E

    for g in range(N_KV):
        acc = jnp.full(b.shape, NEG, F32)
        base = head_row(N_BUCKETS - 1, g) if rel_far else None
        for k in range(N_BUCKETS):
            row = head_row(k, g)
            if rel_far:
                row = row - base
            acc = jnp.where(b == k, row, acc)
        out_ref[g] = acc


def _bias_tiles(rel_table, bkt, qb, rel_far=False):
    r, c = bkt.shape
    rb = r
    for cand in (512, 256, 128, 64, 32, 16, 8):
        if r % cand == 0:
            rb = cand
            break
    return pl.pallas_call(
        functools.partial(_bias_kernel, qb=qb, rel_far=rel_far),
        grid=(r // rb,),
        in_specs=[pl.BlockSpec(memory_space=pltpu.SMEM), pl.BlockSpec((rb, c), lambda i: (i, 0))],
        out_specs=pl.BlockSpec((N_KV, rb, c), lambda i: (0, i, 0)),
        out_shape=jax.ShapeDtypeStruct((N_KV, r, c), F32),
        compiler_params=pltpu.CompilerParams(dimension_semantics=("parallel",)),
        name="bias_tiles",
    )(rel_table.reshape(-1), jnp.asarray(bkt))


def _compress_kernel(tbl_ref, *refs, pg):
    del tbl_ref
    page_refs = refs[:pg + 1]
    perm_ref, w1_ref, pe_ref, b1_ref, w2_ref, b2_ref, kc_ref, vct_ref, x_ref = refs[pg + 1:]
    mp = (pg + 1) * 8
    nrow = pg * 8
    left = lax.broadcasted_iota(jnp.int32, (8, 128), 1) < HEAD_DIM
    perm = perm_ref[...]
    for k, pr in enumerate(page_refs):
        tok = _dot_nt(perm, pr[0].astype(BF16))
        for kv in range(2):
            for t in range(CMP_STRIDE // 2):
                e = tok[16 * t:16 * t + 8, kv * KV_W:(kv + 1) * KV_W]
                o = tok[16 * t + 8:16 * t + 16, kv * KV_W:(kv + 1) * KV_W]
                sw = pltpu.roll(jnp.where(left, o, e), HEAD_DIM, 1)
                x_ref[kv, 8 * k:8 * k + 8, 128 * t:128 * t + 128] = jnp.where(left, e, sw)
                x_ref[kv, mp + 8 * k:mp + 8 * k + 8, 128 * t:128 * t + 128] = jnp.where(left, sw, o)
    outs = []
    for kv in range(2):
        x_ref[kv, 2 * mp:2 * mp + 8, :] = pe_ref[kv]
        p = _dot(x_ref[kv].astype(BF16), w1_ref[kv])
        hc = (b1_ref[kv] + p[2 * mp:2 * mp + 1, 0:CMP_HIDDEN]
              + p[2 * mp + 1:2 * mp + 2, CMP_HIDDEN:2 * CMP_HIDDEN])
        per_g = []
        for g in range(N_KV):
            base = g * mp
            h = (p[base:base + nrow, 0:CMP_HIDDEN]
                 + p[base + 1:base + nrow + 1, CMP_HIDDEN:2 * CMP_HIDDEN] + hc)
            per_g.append(_dot(_gelu(h).astype(BF16), w2_ref[kv]) + b2_ref[kv])
        outs.append(jnp.concatenate(per_g, axis=1))
    kc_ref[0] = outs[0].astype(BF16)
    vct_ref[0] = outs[1].T.astype(BF16)


def _compress(table, pages, w1cat, pe8, b1, w2, b2):
    if table is None:
        nb, npg = pages.shape[0], pages.shape[2] // PAGE_SIZE
        table = jnp.zeros((1, 1), jnp.int32)
        index = lambda b, idx, tbl: (b, 0, idx)
    else:
        nb, npg = table.shape
        index = lambda b, idx, tbl: (tbl[b, idx], 0, 0)
    pg = min(64, npg)
    assert npg % pg == 0
    mp = (pg + 1) * 8

    def page_spec(k):
        return pl.BlockSpec(
            (1, KV_ROW, PAGE_SIZE),
            lambda b, j, tbl: index(b, jnp.minimum(j * pg + k, npg - 1), tbl))

    tok = np.arange(PAGE_SIZE)
    perm_np = np.zeros((PAGE_SIZE, PAGE_SIZE), np.float32)
    perm_np[(tok % CMP_STRIDE) * (PAGE_SIZE // CMP_STRIDE) + tok // CMP_STRIDE, tok] = 1.0
    perm = jnp.asarray(perm_np, BF16)
    full = lambda a: pl.BlockSpec(a.shape, lambda b, j, tbl: (0,) * a.ndim)
    grid_spec = pltpu.PrefetchScalarGridSpec(
        num_scalar_prefetch=1,
        grid=(nb, npg // pg),
        in_specs=[page_spec(k) for k in range(pg + 1)]
        + [full(a) for a in (perm, w1cat, pe8, b1, w2, b2)],
        out_specs=[pl.BlockSpec((1, pg * 8, KV_W), lambda b, j, tbl: (b, j, 0)),
                   pl.BlockSpec((1, KV_W, pg * 8), lambda b, j, tbl: (b, 0, j))],
        scratch_shapes=[pltpu.VMEM((2, 2 * mp + 8, CMP_FLAT), F32)],
    )
    return pl.pallas_call(
        functools.partial(_compress_kernel, pg=pg),
        grid_spec=grid_spec,
        out_shape=[jax.ShapeDtypeStruct((nb, npg * 8, KV_W), BF16),
                   jax.ShapeDtypeStruct((nb, KV_W, npg * 8), BF16)],
        compiler_params=pltpu.CompilerParams(dimension_semantics=("parallel", "parallel"),
                                             vmem_limit_bytes=VMEM_LIMIT),
        name="compress",
    )(table, *([pages] * (pg + 1)), perm, w1cat, pe8, b1, w2, b2)


def _topk_rows(imp, n_sel):
    ns = imp.shape[0]
    blk = lax.broadcasted_iota(jnp.int32, imp.shape, 0).astype(F32)
    for _ in range(n_sel):
        mx = jnp.max(imp, axis=0, keepdims=True)
        idx = jnp.min(jnp.where(imp == mx, blk, float(ns)), axis=0, keepdims=True)
        imp = jnp.where(blk == idx, -jnp.inf, imp)
    return imp == -jnp.inf


def _select_mask(imp, qpos):
    blk = lax.broadcasted_iota(jnp.int32, imp.shape, 0)
    cur = qpos // SEL_BLOCK
    forced = (blk == 0) | (blk == cur) | (blk == cur - 1)
    valid = blk * SEL_BLOCK <= qpos
    imp = imp + jnp.where(forced, FORCE_BONUS, 0.0)
    imp = jnp.where(valid, imp, NEG)
    sel = _topk_rows(imp, N_SEL)
    return jnp.where(sel & valid, 0.0, NEG)


def _softmax_rows(s):
    m = jnp.max(s, axis=0, keepdims=True)
    e = jnp.exp2(s - m)
    inv = jnp.where(m > NEG / 2, 1.0 / jnp.sum(e, axis=0, keepdims=True), 0.0)
    return e * inv


def _online_update(s, v_dot, m, l, acc):
    m_new = jnp.maximum(m, jnp.max(s, axis=0, keepdims=True))
    alpha = jnp.exp2(m - m_new)
    p = jnp.exp2(s - m_new)
    l = alpha * l + jnp.sum(p, axis=0, keepdims=True)
    acc = alpha * acc + v_dot(p.astype(BF16))
    return m_new, l, acc


def _nsa_prompt_kernel(qt_ref, gt_ref, kc_ref, vct_ref, ks_ref, vst_ref, kw_ref, vwt_ref,
                       cband_ref, far_ref, seld_ref, wbias_ref, cover_ref,
                       o_ref, bias_ref, msk_ref, s_ref, *, nsub):
    i = pl.program_id(1)
    cols = GQA * Q_BLOCK
    ncp = kc_ref.shape[1]
    gw = GQA * HEAD_DIM
    qpos = i * Q_BLOCK + lax.broadcasted_iota(jnp.int32, (1, Q_BLOCK), 1)
    n_trips = i // nsub + 1
    tk = nsub * CHUNK
    c0 = pl.multiple_of(jnp.maximum(i * 8 - 8, 0), 8)
    crow = lax.broadcasted_iota(jnp.int32, (ncp, 1), 0)
    woff = pl.multiple_of(jnp.maximum(i * Q_BLOCK - WINDOW, 0), Q_BLOCK)
    nwk = WINDOW + Q_BLOCK

    def query(g):
        qblk = qt_ref[g * gw:(g + 1) * gw, :]
        q64 = jnp.concatenate([qblk[r * HEAD_DIM:(r + 1) * HEAD_DIM, :] for r in range(GQA)], axis=1)
        zero = jnp.zeros_like(q64)
        return jnp.concatenate([q64, zero] if g == 0 else [zero, q64], axis=0)

    def compressed(g, q):
        far = far_ref[g]
        bias_ref[g] = jnp.where(crow < c0, far, NEG)
        bias_ref[g, pl.ds(c0, 16), :] = cband_ref[g, jnp.minimum(i, 1)]
        s_c = _dot(kc_ref[0], q) + bias_ref[g]
        m_c = jnp.max(s_c, axis=0, keepdims=True)
        e_c = jnp.exp2(s_c - m_c).astype(BF16)
        vc = jnp.concatenate([vct_ref[0, g * HEAD_DIM:(g + 1) * HEAD_DIM, :],
                              jnp.ones((16, ncp), BF16)], axis=0)
        a_c = _dot(vc, e_c)
        inv = jnp.where(m_c > NEG / 2, 1.0 / a_c[HEAD_DIM:HEAD_DIM + 1], 0.0)
        o_c = a_c[0:HEAD_DIM] * inv
        w = _dot(cover_ref[...], e_c) * inv
        imp = (w[:, 0:Q_BLOCK] + w[:, Q_BLOCK:2 * Q_BLOCK]
               + w[:, 2 * Q_BLOCK:3 * Q_BLOCK] + w[:, 3 * Q_BLOCK:4 * Q_BLOCK])
        return o_c, imp

    def window(g, q):
        s_w = _dot(kw_ref[pl.ds(woff, nwk), :], q) + wbias_ref[g, 0]
        e_w = jnp.exp2(s_w - jnp.max(s_w, axis=0, keepdims=True)).astype(BF16)
        vw = jnp.concatenate([vwt_ref[g * HEAD_DIM:(g + 1) * HEAD_DIM, pl.ds(woff, nwk)],
                              jnp.ones((16, nwk), BF16)], axis=0)
        a_w = _dot(vw, e_w)
        return a_w[0:HEAD_DIM] * (1.0 / a_w[HEAD_DIM:HEAD_DIM + 1])

    def select(g, imp):
        mask = _select_mask(imp, qpos)
        msk_ref[g] = jnp.concatenate([mask] * GQA, axis=1) + far_ref[g]

    def scores(g, q, t, slot):
        t = jnp.minimum(t, n_trips - 1)
        koff = pl.multiple_of(t * tk, tk)
        s = _dot(ks_ref[pl.ds(koff, tk), :], q)
        for j in range(2 * nsub):
            blk = slice(j * SEL_BLOCK, (j + 1) * SEL_BLOCK)
            s_ref[g, slot, blk, :] = s[blk] + msk_ref[g, pl.ds(2 * nsub * t + j, 1), :]

    def attend(g, t, slot, near, carry):
        s = s_ref[g, slot]
        if near:
            s = s + jnp.concatenate(
                [seld_ref[g, jnp.clip(t * nsub + u - i + 2, 0, 2)] for u in range(nsub)], axis=0)
        koff = pl.multiple_of(t * tk, tk)
        vt = jnp.concatenate([vst_ref[g * HEAD_DIM:(g + 1) * HEAD_DIM, pl.ds(koff, tk)],
                              jnp.ones((16, tk), BF16)], axis=0)
        m, acc = carry
        m_new = jnp.maximum(m, jnp.max(s, axis=0, keepdims=True))
        p = jnp.exp2(s - m_new).astype(BF16)
        return m_new, jnp.exp2(m - m_new) * acc + _dot(vt, p)

    def selected(qs):
        groups = range(N_KV)

        def make_pair(near):
            def pair(tt, carry):
                for g in groups:
                    scores(g, qs[g], 2 * tt + 1, 1)
                carry = [attend(g, 2 * tt, 0, near, carry[g]) for g in groups]
                for g in groups:
                    scores(g, qs[g], 2 * tt + 2, 0)
                return [attend(g, 2 * tt + 1, 1, near, carry[g]) for g in groups]
            return pair

        n_far_pairs = (jnp.maximum(i - 1, 0) // nsub) // 2
        init = [(jnp.full((1, cols), -jnp.inf, F32), jnp.zeros((HEAD_DIM + 16, cols), F32))
                for _ in groups]
        carry = lax.fori_loop(0, n_far_pairs, make_pair(False), init)
        carry = lax.fori_loop(n_far_pairs, n_trips // 2, make_pair(True), carry)
        carry = lax.cond(n_trips % 2 == 1,
                         lambda c: [attend(g, n_trips - 1, 0, True, c[g]) for g in groups],
                         lambda c: c, carry)
        return [acc[0:HEAD_DIM] * (1.0 / acc[HEAD_DIM:HEAD_DIM + 1]) for _, acc in carry]

    gt = gt_ref[...]

    def gate(g, j):
        return jnp.concatenate([gt[(g * GQA + r) * 3 + j:(g * GQA + r) * 3 + j + 1, :]
                                for r in range(GQA)], axis=1)

    qs = [query(g) for g in range(N_KV)]
    cmp_out, o_win = [], []
    for g in range(N_KV):
        cmp_out.append(compressed(g, qs[g]))
        o_win.append(window(g, qs[g]))
        select(g, cmp_out[g][1])
    for g in range(N_KV):
        scores(g, qs[g], 0, 0)
    o_sel = selected(qs)
    for g in range(N_KV):
        o = gate(g, 0) * cmp_out[g][0] + gate(g, 1) * o_sel[g] + gate(g, 2) * o_win[g]
        o_ref[:, g * gw:(g + 1) * gw] = jnp.concatenate(
            [o[:, r * Q_BLOCK:(r + 1) * Q_BLOCK].T for r in range(GQA)], axis=1).astype(BF16)


def _nsa_prompt(qt, gt, kc, vct, ks, vst, kw, vwt, cband, far, seld, wbias, cover_t, bsz, seq):
    nblk = seq // Q_BLOCK
    ncp = kc.shape[1]
    ns = cover_t.shape[0]
    cols = GQA * Q_BLOCK
    nsub = next(c for c in (4, 2, 1) if nblk % c == 0)
    full = lambda a: pl.BlockSpec(a.shape, lambda b, i: (0,) * a.ndim)
    in_specs = [
        pl.BlockSpec((D_B, Q_BLOCK), lambda b, i: (0, b * nblk + i)),
        pl.BlockSpec((GATE_ROWS, Q_BLOCK), lambda b, i: (0, b * nblk + i)),
        pl.BlockSpec((1, ncp, KV_W), lambda b, i: (b, 0, 0)),
        pl.BlockSpec((1, KV_W, ncp), lambda b, i: (b, 0, 0)),
        pl.BlockSpec((seq, KV_W), lambda b, i: (b, 0)),
        pl.BlockSpec((KV_W, seq), lambda b, i: (0, b)),
        pl.BlockSpec((seq, KV_W), lambda b, i: (b, 0)),
        pl.BlockSpec((KV_W, seq), lambda b, i: (0, b)),
        full(cband), full(far), full(seld),
        pl.BlockSpec((N_KV, 1) + wbias.shape[2:],
                     lambda b, i: (0, jnp.minimum(i, WIN_VARIANTS - 1), 0, 0)),
        full(cover_t),
    ]
    return pl.pallas_call(
        functools.partial(_nsa_prompt_kernel, nsub=nsub),
        grid=(bsz, nblk),
        in_specs=in_specs,
        out_specs=pl.BlockSpec((Q_BLOCK, D_B), lambda b, i: (b * nblk + i, 0)),
        out_shape=jax.ShapeDtypeStruct((bsz * seq, D_B), BF16),
        scratch_shapes=[pltpu.VMEM((N_KV, ncp, cols), F32), pltpu.VMEM((N_KV, ns, cols), F32),
                        pltpu.VMEM((N_KV, 2, nsub * CHUNK, cols), F32)],
        compiler_params=pltpu.CompilerParams(
            dimension_semantics=("parallel", "arbitrary"),
            vmem_limit_bytes=VMEM_LIMIT),
        name="nsa_prompt",
    )(qt, gt, kc, vct, ks, vst, kw, vwt, cband, far, seld, wbias, cover_t)


def _nsa_sample_kernel(tbl_ref, *refs, pgs, npg, past, ds):
    del tbl_ref
    page_refs = refs[:pgs]
    (qbd_ref, gt_ref, kc_ref, vct_ref, win_ref, new_ref, cbias_ref, wbias_ref, sfar_ref, slast_ref,
     snew_ref, cover_ref, rsum_ref, o_ref, msk_ref, m_ref, acc_ref, ocw_ref) = refs[pgs:]
    j = pl.program_id(1)
    qbd = qbd_ref[0]
    ncol = qbd.shape[1]
    qcols = ncol // (N_KV * GQA)
    g = gt_ref[0]

    @pl.when(j == 0)
    def _():
        p_c = _softmax_rows(_dot(kc_ref[0], qbd) + cbias_ref[...])
        o_c = _dot(vct_ref[0], p_c.astype(BF16))
        imp = _split_dot(_split_dot_l(cover_ref[...], p_c), rsum_ref[...])
        lane = lax.broadcasted_iota(jnp.int32, (1, ncol), 1)
        qpos = past + (lane % qcols) % ds
        msk_ref[...] = _select_mask(imp, qpos) + sfar_ref[...]
        kw = win_ref[0, :, 0:KV_W].astype(BF16)
        vw = win_ref[0, :, KV_W:KV_ROW].astype(BF16)
        p_w = _softmax_rows(_dot(kw, qbd) + wbias_ref[...])
        o_w = _dot_t(vw, p_w.astype(BF16))
        ocw_ref[...] = g[0:1, :] * o_c + g[2:3, :] * o_w
        m_ref[...] = jnp.full(m_ref.shape, -jnp.inf, F32)
        acc_ref[...] = jnp.zeros(acc_ref.shape, F32)

    last = j == pl.num_programs(1) - 1
    blocks, vts = [], []
    for k, pr in enumerate(page_refs):
        pidx = j * pgs + k
        kk = pr[0, 0:KV_W, :].T.astype(BF16)
        vts.append(pr[0, KV_W:KV_ROW, :].astype(BF16))
        s = _dot(kk, qbd)
        if k == pgs - 1:
            s = s + jnp.where(last, slast_ref[...], 0.0)
        for hb in range(2):
            blocks.append(s[hb * SEL_BLOCK:(hb + 1) * SEL_BLOCK, :]
                          + msk_ref[pl.ds(2 * pidx + hb, 1), :])
    top = blocks[0]
    for s in blocks[1:]:
        top = jnp.maximum(top, s)
    m = m_ref[...]
    m_new = jnp.maximum(m, jnp.max(top, axis=0, keepdims=True))
    p = jnp.concatenate([jnp.exp2(s - m_new).astype(BF16) for s in blocks], axis=0)
    vt = jnp.concatenate([jnp.concatenate(vts, axis=1), jnp.ones((16, pgs * PAGE_SIZE), BF16)], axis=0)
    acc_ref[...] = jnp.exp2(m - m_new) * acc_ref[...] + _dot(vt, p)
    m_ref[...] = m_new

    @pl.when(last)
    def _():
        kn = new_ref[0, :, 0:KV_W].astype(BF16)
        vn = new_ref[0, :, KV_W:KV_ROW].astype(BF16)
        s = _dot(kn, qbd) + snew_ref[...] + msk_ref[pl.ds(2 * npg, 1), :]
        m = m_ref[...]
        m_new = jnp.maximum(m, jnp.max(s, axis=0, keepdims=True))
        p = jnp.exp2(s - m_new).astype(BF16)
        acc = jnp.exp2(m - m_new) * acc_ref[...]
        num = acc[0:KV_W] + _dot_t(vn, p)
        den = acc[KV_W:KV_W + 1] + jnp.sum(p.astype(F32), axis=0, keepdims=True)
        o = ocw_ref[...] + g[1:2, :] * (num * (1.0 / den))
        for gi in range(N_KV):
            o_ref[0, gi] = o[gi * HEAD_DIM:(gi + 1) * HEAD_DIM, gi * GQA * qcols:(gi + 1) * GQA * qcols]


def _nsa_sample(table, pages, qbd, gt, kc, vct, win, new, cbias, wbias, sfar, slast, snew,
                cover_t, rsum, past, ds):
    nb, npg = table.shape
    pgs = min(64, npg)
    assert npg % pgs == 0
    ncol = qbd.shape[2]
    ns = cover_t.shape[0]

    def page_spec(k):
        return pl.BlockSpec((1, KV_ROW, PAGE_SIZE), lambda b, j, tbl: (tbl[b, j * pgs + k], 0, 0))

    per_b = lambda a: pl.BlockSpec((1,) + a.shape[1:], lambda b, j, tbl: (b,) + (0,) * (a.ndim - 1))
    full = lambda a: pl.BlockSpec(a.shape, lambda b, j, tbl: (0,) * a.ndim)
    grid_spec = pltpu.PrefetchScalarGridSpec(
        num_scalar_prefetch=1,
        grid=(nb, npg // pgs),
        in_specs=[page_spec(k) for k in range(pgs)]
        + [per_b(a) for a in (qbd, gt, kc, vct, win, new)]
        + [full(a) for a in (cbias, wbias, sfar, slast, snew, cover_t, rsum)],
        out_specs=pl.BlockSpec((1, N_KV, HEAD_DIM, ncol // N_KV), lambda b, j, tbl: (b, 0, 0, 0)),
        scratch_shapes=[pltpu.VMEM((ns, ncol), F32), pltpu.VMEM((1, ncol), F32),
                        pltpu.VMEM((KV_W + 16, ncol), F32), pltpu.VMEM((KV_W, ncol), F32)],
    )
    return pl.pallas_call(
        functools.partial(_nsa_sample_kernel, pgs=pgs, npg=npg, past=past, ds=ds),
        grid_spec=grid_spec,
        out_shape=jax.ShapeDtypeStruct((nb, N_KV, HEAD_DIM, ncol // N_KV), F32),
        compiler_params=pltpu.CompilerParams(dimension_semantics=("parallel", "arbitrary"),
                                             vmem_limit_bytes=VMEM_LIMIT),
        name="nsa_sample",
    )(table, *([pages] * pgs), qbd, gt, kc, vct, win, new, cbias, wbias, sfar, slast, snew,
      cover_t, rsum)


def _cover_t(nc, ns, nc_pad, ns_pad):
    c0 = np.arange(nc) * CMP_STRIDE
    s0 = np.arange(ns) * SEL_BLOCK
    m = (c0[None, :] < s0[:, None] + SEL_BLOCK) & (c0[None, :] + CMP_BLOCK > s0[:, None])
    out = np.zeros((ns_pad, nc_pad), np.float32)
    out[:ns, :nc] = m
    return jnp.asarray(out, BF16)


def _prep_weights(norm_g, w_in, ln_v_g, spatial_w, spatial_b, cmp_pe, cmp_w1, cmp_b1, cmp_w2, cmp_b2,
                  w_out, final_g):
    offs = np.cumsum((D_A, D_A, D_A, D_B, 2 * KV_W, 2 * KV_W, 2 * KV_W, 3 * N_HEADS, D_B))
    g0, g1 = int(offs[6]), int(offs[7])
    w_perm = jnp.concatenate(
        [w_in[:, :g0], w_in[:, g1:], w_in[:, g0:g1],
         jnp.zeros((D_MODEL, GATE_PAD - 3 * N_HEADS), w_in.dtype)], axis=1).astype(BF16)
    pavg = jnp.asarray(np.kron(np.eye(A_GROUPS), np.full((HEAD_DIM, HEAD_DIM), 1.0 / HEAD_DIM)), BF16)
    w1cat = jnp.concatenate([cmp_w1[:, s].reshape(2, CMP_FLAT, CMP_HIDDEN) for s in range(CMP_R)],
                            axis=2).astype(BF16)
    pe8 = jnp.concatenate([cmp_pe.reshape(2, CMP_R, CMP_FLAT),
                           jnp.zeros((2, 8 - CMP_R, CMP_FLAT), F32)], axis=1)
    return dict(
        norm_g=norm_g.reshape(1, D_MODEL), w_perm=w_perm, ln_g=ln_v_g.reshape(1, D_A), pavg=pavg,
        w1cat=w1cat, pe8=pe8, b1=cmp_b1.reshape(2, 1, CMP_HIDDEN), w2=cmp_w2.astype(BF16),
        b2=cmp_b2.reshape(2, 1, HEAD_DIM), wo=w_out.astype(BF16), fg=final_g.reshape(1, D_MODEL),
        spatial_w=spatial_w, spatial_b=spatial_b)


def _spatial_operands(spatial_w, spatial_b, n):
    reps = CHUNK // n
    w = jnp.tril(spatial_w[:, :n, :n])
    eye = jnp.eye(reps, dtype=w.dtype)
    wsp = jnp.einsum('ab,gts->gatbs', eye, w).reshape(A_GROUPS, CHUNK, CHUNK).astype(BF16)
    b = jnp.tile(spatial_b[:, :n].T, (reps, 1))
    bsp = jnp.repeat(b, HEAD_DIM, axis=1)
    return wsp, bsp


def _prompt_bias_buckets():
    ql = np.arange(Q_BLOCK)[None, :]
    cl = np.arange(16)[:, None]
    band = np.stack([ql - CMP_STRIDE * cl - (CMP_BLOCK - 1),
                     ql + 97 - CMP_STRIDE * cl])
    kl = np.arange(CHUNK)[:, None]
    seld = np.stack([np.full((CHUNK, Q_BLOCK), FAR_DIST), CHUNK + ql - kl, ql - kl])
    wl = np.arange(WINDOW + Q_BLOCK)[:, None]
    dw = np.stack([Q_BLOCK * v + ql - wl for v in range(WIN_VARIANTS)])
    dw = np.where(dw < WINDOW, dw, -1)
    far = np.full((8, Q_BLOCK), FAR_DIST)
    tile4 = lambda d: np.tile(_t5_bucket_np(d.reshape(-1, Q_BLOCK)), (1, GQA))
    return tile4(band), tile4(seld), tile4(dw), tile4(far)


def _sample_bias_buckets(past, ds, qpad, ncp, nwin_pad):
    ql = (np.arange(qpad) % ds)[None, :]
    c = np.arange(ncp)[:, None]
    dc = past + ql - (CMP_STRIDE * c + CMP_BLOCK - 1)
    wl = np.arange(nwin_pad)[:, None]
    dw = WINDOW + ql - wl
    dw = np.where((dw < WINDOW) & (wl < WINDOW + ds), dw, -1)
    kl = np.arange(PAGE_SIZE)[:, None]
    dlast = PAGE_SIZE + ql - kl
    nl = np.arange(16)[:, None]
    dnew = np.where(nl < ds, ql - nl, -1)
    far = np.full((8, qpad), FAR_DIST)
    tile4 = lambda d: np.tile(_t5_bucket_np(d), (1, GQA))
    return tile4(dc), tile4(dw), tile4(dlast), tile4(dnew), tile4(far)


def kernel(x_prompt, x_sample, cache_cmp_kv, cache_sel_kv, state_win_kv, page_table, norm_g, w_in,
           ln_v_g, spatial_w, spatial_b, cmp_pe, cmp_w1, cmp_b1, cmp_w2, cmp_b2, rel_table, w_out,
           final_g):
    depth = norm_g.shape[0]
    assert depth == 1
    bsz, seq = x_prompt.shape[:2]
    db, ds = x_sample.shape[:2]
    npg = page_table.shape[1]
    past = npg * PAGE_SIZE
    win_buf = state_win_kv.shape[2]
    assert win_buf == WINDOW and past >= WINDOW and seq % Q_BLOCK == 0 and seq >= WINDOW + Q_BLOCK
    assert CHUNK % ds == 0 and ds <= 8 and (db * ds) % CHUNK == 0
    nblk = seq // Q_BLOCK
    l = 0
    wts = _prep_weights(norm_g[l], w_in[l], ln_v_g[l], spatial_w[l], spatial_b[l], cmp_pe[l],
                        cmp_w1[l], cmp_b1[l], cmp_w2[l], cmp_b2[l], w_out[l], final_g)
    inproj = lambda x, nb: _inproj(x, nb, wts['norm_g'], wts['w_perm'], wts['ln_g'], wts['pavg'])
    compress = lambda tbl, pages: _compress(tbl, pages, wts['w1cat'], wts['pe8'], wts['b1'],
                                            wts['w2'], wts['b2'])
    kv_out = lambda a: a.reshape(a.shape[0], 2, N_KV, HEAD_DIM, a.shape[2]).transpose(0, 4, 1, 2, 3)[None]
    pages_t = lambda c: c.transpose(0, 2, 3, 4, 1).reshape(c.shape[0], KV_ROW, PAGE_SIZE)

    n_p = bsz * seq
    xp = x_prompt.reshape(n_p, D_MODEL)
    ua, _, vnb, zas, zbs, kvc, kvs, kvw, qt, gt, ksk, vst, kwk, vwt = inproj(xp, bsz)
    kc, vct = compress(None, kvc)
    ncp = seq // CMP_STRIDE
    cover_p = _cover_t(ncp - CMP_R + 1, seq // SEL_BLOCK, ncp, seq // SEL_BLOCK)
    band_b, seld_b, win_b, far_b = _prompt_bias_buckets()
    cols = GQA * Q_BLOCK
    cband = _bias_tiles(rel_table, band_b, Q_BLOCK).reshape(N_KV, 2, 16, cols)
    seld = _bias_tiles(rel_table, seld_b, Q_BLOCK, rel_far=True).reshape(N_KV, 3, CHUNK, cols)
    wbias = _bias_tiles(rel_table, win_b, Q_BLOCK).reshape(N_KV, WIN_VARIANTS, WINDOW + Q_BLOCK, cols)
    far = _bias_tiles(rel_table, far_b, Q_BLOCK)[:, 0:1]
    ob = _nsa_prompt(qt, gt, kc, vct, ksk, vst, kwk, vwt, cband, far, seld, wbias, cover_p, bsz, seq)
    wsp, bsp = _spatial_operands(wts['spatial_w'], wts['spatial_b'], CHUNK)
    y_prompt = _mixout(xp, ua, vnb, zas, ob, zbs, wsp, bsp, wts['wo'], wts['fg']).reshape(bsz, seq, D_MODEL)
    new_cmp_p = kv_out(kvc)
    new_sel_p = kv_out(kvs)
    new_win_p = kv_out(kvw[:, :, seq - win_buf:])

    n_s = db * ds
    xs = x_sample.reshape(n_s, D_MODEL)
    ua, vn, vnb, zas, zbs, kvc, kvs, kvw, qt, gt, _, _, _, _ = inproj(xs, 1)
    kvc, kvs, kvw = (a[0].T for a in (kvc, kvs, kvw))
    kc, vct = compress(page_table, pages_t(cache_cmp_kv[l]))
    ncs = past // CMP_STRIDE
    t_all = past + ds
    nss = -(-t_all // SEL_BLOCK)
    nss_pad = -(-nss // 8) * 8
    cover_s = _cover_t(t_all // CMP_STRIDE - CMP_R + 1, nss, ncs, nss_pad)
    qpad = 16
    ncol = N_KV * GQA * qpad
    nwin_pad = -(-(win_buf + ds) // 16) * 16
    dc_b, dw_b, dl_b, dn_b, far_b = _sample_bias_buckets(past, ds, qpad, ncs, nwin_pad)
    both = lambda t: jnp.concatenate([t[0], t[1]], axis=1)
    cbias_s = both(_bias_tiles(rel_table, dc_b, qpad))
    wbias_s = both(_bias_tiles(rel_table, dw_b, qpad))
    slast = both(_bias_tiles(rel_table, dl_b, qpad, rel_far=True))
    snew = both(_bias_tiles(rel_table, dn_b, qpad, rel_far=True))
    sfar = both(_bias_tiles(rel_table, far_b, qpad))[0:1]

    qg = qt.reshape(N_KV, GQA, HEAD_DIM, db, ds).transpose(3, 0, 2, 1, 4)
    qg = jnp.tile(qg, (1, 1, 1, 1, qpad // ds)).reshape(db, N_KV, HEAD_DIM, GQA * qpad)
    zq = jnp.zeros_like(qg[:, 0])
    qbd = jnp.concatenate([jnp.concatenate([qg[:, 0], zq], axis=2),
                           jnp.concatenate([zq, qg[:, 1]], axis=2)], axis=1)
    gts = gt[:3 * N_HEADS].reshape(N_KV, GQA, 3, db, ds).transpose(3, 2, 0, 1, 4)
    gts = jnp.tile(gts, (1, 1, 1, 1, qpad // ds)).reshape(db, 3, ncol)
    win = jnp.concatenate([state_win_kv[l].reshape(db, win_buf, KV_ROW),
                           kvw.reshape(db, ds, KV_ROW)], axis=1)
    win_pad = jnp.pad(win, ((0, 0), (0, nwin_pad - win_buf - ds), (0, 0)))
    new_pad = jnp.pad(kvs.reshape(db, ds, KV_ROW), ((0, 0), (0, 16 - ds), (0, 0)))
    rsum = jnp.asarray(np.kron(np.eye(N_KV), np.kron(np.ones((GQA, GQA)), np.eye(qpad))), BF16)
    o_s = _nsa_sample(page_table, pages_t(cache_sel_kv[l]), qbd, gts,
                      kc, vct, win_pad, new_pad, cbias_s, wbias_s, sfar, slast, snew,
                      cover_s, rsum, past, ds)
    ob = (o_s.reshape(db, N_KV, HEAD_DIM, GQA, qpad)[..., :ds].transpose(0, 4, 1, 3, 2)
          .reshape(n_s, D_B))
    wsp, bsp = _spatial_operands(wts['spatial_w'], wts['spatial_b'], ds)
    y_sample = _mixout(xs, ua, vnb, zas, ob, zbs, wsp, bsp, wts['wo'], wts['fg']).reshape(db, ds, D_MODEL)
    kv5 = lambda a: a.reshape(1, db, ds, 2, N_KV, HEAD_DIM)
    new_cmp_s = kv5(kvc)
    new_sel_s = kv5(kvs)
    new_win_s = win[:, ds:].reshape(1, db, win_buf, 2, N_KV, HEAD_DIM)
    new_chunk_v = vn.reshape(1, db, ds, D_A)
    return (y_prompt, y_sample, new_cmp_p, new_sel_p, new_win_p, new_cmp_s, new_sel_s, new_win_s,
            new_chunk_v)
```

```python
import functools
import math

import numpy as np
import jax
import jax.numpy as jnp
from jax import lax
from jax.experimental import pallas as pl
from jax.experimental.pallas import tpu as pltpu

F32 = jnp.float32
BF16 = jnp.bfloat16

D_MODEL = 1024
HEAD_DIM = 64
D_A = 512
D_B = 512
A_GROUPS = D_A // HEAD_DIM
CHUNK = 128
N_HEADS = D_B // HEAD_DIM
N_KV = 2
GQA = N_HEADS // N_KV
KV_W = N_KV * HEAD_DIM
KV_ROW = 2 * KV_W
CMP_STRIDE = 16
CMP_BLOCK = 32
CMP_R = CMP_BLOCK // CMP_STRIDE
CMP_HIDDEN = 256
CMP_FLAT = CMP_STRIDE * HEAD_DIM
SEL_BLOCK = 64
N_SEL = 16
WINDOW = 512
N_BUCKETS = 32
MAX_DISTANCE = 128
Q_BLOCK = 128
PAGE_SIZE = 128
RMS_EPS = 1e-6
LN_EPS = 1e-5
NEG = -1e30
FORCE_BONUS = 1e6
LOG2E = 1.4426950408889634
Q_SCALE = HEAD_DIM ** -0.5 * LOG2E
FAR_DIST = 1 << 20
WIN_VARIANTS = WINDOW // Q_BLOCK + 1

_OFF_U, _OFF_V, _OFF_ZA, _OFF_Q = 0, 512, 1024, 1536
_OFF_KVC, _OFF_KVS, _OFF_KVW, _OFF_ZB, _OFF_G = 2048, 2304, 2560, 2816, 3328
D_IN_PAD = 3456
GATE_PAD = 128
GATE_ROWS = 32

VMEM_LIMIT = 52 * 1024 * 1024


def _gelu(x):
    return x * (0.5 * (1.0 + jnp.tanh(0.7978845608028654 * (x + 0.044715 * (x * x * x)))))


def _sigmoid(x):
    return 1.0 / (1.0 + jnp.exp(-x))


def _dot(a, b):
    return jnp.dot(a, b, preferred_element_type=F32)


def _dot_t(a, b):
    return lax.dot_general(a, b, (((0,), (0,)), ((), ())), preferred_element_type=F32)


def _dot_nt(a, b):
    return lax.dot_general(a, b, (((1,), (1,)), ((), ())), preferred_element_type=F32)


def _split_dot(a, b):
    hi = a.astype(BF16)
    lo = (a - hi.astype(F32)).astype(BF16)
    return _dot(hi, b) + _dot(lo, b)


def _split_dot_l(a, b):
    hi = b.astype(BF16)
    lo = (b - hi.astype(F32)).astype(BF16)
    return _dot(a, hi) + _dot(a, lo)


def _inproj_kernel(x_ref, ng_ref, w_ref, lng_ref, pavg_ref,
                   ua_ref, vn_ref, vnb_ref, zas_ref, zbs_ref, kvc_ref, kvs_ref, kvw_ref,
                   qt_ref, gt_ref, ksk_ref, vst_ref, kwk_ref, vwt_ref):
    x = x_ref[...]
    ms = jnp.mean(x * x, axis=-1, keepdims=True)
    h = (x * lax.rsqrt(ms + RMS_EPS) * ng_ref[...]).astype(BF16)

    def proj(a, b):
        return _dot(h, w_ref[:, a:b])

    ua_ref[...] = _gelu(proj(_OFF_U, _OFF_V)).astype(BF16)
    v = _gelu(proj(_OFF_V, _OFF_ZA))
    mu = _dot(v.astype(BF16), pavg_ref[...])
    d = v - mu
    var = _dot((d * d).astype(BF16), pavg_ref[...])
    vn = d * lax.rsqrt(var + LN_EPS) * lng_ref[...]
    vn_ref[...] = vn
    vnb_ref[...] = vn.astype(BF16)
    za = proj(_OFF_ZA, _OFF_Q)
    zas_ref[...] = (za * _sigmoid(za)).astype(BF16)
    zb = proj(_OFF_ZB, _OFF_G)
    zbs_ref[...] = (zb * _sigmoid(zb)).astype(BF16)
    qt_ref[...] = (proj(_OFF_Q, _OFF_KVC) * Q_SCALE).T.astype(BF16)
    gt_ref[...] = _sigmoid(proj(_OFF_G, D_IN_PAD)).T[0:GATE_ROWS, :]
    kvc_ref[0] = proj(_OFF_KVC, _OFF_KVS).T
    kvs = proj(_OFF_KVS, _OFF_KVW)
    kvs_t = kvs.T
    kvs_ref[0] = kvs_t
    ksk_ref[...] = kvs[:, 0:KV_W].astype(BF16)
    vst_ref[...] = kvs_t[KV_W:KV_ROW, :].astype(BF16)
    kvw = proj(_OFF_KVW, _OFF_ZB)
    kvw_t = kvw.T
    kvw_ref[0] = kvw_t
    kwk_ref[...] = kvw[:, 0:KV_W].astype(BF16)
    vwt_ref[...] = kvw_t[KV_W:KV_ROW, :].astype(BF16)


def _inproj(x, nb, norm_g, w_perm, ln_g, pavg):
    n = x.shape[0]
    s = n // nb
    tm = min(1024, s)
    assert s % tm == 0
    per = s // tm
    row = lambda w: pl.BlockSpec((tm, w), lambda i: (i, 0))
    col = lambda h: pl.BlockSpec((h, tm), lambda i: (0, i))
    kvt = pl.BlockSpec((1, KV_ROW, tm), lambda i: (i // per, 0, i % per))
    full = lambda a: pl.BlockSpec(a.shape, lambda i: (0,) * a.ndim)
    rows = [(D_A, BF16), (D_A, F32), (D_A, BF16), (D_A, BF16), (D_B, BF16)]
    sds = lambda shape, dt: jax.ShapeDtypeStruct(shape, dt)
    out_specs = ([row(w) for w, _ in rows] + [kvt, kvt, kvt]
                 + [col(D_B), col(GATE_ROWS), row(KV_W), col(KV_W), row(KV_W), col(KV_W)])
    out_shape = ([sds((n, w), dt) for w, dt in rows] + [sds((nb, KV_ROW, s), F32)] * 3
                 + [sds((D_B, n), BF16), sds((GATE_ROWS, n), F32), sds((n, KV_W), BF16),
                    sds((KV_W, n), BF16), sds((n, KV_W), BF16), sds((KV_W, n), BF16)])
    return pl.pallas_call(
        _inproj_kernel,
        grid=(n // tm,),
        in_specs=[row(D_MODEL), full(norm_g), full(w_perm), full(ln_g), full(pavg)],
        out_specs=out_specs,
        out_shape=out_shape,
        compiler_params=pltpu.CompilerParams(dimension_semantics=("parallel",),
                                             vmem_limit_bytes=VMEM_LIMIT),
        name="inproj",
    )(x, norm_g, w_perm, ln_g, pavg)


def _mixout_kernel(x_ref, ua_ref, vn_ref, zas_ref, ob_ref, zbs_ref, wsp_ref, bsp_ref, wo_ref, fg_ref,
                   y_ref, s_ref):
    tm = x_ref.shape[0]
    for c in range(tm // CHUNK):
        rows = slice(c * CHUNK, (c + 1) * CHUNK)
        vc = vn_ref[rows, :]
        for g in range(A_GROUPS):
            cols = slice(g * HEAD_DIM, (g + 1) * HEAD_DIM)
            s_ref[rows, cols] = _dot(wsp_ref[g], vc[:, cols])
        s_ref[rows, :] = s_ref[rows, :] + bsp_ref[...]
    mix_a = (ua_ref[...].astype(F32) * s_ref[...] * zas_ref[...].astype(F32)).astype(BF16)
    mix_b = (ob_ref[...].astype(F32) * zbs_ref[...].astype(F32)).astype(BF16)
    y = x_ref[...] + _dot(mix_a, wo_ref[0:D_A, :]) + _dot(mix_b, wo_ref[D_A:D_A + D_B, :])
    ms = jnp.mean(y * y, axis=-1, keepdims=True)
    y_ref[...] = y * lax.rsqrt(ms + RMS_EPS) * fg_ref[...]


def _mixout(x, ua, vn, zas, ob, zbs, wsp, bsp, wo, fg):
    n = x.shape[0]
    tm = min(512, n)
    row = lambda w: pl.BlockSpec((tm, w), lambda i: (i, 0))
    full = lambda a: pl.BlockSpec(a.shape, lambda i: (0,) * a.ndim)
    return pl.pallas_call(
        _mixout_kernel,
        grid=(n // tm,),
        in_specs=[row(D_MODEL), row(D_A), row(D_A), row(D_A), row(D_B), row(D_B),
                  full(wsp), full(bsp), full(wo), full(fg)],
        out_specs=row(D_MODEL),
        out_shape=jax.ShapeDtypeStruct((n, D_MODEL), F32),
        scratch_shapes=[pltpu.VMEM((tm, D_A), F32)],
        compiler_params=pltpu.CompilerParams(dimension_semantics=("parallel",),
                                             vmem_limit_bytes=VMEM_LIMIT),
        name="mixout",
    )(x, ua, vn, zas, ob, zbs, wsp, bsp, wo, fg)


def _t5_bucket_np(dist):
    dist = np.asarray(dist, np.int64)
    n = np.maximum(dist, 0)
    max_exact = N_BUCKETS // 2
    nf = np.maximum(n, max_exact).astype(np.float64)
    large = max_exact + (np.log(nf / max_exact) / math.log(MAX_DISTANCE / max_exact)
                         * (N_BUCKETS - max_exact)).astype(np.int64)
    b = np.where(n < max_exact, n, np.minimum(large, N_BUCKETS - 1))
    return np.where(dist < 0, -1, b).astype(np.int32)


def _bias_kernel(tbl_ref, bkt_ref, out_ref, *, qb, rel_far):
    b = bkt_ref[...]
    grp = lax.broadcasted_iota(jnp.int32, (1, b.shape[1]), 1) // qb

    def head_row(k, g):
        row = jnp.zeros(grp.shape, F32)
        for r in range(GQA):
            row = jnp.where(grp == r, tbl_ref[k * N_HEADS + g * GQA + r], row)
        return row * LOG2E

    for g in range(N_KV):
        acc = jnp.full(b.shape, NEG, F32)
        base = head_row(N_BUCKETS - 1, g) if rel_far else None
        for k in range(N_BUCKETS):
            row = head_row(k, g)
            if rel_far:
                row = row - base
            acc = jnp.where(b == k, row, acc)
        out_ref[g] = acc


def _bias_tiles(rel_table, bkt, qb, rel_far=False):
    r, c = bkt.shape
    rb = r
    for cand in (512, 256, 128, 64, 32, 16, 8):
        if r % cand == 0:
            rb = cand
            break
    return pl.pallas_call(
        functools.partial(_bias_kernel, qb=qb, rel_far=rel_far),
        grid=(r // rb,),
        in_specs=[pl.BlockSpec(memory_space=pltpu.SMEM), pl.BlockSpec((rb, c), lambda i: (i, 0))],
        out_specs=pl.BlockSpec((N_KV, rb, c), lambda i: (0, i, 0)),
        out_shape=jax.ShapeDtypeStruct((N_KV, r, c), F32),
        compiler_params=pltpu.CompilerParams(dimension_semantics=("parallel",)),
        name="bias_tiles",
    )(rel_table.reshape(-1), jnp.asarray(bkt))


def _compress_kernel(tbl_ref, *refs, pg):
    del tbl_ref
    page_refs = refs[:pg + 1]
    perm_ref, w1_ref, pe_ref, b1_ref, w2_ref, b2_ref, kc_ref, vct_ref, x_ref = refs[pg + 1:]
    mp = (pg + 1) * 8
    nrow = pg * 8
    left = lax.broadcasted_iota(jnp.int32, (8, 128), 1) < HEAD_DIM
    perm = perm_ref[...]
    for k, pr in enumerate(page_refs):
        tok = _dot_nt(perm, pr[0].astype(BF16))
        for kv in range(2):
            for t in range(CMP_STRIDE // 2):
                e = tok[16 * t:16 * t + 8, kv * KV_W:(kv + 1) * KV_W]
                o = tok[16 * t + 8:16 * t + 16, kv * KV_W:(kv + 1) * KV_W]
                sw = pltpu.roll(jnp.where(left, o, e), HEAD_DIM, 1)
                x_ref[kv, 8 * k:8 * k + 8, 128 * t:128 * t + 128] = jnp.where(left, e, sw)
                x_ref[kv, mp + 8 * k:mp + 8 * k + 8, 128 * t:128 * t + 128] = jnp.where(left, sw, o)
    outs = []
    for kv in range(2):
        x_ref[kv, 2 * mp:2 * mp + 8, :] = pe_ref[kv]
        p = _dot(x_ref[kv].astype(BF16), w1_ref[kv])
        hc = (b1_ref[kv] + p[2 * mp:2 * mp + 1, 0:CMP_HIDDEN]
              + p[2 * mp + 1:2 * mp + 2, CMP_HIDDEN:2 * CMP_HIDDEN])
        per_g = []
        for g in range(N_KV):
            base = g * mp
            h = (p[base:base + nrow, 0:CMP_HIDDEN]
                 + p[base + 1:base + nrow + 1, CMP_HIDDEN:2 * CMP_HIDDEN] + hc)
            per_g.append(_dot(_gelu(h).astype(BF16), w2_ref[kv]) + b2_ref[kv])
        outs.append(jnp.concatenate(per_g, axis=1))
    kc_ref[0] = outs[0].astype(BF16)
    vct_ref[0] = outs[1].T.astype(BF16)


def _compress(table, pages, w1cat, pe8, b1, w2, b2):
    if table is None:
        nb, npg = pages.shape[0], pages.shape[2] // PAGE_SIZE
        table = jnp.zeros((1, 1), jnp.int32)
        index = lambda b, idx, tbl: (b, 0, idx)
    else:
        nb, npg = table.shape
        index = lambda b, idx, tbl: (tbl[b, idx], 0, 0)
    pg = min(64, npg)
    assert npg % pg == 0
    mp = (pg + 1) * 8

    def page_spec(k):
        return pl.BlockSpec(
            (1, KV_ROW, PAGE_SIZE),
            lambda b, j, tbl: index(b, jnp.minimum(j * pg + k, npg - 1), tbl))

    tok = np.arange(PAGE_SIZE)
    perm_np = np.zeros((PAGE_SIZE, PAGE_SIZE), np.float32)
    perm_np[(tok % CMP_STRIDE) * (PAGE_SIZE // CMP_STRIDE) + tok // CMP_STRIDE, tok] = 1.0
    perm = jnp.asarray(perm_np, BF16)
    full = lambda a: pl.BlockSpec(a.shape, lambda b, j, tbl: (0,) * a.ndim)
    grid_spec = pltpu.PrefetchScalarGridSpec(
        num_scalar_prefetch=1,
        grid=(nb, npg // pg),
        in_specs=[page_spec(k) for k in range(pg + 1)]
        + [full(a) for a in (perm, w1cat, pe8, b1, w2, b2)],
        out_specs=[pl.BlockSpec((1, pg * 8, KV_W), lambda b, j, tbl: (b, j, 0)),
                   pl.BlockSpec((1, KV_W, pg * 8), lambda b, j, tbl: (b, 0, j))],
        scratch_shapes=[pltpu.VMEM((2, 2 * mp + 8, CMP_FLAT), F32)],
    )
    return pl.pallas_call(
        functools.partial(_compress_kernel, pg=pg),
        grid_spec=grid_spec,
        out_shape=[jax.ShapeDtypeStruct((nb, npg * 8, KV_W), BF16),
                   jax.ShapeDtypeStruct((nb, KV_W, npg * 8), BF16)],
        compiler_params=pltpu.CompilerParams(dimension_semantics=("parallel", "parallel"),
                                             vmem_limit_bytes=VMEM_LIMIT),
        name="compress",
    )(table, *([pages] * (pg + 1)), perm, w1cat, pe8, b1, w2, b2)


def _topk_rows(imp, n_sel):
    ns = imp.shape[0]
    blk = lax.broadcasted_iota(jnp.int32, imp.shape, 0).astype(F32)
    for _ in range(n_sel):
        mx = jnp.max(imp, axis=0, keepdims=True)
        idx = jnp.min(jnp.where(imp == mx, blk, float(ns)), axis=0, keepdims=True)
        imp = jnp.where(blk == idx, -jnp.inf, imp)
    return imp == -jnp.inf


def _select_mask(imp, qpos):
    blk = lax.broadcasted_iota(jnp.int32, imp.shape, 0)
    cur = qpos // SEL_BLOCK
    forced = (blk == 0) | (blk == cur) | (blk == cur - 1)
    valid = blk * SEL_BLOCK <= qpos
    imp = imp + jnp.where(forced, FORCE_BONUS, 0.0)
    imp = jnp.where(valid, imp, NEG)
    sel = _topk_rows(imp, N_SEL)
    return jnp.where(sel & valid, 0.0, NEG)


def _softmax_rows(s):
    m = jnp.max(s, axis=0, keepdims=True)
    e = jnp.exp2(s - m)
    inv = jnp.where(m > NEG / 2, 1.0 / jnp.sum(e, axis=0, keepdims=True), 0.0)
    return e * inv


def _online_update(s, v_dot, m, l, acc):
    m_new = jnp.maximum(m, jnp.max(s, axis=0, keepdims=True))
    alpha = jnp.exp2(m - m_new)
    p = jnp.exp2(s - m_new)
    l = alpha * l + jnp.sum(p, axis=0, keepdims=True)
    acc = alpha * acc + v_dot(p.astype(BF16))
    return m_new, l, acc


def _nsa_prompt_kernel(qt_ref, gt_ref, kc_ref, vct_ref, ks_ref, vst_ref, kw_ref, vwt_ref,
                       cband_ref, far_ref, seld_ref, wbias_ref, cover_ref,
                       o_ref, bias_ref, msk_ref, s_ref, *, nsub):
    i = pl.program_id(1)
    cols = GQA * Q_BLOCK
    ncp = kc_ref.shape[1]
    gw = GQA * HEAD_DIM
    qpos = i * Q_BLOCK + lax.broadcasted_iota(jnp.int32, (1, Q_BLOCK), 1)
    n_trips = i // nsub + 1
    tk = nsub * CHUNK
    c0 = pl.multiple_of(jnp.maximum(i * 8 - 8, 0), 8)
    crow = lax.broadcasted_iota(jnp.int32, (ncp, 1), 0)
    woff = pl.multiple_of(jnp.maximum(i * Q_BLOCK - WINDOW, 0), Q_BLOCK)
    nwk = WINDOW + Q_BLOCK
    half = cols // 2

    def query(g):
        qblk = qt_ref[g * gw:(g + 1) * gw, :]
        q64 = jnp.concatenate([qblk[r * HEAD_DIM:(r + 1) * HEAD_DIM, :] for r in range(GQA)], axis=1)
        zero = jnp.zeros_like(q64)
        return jnp.concatenate([q64, zero] if g == 0 else [zero, q64], axis=0)

    def compressed(g, q):
        far = far_ref[g]
        bias_ref[g] = jnp.where(crow < c0, far, NEG)
        bias_ref[g, pl.ds(c0, 16), :] = cband_ref[g, jnp.minimum(i, 1)]
        vc = jnp.concatenate([vct_ref[0, g * HEAD_DIM:(g + 1) * HEAD_DIM, :],
                              jnp.ones((16, ncp), BF16)], axis=0)
        o_parts, imp = [], None
        for h in range(2):
            hc = slice(h * half, (h + 1) * half)
            s_c = _dot(kc_ref[0], q[:, hc]) + bias_ref[g, :, hc]
            m_c = jnp.max(s_c, axis=0, keepdims=True)
            e_c = jnp.exp2(s_c - m_c).astype(BF16)
            a_c = _dot(vc, e_c)
            inv = jnp.where(m_c > NEG / 2, 1.0 / a_c[HEAD_DIM:HEAD_DIM + 1], 0.0)
            o_parts.append(a_c[0:HEAD_DIM] * inv)
            w = _dot(cover_ref[...], e_c) * inv
            part = w[:, 0:Q_BLOCK] + w[:, Q_BLOCK:2 * Q_BLOCK]
            imp = part if imp is None else imp + part
        return jnp.concatenate(o_parts, axis=1), imp

    def window(g, q):
        vw = jnp.concatenate([vwt_ref[g * HEAD_DIM:(g + 1) * HEAD_DIM, pl.ds(woff, nwk)],
                              jnp.ones((16, nwk), BF16)], axis=0)
        kw = kw_ref[pl.ds(woff, nwk), :]
        parts = []
        for h in range(2):
            hc = slice(h * half, (h + 1) * half)
            s_w = _dot(kw, q[:, hc]) + wbias_ref[g, 0, :, hc]
            e_w = jnp.exp2(s_w - jnp.max(s_w, axis=0, keepdims=True)).astype(BF16)
            a_w = _dot(vw, e_w)
            parts.append(a_w[0:HEAD_DIM] * (1.0 / a_w[HEAD_DIM:HEAD_DIM + 1]))
        return jnp.concatenate(parts, axis=1)

    def select(g, imp):
        mask = _select_mask(imp, qpos)
        msk_ref[g] = jnp.concatenate([mask] * GQA, axis=1) + far_ref[g]

    def scores(g, q, t, slot):
        t = jnp.minimum(t, n_trips - 1)
        koff = pl.multiple_of(t * tk, tk)
        s = _dot(ks_ref[pl.ds(koff, tk), :], q)
        for j in range(2 * nsub):
            blk = slice(j * SEL_BLOCK, (j + 1) * SEL_BLOCK)
            s_ref[g, slot, blk, :] = s[blk] + msk_ref[g, pl.ds(2 * nsub * t + j, 1), :]

    def attend(g, t, slot, near, carry):
        s = s_ref[g, slot]
        if near:
            s = s + jnp.concatenate(
                [seld_ref[g, jnp.clip(t * nsub + u - i + 2, 0, 2)] for u in range(nsub)], axis=0)
        koff = pl.multiple_of(t * tk, tk)
        vt = jnp.concatenate([vst_ref[g * HEAD_DIM:(g + 1) * HEAD_DIM, pl.ds(koff, tk)],
                              jnp.ones((16, tk), BF16)], axis=0)
        m, acc = carry
        m_new = jnp.maximum(m, jnp.max(s, axis=0, keepdims=True))
        p = jnp.exp2(s - m_new).astype(BF16)
        return m_new, jnp.exp2(m - m_new) * acc + _dot(vt, p)

    def selected(qs):
        groups = range(N_KV)

        def make_pair(near):
            def pair(tt, carry):
                for g in groups:
                    scores(g, qs[g], 2 * tt + 1, 1)
                carry = [attend(g, 2 * tt, 0, near, carry[g]) for g in groups]
                for g in groups:
                    scores(g, qs[g], 2 * tt + 2, 0)
                return [attend(g, 2 * tt + 1, 1, near, carry[g]) for g in groups]
            return pair

        n_far_pairs = (jnp.maximum(i - 1, 0) // nsub) // 2
        init = [(jnp.full((1, cols), -jnp.inf, F32), jnp.zeros((HEAD_DIM + 16, cols), F32))
                for _ in groups]
        carry = lax.fori_loop(0, n_far_pairs, make_pair(False), init)
        carry = lax.fori_loop(n_far_pairs, n_trips // 2, make_pair(True), carry)
        carry = lax.cond(n_trips % 2 == 1,
                         lambda c: [attend(g, n_trips - 1, 0, True, c[g]) for g in groups],
                         lambda c: c, carry)
        return [acc[0:HEAD_DIM] * (1.0 / acc[HEAD_DIM:HEAD_DIM + 1]) for _, acc in carry]

    gt = gt_ref[...]

    def gate(g, j):
        return jnp.concatenate([gt[(g * GQA + r) * 3 + j:(g * GQA + r) * 3 + j + 1, :]
                                for r in range(GQA)], axis=1)

    qs = [query(g) for g in range(N_KV)]
    cmp_out, o_win = [], []
    for g in range(N_KV):
        cmp_out.append(compressed(g, qs[g]))
        o_win.append(window(g, qs[g]))
        select(g, cmp_out[g][1])
    for g in range(N_KV):
        scores(g, qs[g], 0, 0)
    o_sel = selected(qs)
    for g in range(N_KV):
        o = gate(g, 0) * cmp_out[g][0] + gate(g, 1) * o_sel[g] + gate(g, 2) * o_win[g]
        o_ref[:, g * gw:(g + 1) * gw] = jnp.concatenate(
            [o[:, r * Q_BLOCK:(r + 1) * Q_BLOCK].T for r in range(GQA)], axis=1).astype(BF16)


def _nsa_prompt(qt, gt, kc, vct, ks, vst, kw, vwt, cband, far, seld, wbias, cover_t, bsz, seq):
    nblk = seq // Q_BLOCK
    ncp = kc.shape[1]
    ns = cover_t.shape[0]
    cols = GQA * Q_BLOCK
    nsub = next(c for c in (4, 2, 1) if nblk % c == 0)
    full = lambda a: pl.BlockSpec(a.shape, lambda b, i: (0,) * a.ndim)
    in_specs = [
        pl.BlockSpec((D_B, Q_BLOCK), lambda b, i: (0, b * nblk + i)),
        pl.BlockSpec((GATE_ROWS, Q_BLOCK), lambda b, i: (0, b * nblk + i)),
        pl.BlockSpec((1, ncp, KV_W), lambda b, i: (b, 0, 0)),
        pl.BlockSpec((1, KV_W, ncp), lambda b, i: (b, 0, 0)),
        pl.BlockSpec((seq, KV_W), lambda b, i: (b, 0)),
        pl.BlockSpec((KV_W, seq), lambda b, i: (0, b)),
        pl.BlockSpec((seq, KV_W), lambda b, i: (b, 0)),
        pl.BlockSpec((KV_W, seq), lambda b, i: (0, b)),
        full(cband), full(far), full(seld),
        pl.BlockSpec((N_KV, 1) + wbias.shape[2:],
                     lambda b, i: (0, jnp.minimum(i, WIN_VARIANTS - 1), 0, 0)),
        full(cover_t),
    ]
    return pl.pallas_call(
        functools.partial(_nsa_prompt_kernel, nsub=nsub),
        grid=(bsz, nblk),
        in_specs=in_specs,
        out_specs=pl.BlockSpec((Q_BLOCK, D_B), lambda b, i: (b * nblk + i, 0)),
        out_shape=jax.ShapeDtypeStruct((bsz * seq, D_B), BF16),
        scratch_shapes=[pltpu.VMEM((N_KV, ncp, cols), F32), pltpu.VMEM((N_KV, ns, cols), F32),
                        pltpu.VMEM((N_KV, 2, nsub * CHUNK, cols), F32)],
        compiler_params=pltpu.CompilerParams(
            dimension_semantics=("parallel", "arbitrary"),
            vmem_limit_bytes=VMEM_LIMIT),
        name="nsa_prompt",
    )(qt, gt, kc, vct, ks, vst, kw, vwt, cband, far, seld, wbias, cover_t)


def _nsa_sample_kernel(tbl_ref, *refs, pgs, npg, past, ds):
    del tbl_ref
    page_refs = refs[:pgs]
    (qbd_ref, gt_ref, kc_ref, vct_ref, win_ref, new_ref, cbias_ref, wbias_ref, sfar_ref, slast_ref,
     snew_ref, cover_ref, rsum_ref, o_ref, msk_ref, m_ref, acc_ref, ocw_ref) = refs[pgs:]
    j = pl.program_id(1)
    qbd = qbd_ref[0]
    ncol = qbd.shape[1]
    qcols = ncol // (N_KV * GQA)
    g = gt_ref[0]

    @pl.when(j == 0)
    def _():
        p_c = _softmax_rows(_dot(kc_ref[0], qbd) + cbias_ref[...])
        o_c = _dot(vct_ref[0], p_c.astype(BF16))
        imp = _split_dot(_split_dot_l(cover_ref[...], p_c), rsum_ref[...])
        lane = lax.broadcasted_iota(jnp.int32, (1, ncol), 1)
        qpos = past + (lane % qcols) % ds
        msk_ref[...] = _select_mask(imp, qpos) + sfar_ref[...]
        kw = win_ref[0, :, 0:KV_W].astype(BF16)
        vw = win_ref[0, :, KV_W:KV_ROW].astype(BF16)
        p_w = _softmax_rows(_dot(kw, qbd) + wbias_ref[...])
        o_w = _dot_t(vw, p_w.astype(BF16))
        ocw_ref[...] = g[0:1, :] * o_c + g[2:3, :] * o_w
        m_ref[...] = jnp.full(m_ref.shape, -jnp.inf, F32)
        acc_ref[...] = jnp.zeros(acc_ref.shape, F32)

    last = j == pl.num_programs(1) - 1
    blocks, vts = [], []
    for k, pr in enumerate(page_refs):
        pidx = j * pgs + k
        kk = pr[0, 0:KV_W, :].T.astype(BF16)
        vts.append(pr[0, KV_W:KV_ROW, :].astype(BF16))
        s = _dot(kk, qbd)
        if k == pgs - 1:
            s = s + jnp.where(last, slast_ref[...], 0.0)
        for hb in range(2):
            blocks.append(s[hb * SEL_BLOCK:(hb + 1) * SEL_BLOCK, :]
                          + msk_ref[pl.ds(2 * pidx + hb, 1), :])
    top = blocks[0]
    for s in blocks[1:]:
        top = jnp.maximum(top, s)
    m = m_ref[...]
    m_new = jnp.maximum(m, jnp.max(top, axis=0, keepdims=True))
    p = jnp.concatenate([jnp.exp2(s - m_new).astype(BF16) for s in blocks], axis=0)
    vt = jnp.concatenate([jnp.concatenate(vts, axis=1), jnp.ones((16, pgs * PAGE_SIZE), BF16)], axis=0)
    acc_ref[...] = jnp.exp2(m - m_new) * acc_ref[...] + _dot(vt, p)
    m_ref[...] = m_new

    @pl.when(last)
    def _():
        kn = new_ref[0, :, 0:KV_W].astype(BF16)
        vn = new_ref[0, :, KV_W:KV_ROW].astype(BF16)
        s = _dot(kn, qbd) + snew_ref[...] + msk_ref[pl.ds(2 * npg, 1), :]
        m = m_ref[...]
        m_new = jnp.maximum(m, jnp.max(s, axis=0, keepdims=True))
        p = jnp.exp2(s - m_new).astype(BF16)
        acc = jnp.exp2(m - m_new) * acc_ref[...]
        num = acc[0:KV_W] + _dot_t(vn, p)
        den = acc[KV_W:KV_W + 1] + jnp.sum(p.astype(F32), axis=0, keepdims=True)
        o = ocw_ref[...] + g[1:2, :] * (num * (1.0 / den))
        for gi in range(N_KV):
            o_ref[0, gi] = o[gi * HEAD_DIM:(gi + 1) * HEAD_DIM, gi * GQA * qcols:(gi + 1) * GQA * qcols]


def _nsa_sample(table, pages, qbd, gt, kc, vct, win, new, cbias, wbias, sfar, slast, snew,
                cover_t, rsum, past, ds):
    nb, npg = table.shape
    pgs = min(64, npg)
    assert npg % pgs == 0
    ncol = qbd.shape[2]
    ns = cover_t.shape[0]

    def page_spec(k):
        return pl.BlockSpec((1, KV_ROW, PAGE_SIZE), lambda b, j, tbl: (tbl[b, j * pgs + k], 0, 0))

    per_b = lambda a: pl.BlockSpec((1,) + a.shape[1:], lambda b, j, tbl: (b,) + (0,) * (a.ndim - 1))
    full = lambda a: pl.BlockSpec(a.shape, lambda b, j, tbl: (0,) * a.ndim)
    grid_spec = pltpu.PrefetchScalarGridSpec(
        num_scalar_prefetch=1,
        grid=(nb, npg // pgs),
        in_specs=[page_spec(k) for k in range(pgs)]
        + [per_b(a) for a in (qbd, gt, kc, vct, win, new)]
        + [full(a) for a in (cbias, wbias, sfar, slast, snew, cover_t, rsum)],
        out_specs=pl.BlockSpec((1, N_KV, HEAD_DIM, ncol // N_KV), lambda b, j, tbl: (b, 0, 0, 0)),
        scratch_shapes=[pltpu.VMEM((ns, ncol), F32), pltpu.VMEM((1, ncol), F32),
                        pltpu.VMEM((KV_W + 16, ncol), F32), pltpu.VMEM((KV_W, ncol), F32)],
    )
    return pl.pallas_call(
        functools.partial(_nsa_sample_kernel, pgs=pgs, npg=npg, past=past, ds=ds),
        grid_spec=grid_spec,
        out_shape=jax.ShapeDtypeStruct((nb, N_KV, HEAD_DIM, ncol // N_KV), F32),
        compiler_params=pltpu.CompilerParams(dimension_semantics=("parallel", "arbitrary"),
                                             vmem_limit_bytes=VMEM_LIMIT),
        name="nsa_sample",
    )(table, *([pages] * pgs), qbd, gt, kc, vct, win, new, cbias, wbias, sfar, slast, snew,
      cover_t, rsum)


def _cover_t(nc, ns, nc_pad, ns_pad):
    c0 = np.arange(nc) * CMP_STRIDE
    s0 = np.arange(ns) * SEL_BLOCK
    m = (c0[None, :] < s0[:, None] + SEL_BLOCK) & (c0[None, :] + CMP_BLOCK > s0[:, None])
    out = np.zeros((ns_pad, nc_pad), np.float32)
    out[:ns, :nc] = m
    return jnp.asarray(out, BF16)


def _prep_weights(norm_g, w_in, ln_v_g, spatial_w, spatial_b, cmp_pe, cmp_w1, cmp_b1, cmp_w2, cmp_b2,
                  w_out, final_g):
    offs = np.cumsum((D_A, D_A, D_A, D_B, 2 * KV_W, 2 * KV_W, 2 * KV_W, 3 * N_HEADS, D_B))
    g0, g1 = int(offs[6]), int(offs[7])
    w_perm = jnp.concatenate(
        [w_in[:, :g0], w_in[:, g1:], w_in[:, g0:g1],
         jnp.zeros((D_MODEL, GATE_PAD - 3 * N_HEADS), w_in.dtype)], axis=1).astype(BF16)
    pavg = jnp.asarray(np.kron(np.eye(A_GROUPS), np.full((HEAD_DIM, HEAD_DIM), 1.0 / HEAD_DIM)), BF16)
    w1cat = jnp.concatenate([cmp_w1[:, s].reshape(2, CMP_FLAT, CMP_HIDDEN) for s in range(CMP_R)],
                            axis=2).astype(BF16)
    pe8 = jnp.concatenate([cmp_pe.reshape(2, CMP_R, CMP_FLAT),
                           jnp.zeros((2, 8 - CMP_R, CMP_FLAT), F32)], axis=1)
    return dict(
        norm_g=norm_g.reshape(1, D_MODEL), w_perm=w_perm, ln_g=ln_v_g.reshape(1, D_A), pavg=pavg,
        w1cat=w1cat, pe8=pe8, b1=cmp_b1.reshape(2, 1, CMP_HIDDEN), w2=cmp_w2.astype(BF16),
        b2=cmp_b2.reshape(2, 1, HEAD_DIM), wo=w_out.astype(BF16), fg=final_g.reshape(1, D_MODEL),
        spatial_w=spatial_w, spatial_b=spatial_b)


def _spatial_operands(spatial_w, spatial_b, n):
    reps = CHUNK // n
    w = jnp.tril(spatial_w[:, :n, :n])
    eye = jnp.eye(reps, dtype=w.dtype)
    wsp = jnp.einsum('ab,gts->gatbs', eye, w).reshape(A_GROUPS, CHUNK, CHUNK).astype(BF16)
    b = jnp.tile(spatial_b[:, :n].T, (reps, 1))
    bsp = jnp.repeat(b, HEAD_DIM, axis=1)
    return wsp, bsp


def _prompt_bias_buckets():
    ql = np.arange(Q_BLOCK)[None, :]
    cl = np.arange(16)[:, None]
    band = np.stack([ql - CMP_STRIDE * cl - (CMP_BLOCK - 1),
                     ql + 97 - CMP_STRIDE * cl])
    kl = np.arange(CHUNK)[:, None]
    seld = np.stack([np.full((CHUNK, Q_BLOCK), FAR_DIST), CHUNK + ql - kl, ql - kl])
    wl = np.arange(WINDOW + Q_BLOCK)[:, None]
    dw = np.stack([Q_BLOCK * v + ql - wl for v in range(WIN_VARIANTS)])
    dw = np.where(dw < WINDOW, dw, -1)
    far = np.full((8, Q_BLOCK), FAR_DIST)
    tile4 = lambda d: np.tile(_t5_bucket_np(d.reshape(-1, Q_BLOCK)), (1, GQA))
    return tile4(band), tile4(seld), tile4(dw), tile4(far)


def _sample_bias_buckets(past, ds, qpad, ncp, nwin_pad):
    ql = (np.arange(qpad) % ds)[None, :]
    c = np.arange(ncp)[:, None]
    dc = past + ql - (CMP_STRIDE * c + CMP_BLOCK - 1)
    wl = np.arange(nwin_pad)[:, None]
    dw = WINDOW + ql - wl
    dw = np.where((dw < WINDOW) & (wl < WINDOW + ds), dw, -1)
    kl = np.arange(PAGE_SIZE)[:, None]
    dlast = PAGE_SIZE + ql - kl
    nl = np.arange(16)[:, None]
    dnew = np.where(nl < ds, ql - nl, -1)
    far = np.full((8, qpad), FAR_DIST)
    tile4 = lambda d: np.tile(_t5_bucket_np(d), (1, GQA))
    return tile4(dc), tile4(dw), tile4(dlast), tile4(dnew), tile4(far)


def kernel(x_prompt, x_sample, cache_cmp_kv, cache_sel_kv, state_win_kv, page_table, norm_g, w_in,
           ln_v_g, spatial_w, spatial_b, cmp_pe, cmp_w1, cmp_b1, cmp_w2, cmp_b2, rel_table, w_out,
           final_g):
    depth = norm_g.shape[0]
    assert depth == 1
    bsz, seq = x_prompt.shape[:2]
    db, ds = x_sample.shape[:2]
    npg = page_table.shape[1]
    past = npg * PAGE_SIZE
    win_buf = state_win_kv.shape[2]
    assert win_buf == WINDOW and past >= WINDOW and seq % Q_BLOCK == 0 and seq >= WINDOW + Q_BLOCK
    assert CHUNK % ds == 0 and ds <= 8 and (db * ds) % CHUNK == 0
    nblk = seq // Q_BLOCK
    l = 0
    wts = _prep_weights(norm_g[l], w_in[l], ln_v_g[l], spatial_w[l], spatial_b[l], cmp_pe[l],
                        cmp_w1[l], cmp_b1[l], cmp_w2[l], cmp_b2[l], w_out[l], final_g)
    inproj = lambda x, nb: _inproj(x, nb, wts['norm_g'], wts['w_perm'], wts['ln_g'], wts['pavg'])
    compress = lambda tbl, pages: _compress(tbl, pages, wts['w1cat'], wts['pe8'], wts['b1'],
                                            wts['w2'], wts['b2'])
    kv_out = lambda a: a.reshape(a.shape[0], 2, N_KV, HEAD_DIM, a.shape[2]).transpose(0, 4, 1, 2, 3)[None]
    pages_t = lambda c: c.transpose(0, 2, 3, 4, 1).reshape(c.shape[0], KV_ROW, PAGE_SIZE)

    n_p = bsz * seq
    xp = x_prompt.reshape(n_p, D_MODEL)
    ua, _, vnb, zas, zbs, kvc, kvs, kvw, qt, gt, ksk, vst, kwk, vwt = inproj(xp, bsz)
    kc, vct = compress(None, kvc)
    ncp = seq // CMP_STRIDE
    cover_p = _cover_t(ncp - CMP_R + 1, seq // SEL_BLOCK, ncp, seq // SEL_BLOCK)
    band_b, seld_b, win_b, far_b = _prompt_bias_buckets()
    cols = GQA * Q_BLOCK
    cband = _bias_tiles(rel_table, band_b, Q_BLOCK).reshape(N_KV, 2, 16, cols)
    seld = _bias_tiles(rel_table, seld_b, Q_BLOCK, rel_far=True).reshape(N_KV, 3, CHUNK, cols)
    wbias = _bias_tiles(rel_table, win_b, Q_BLOCK).reshape(N_KV, WIN_VARIANTS, WINDOW + Q_BLOCK, cols)
    far = _bias_tiles(rel_table, far_b, Q_BLOCK)[:, 0:1]
    ob = _nsa_prompt(qt, gt, kc, vct, ksk, vst, kwk, vwt, cband, far, seld, wbias, cover_p, bsz, seq)
    wsp, bsp = _spatial_operands(wts['spatial_w'], wts['spatial_b'], CHUNK)
    y_prompt = _mixout(xp, ua, vnb, zas, ob, zbs, wsp, bsp, wts['wo'], wts['fg']).reshape(bsz, seq, D_MODEL)
    new_cmp_p = kv_out(kvc)
    new_sel_p = kv_out(kvs)
    new_win_p = kv_out(kvw[:, :, seq - win_buf:])

    n_s = db * ds
    xs = x_sample.reshape(n_s, D_MODEL)
    ua, vn, vnb, zas, zbs, kvc, kvs, kvw, qt, gt, _, _, _, _ = inproj(xs, 1)
    kvc, kvs, kvw = (a[0].T for a in (kvc, kvs, kvw))
    kc, vct = compress(page_table, pages_t(cache_cmp_kv[l]))
    ncs = past // CMP_STRIDE
    t_all = past + ds
    nss = -(-t_all // SEL_BLOCK)
    nss_pad = -(-nss // 8) * 8
    cover_s = _cover_t(t_all // CMP_STRIDE - CMP_R + 1, nss, ncs, nss_pad)
    qpad = 16
    ncol = N_KV * GQA * qpad
    nwin_pad = -(-(win_buf + ds) // 16) * 16
    dc_b, dw_b, dl_b, dn_b, far_b = _sample_bias_buckets(past, ds, qpad, ncs, nwin_pad)
    both = lambda t: jnp.concatenate([t[0], t[1]], axis=1)
    cbias_s = both(_bias_tiles(rel_table, dc_b, qpad))
    wbias_s = both(_bias_tiles(rel_table, dw_b, qpad))
    slast = both(_bias_tiles(rel_table, dl_b, qpad, rel_far=True))
    snew = both(_bias_tiles(rel_table, dn_b, qpad, rel_far=True))
    sfar = both(_bias_tiles(rel_table, far_b, qpad))[0:1]

    qg = qt.reshape(N_KV, GQA, HEAD_DIM, db, ds).transpose(3, 0, 2, 1, 4)
    qg = jnp.tile(qg, (1, 1, 1, 1, qpad // ds)).reshape(db, N_KV, HEAD_DIM, GQA * qpad)
    zq = jnp.zeros_like(qg[:, 0])
    qbd = jnp.concatenate([jnp.concatenate([qg[:, 0], zq], axis=2),
                           jnp.concatenate([zq, qg[:, 1]], axis=2)], axis=1)
    gts = gt[:3 * N_HEADS].reshape(N_KV, GQA, 3, db, ds).transpose(3, 2, 0, 1, 4)
    gts = jnp.tile(gts, (1, 1, 1, 1, qpad // ds)).reshape(db, 3, ncol)
    win = jnp.concatenate([state_win_kv[l].reshape(db, win_buf, KV_ROW),
                           kvw.reshape(db, ds, KV_ROW)], axis=1)
    win_pad = jnp.pad(win, ((0, 0), (0, nwin_pad - win_buf - ds), (0, 0)))
    new_pad = jnp.pad(kvs.reshape(db, ds, KV_ROW), ((0, 0), (0, 16 - ds), (0, 0)))
    rsum = jnp.asarray(np.kron(np.eye(N_KV), np.kron(np.ones((GQA, GQA)), np.eye(qpad))), BF16)
    o_s = _nsa_sample(page_table, pages_t(cache_sel_kv[l]), qbd, gts,
                      kc, vct, win_pad, new_pad, cbias_s, wbias_s, sfar, slast, snew,
                      cover_s, rsum, past, ds)
    ob = (o_s.reshape(db, N_KV, HEAD_DIM, GQA, qpad)[..., :ds].transpose(0, 4, 1, 3, 2)
          .reshape(n_s, D_B))
    wsp, bsp = _spatial_operands(wts['spatial_w'], wts['spatial_b'], ds)
    y_sample = _mixout(xs, ua, vnb, zas, ob, zbs, wsp, bsp, wts['wo'], wts['fg']).reshape(db, ds, D_MODEL)
    kv5 = lambda a: a.reshape(1, db, ds, 2, N_KV, HEAD_DIM)
    new_cmp_s = kv5(kvc)
    new_sel_s = kv5(kvs)
    new_win_s = win[:, ds:].reshape(1, db, win_buf, 2, N_KV, HEAD_DIM)
    new_chunk_v = vn.reshape(1, db, ds, D_A)
    return (y_prompt, y_sample, new_cmp_p, new_sel_p, new_win_p, new_cmp_s, new_sel_s, new_win_s,
            new_chunk_v)
```
